```python
import math
import jax, jax.numpy as jnp
from jax import lax
import numpy as np

D_MODEL = 1024
BATCH = 8
SEQ = 4096
DEPTH = 2

GRID_W = 64
CTX_LEN = 256

BRANCH_W = 512
N_BRANCH = 3

ATT_HEADS = 4
ATT_DH = 64
ATT_DV = 2 * ATT_DH
ATT_QK = ATT_HEADS * 2 * ATT_DH
Q_BLOCK = 128
ROPE_BASE = 10000.0

LRU_BLOCKS = 8
LRU_BS = BRANCH_W // LRU_BLOCKS
LRU_C = 8.0
CONV_W = 4

SSD_HEADDIM = 64
SSD_HEADS = BRANCH_W // SSD_HEADDIM
SSD_GROUPS = 2
SSD_HPG = SSD_HEADS // SSD_GROUPS
SSD_STATE = 128
SSD_CHUNK = 128
SSD_XBC = BRANCH_W + 2 * SSD_GROUPS * SSD_STATE
N_DIR = 2

N_EXPERTS = 64
TOP_K = 8
N_GROUPS = 8
TOPK_GROUPS = 4
EXPERT_F = 256
SHARED_F = 256
ROUTED_SCALE = 2.5
MOE_BLOCK = 128

DN_ALPHA = (2 * DEPTH) ** 0.25
DN_BETA = (8 * DEPTH) ** -0.25
LN_EPS = 1e-5
RMS_EPS = 1e-6

IN_WIDTHS = (ATT_QK, ATT_QK, ATT_HEADS * ATT_DV,
             BRANCH_W, BRANCH_W,
             BRANCH_W, SSD_XBC, N_DIR * SSD_HEADS,
             N_BRANCH * D_MODEL)
IN_SPLITS = tuple(int(s) for s in np.cumsum(IN_WIDTHS)[:-1])
IN_TOTAL = int(sum(IN_WIDTHS))

kernel_name = 'hybrid_dit_diffattn_rglru_ssd_moe'


def layer_norm(x, g, b):
    xf = x.astype(jnp.float32)
    mu = jnp.mean(xf, axis=-1, keepdims=True)
    var = jnp.mean(jnp.square(xf - mu), axis=-1, keepdims=True)
    return ((xf - mu) * lax.rsqrt(var + LN_EPS)).astype(x.dtype) * g + b


def rms_norm(x, g):
    xf = x.astype(jnp.float32)
    return (xf * lax.rsqrt(jnp.mean(xf * xf, axis=-1, keepdims=True) + RMS_EPS)).astype(x.dtype) * g


def axial_rope(n_tokens):
    rows = n_tokens // GRID_W
    row = jnp.repeat(jnp.arange(rows, dtype=jnp.float32), GRID_W)
    col = (jnp.arange(n_tokens) % GRID_W).astype(jnp.float32)
    n_freq = ATT_DH // 4
    inv = ROPE_BASE ** (-jnp.arange(n_freq, dtype=jnp.float32) / n_freq)
    ang = jnp.concatenate([row[:, None] * inv, col[:, None] * inv], axis=-1)
    return jnp.cos(ang), jnp.sin(ang)


def apply_rope(x, cos, sin):
    x1, x2 = jnp.split(x, 2, axis=-1)
    c = cos[:, None, None, :].astype(x.dtype)
    s = sin[:, None, None, :].astype(x.dtype)
    return jnp.concatenate([x1 * c - x2 * s, x2 * c + x1 * s], axis=-1)


def diff_attn_block(q, k, v, lam):
    s = jnp.einsum('bqhmd,bkhmd->bhmqk', q, k).astype(jnp.float32) * (ATT_DH ** -0.5)
    p = jax.nn.softmax(s, axis=-1)
    p = p[:, :, 0] - lam * p[:, :, 1]
    return jnp.einsum('bhqk,bkhe->bqhe', p.astype(v.dtype), v)


def diff_attn_sweep(q, k, v, lam):
    B, L = q.shape[:2]
    qb = q.reshape(B, L // Q_BLOCK, Q_BLOCK, *q.shape[2:]).swapaxes(0, 1)
    ob = lax.map(lambda qi: diff_attn_block(qi, k, v, lam), qb)
    return ob.swapaxes(0, 1).reshape(B, L, *ob.shape[3:])


def diff_attn_head_norm(o, g, lam_init):
    B, L = o.shape[:2]
    return (rms_norm(o, g) * (1.0 - lam_init)).reshape(B, L, -1)


def dwconv_centred(x, w, bias):
    L = x.shape[1]
    left = CONV_W // 2
    xp = jnp.pad(x, ((0, 0), (left, CONV_W - 1 - left), (0, 0)))
    y = xp[:, 0:L] * w[0]
    for j in range(1, CONV_W):
        y = y + xp[:, j:j + L] * w[j]
    return y + bias


def _lin_combine(e1, e2):
    a1, b1 = e1
    a2, b2 = e2
    return a1 * a2, a2 * b1 + b2


def linear_scan(a, b, h0, reverse):
    if reverse:
        b = b.at[:, -1].add(a[:, -1] * h0)
    else:
        b = b.at[:, 0].add(a[:, 0] * h0)
    _, h = lax.associative_scan(_lin_combine, (a, b), reverse=reverse, axis=1)
    return h


def rglru_coeffs(u, w_a, b_a, w_i, b_i, lam):
    B, L, W = u.shape
    ub = u.reshape(B, L, LRU_BLOCKS, LRU_BS)
    r = jax.nn.sigmoid(jnp.einsum('blnk,nkj->blnj', ub, w_a).reshape(B, L, W) + b_a)
    i = jax.nn.sigmoid(jnp.einsum('blnk,nkj->blnj', ub, w_i).reshape(B, L, W) + b_i)
    log_a = -LRU_C * r * jax.nn.softplus(-lam)
    return jnp.exp(log_a), jnp.sqrt(1.0 - jnp.exp(2.0 * log_a)) * (i * u)


def rglru_bidir(u, uc, w_a, b_a, w_i, b_i, lam):
    u = u.astype(jnp.float32)
    uc = uc.astype(jnp.float32)
    y = jnp.zeros_like(u)
    yc = jnp.zeros_like(uc)
    for d in range(N_DIR):
        rev = d == 1
        a_c, b_c = rglru_coeffs(uc, w_a[d], b_a[d], w_i[d], b_i[d], lam[d])
        h_c = linear_scan(a_c, b_c, jnp.zeros_like(b_c[:, 0]), rev)
        h_end = h_c[:, 0] if rev else h_c[:, -1]
        a_l, b_l = rglru_coeffs(u, w_a[d], b_a[d], w_i[d], b_i[d], lam[d])
        y = y + linear_scan(a_l, b_l, h_end, rev)
        yc = yc + h_c
    return y, yc


def segsum(x):
    T = x.shape[-1]
    xe = jnp.broadcast_to(x[..., :, None], x.shape + (T,))
    strict = jnp.tril(jnp.ones((T, T), dtype=bool), -1)
    cs = jnp.cumsum(jnp.where(strict, xe, 0.0), axis=-2)
    return jnp.where(jnp.tril(jnp.ones((T, T), dtype=bool)), cs, -jnp.inf)


def ssd_chunked(xdt, a, bm, cm, h0):
    B, T, G, R, P = xdt.shape
    nc = T // SSD_CHUNK
    xdt = xdt.reshape(B, nc, SSD_CHUNK, G, R, P)
    bm = bm.reshape(B, nc, SSD_CHUNK, G, -1)
    cm = cm.reshape(B, nc, SSD_CHUNK, G, -1)
    a = a.reshape(B, nc, SSD_CHUNK, G, R).transpose(0, 3, 4, 1, 2)
    a_cs = jnp.cumsum(a, axis=-1)
    decay_in = jnp.exp(segsum(a))
    cb = jnp.einsum('bclgn,bcsgn->bcgls', cm, bm)
    y_diag = jnp.einsum('bcgls,bgrcls,bcsgrp->bclgrp', cb, decay_in, xdt)
    decay_to_end = jnp.exp(a_cs[..., -1:] - a_cs)
    states = jnp.einsum('bcsgn,bgrcs,bcsgrp->bcgrpn', bm, decay_to_end, xdt)
    states = jnp.concatenate([h0[:, None], states], axis=1)
    chunk_a = jnp.pad(a_cs[..., -1], ((0, 0), (0, 0), (0, 0), (1, 0)))
    states = jnp.einsum('bgrzc,bcgrpn->bzgrpn', jnp.exp(segsum(chunk_a)), states)
    y_off = jnp.einsum('bclgn,bcgrpn,bgrcl->bclgrp', cm, states[:, :-1], jnp.exp(a_cs))
    return (y_diag + y_off).reshape(B, T, G, R, P), states[:, -1]


def ssd_direction(xs, bm, cm, dt, A, h0, reverse):
    if reverse:
        xs, bm, cm, dt = xs[:, ::-1], bm[:, ::-1], cm[:, ::-1], dt[:, ::-1]
    y, h_last = ssd_chunked(xs * dt[..., None], dt * A, bm, cm, h0)
    return (y[:, ::-1] if reverse else y), h_last


def ssd_prep(xbc, dt_raw, conv_w, conv_b):
    u = jax.nn.silu(dwconv_centred(xbc, conv_w, conv_b)).astype(jnp.float32)
    B, L = u.shape[:2]
    xs, bm, cm = jnp.split(u, [BRANCH_W, BRANCH_W + SSD_GROUPS * SSD_STATE], axis=-1)
    return (xs.reshape(B, L, SSD_GROUPS, SSD_HPG, SSD_HEADDIM),
            bm.reshape(B, L, SSD_GROUPS, SSD_STATE),
            cm.reshape(B, L, SSD_GROUPS, SSD_STATE),
            dt_raw.astype(jnp.float32).reshape(B, L, N_DIR, SSD_GROUPS, SSD_HPG))


def ssd_bidir(xbc, dt_raw, xbcc, dtc_raw, conv_w, conv_b, dt_bias, a_log, d_skip):
    xs, bm, cm, dt = ssd_prep(xbc, dt_raw, conv_w, conv_b)
    xsc, bmc, cmc, dtc = ssd_prep(xbcc, dtc_raw, conv_w, conv_b)
    B, L = xs.shape[:2]
    Lc = xsc.shape[1]
    dtb = dt_bias.astype(jnp.float32).reshape(N_DIR, SSD_GROUPS, SSD_HPG)
    A = -jnp.exp(a_log.astype(jnp.float32)).reshape(N_DIR, SSD_GROUPS, SSD_HPG)
    skip = d_skip.astype(jnp.float32).reshape(SSD_GROUPS, SSD_HPG, 1)
    y = skip * xs
    yc = skip * xsc
    h0 = jnp.zeros((B, SSD_GROUPS, SSD_HPG, SSD_HEADDIM, SSD_STATE), jnp.float32)
    for d in range(N_DIR):
        rev = d == 1
        dt_c = jax.nn.softplus(dtc[:, :, d] + dtb[d])
        dt_l = jax.nn.softplus(dt[:, :, d] + dtb[d])
        y_c, h_ctx = ssd_direction(xsc, bmc, cmc, dt_c, A[d], h0, rev)
        y_l, _ = ssd_direction(xs, bm, cm, dt_l, A[d], h_ctx, rev)
        y = y + y_l
        yc = yc + y_c
    return y.reshape(B, L, BRANCH_W), yc.reshape(B, Lc, BRANCH_W)


def gated_group_rmsnorm(y, z, g):
    B, L = y.shape[:2]
    t = (y * jax.nn.silu(z.astype(jnp.float32))).reshape(B, L, SSD_GROUPS, BRANCH_W // SSD_GROUPS)
    t = t * lax.rsqrt(jnp.mean(t * t, axis=-1, keepdims=True) + RMS_EPS)
    return (t.reshape(B, L, BRANCH_W) * g).astype(z.dtype)


def merge_branches(ya, yb, yc, gates, w_branch, w_out):
    ys = jnp.stack([ya, yb, yc], axis=2)
    proj = jnp.einsum('blnk,nkd->blnd', ys, w_branch)
    g = jax.nn.sigmoid(gates.reshape(*gates.shape[:2], N_BRANCH, D_MODEL).astype(jnp.float32))
    return jnp.einsum('blnd,de->ble', g.astype(proj.dtype) * proj, w_out)


def token_mixers(h, hc, cos, sin, w_in, lam_q, lam_k, lam_init, attn_g,
                 lru_cw, lru_cb, lru_wa, lru_ba, lru_wi, lru_bi, lru_lam,
                 ssd_cw, ssd_cb, ssd_dtb, ssd_alog, ssd_d, ssd_g,
                 w_branch, w_out, need_ctx):
    B, L, _ = h.shape
    Lc = hc.shape[1]
    q, k, v, lx, lg, z, xbc, dt, gates = jnp.split(h @ w_in, IN_SPLITS, axis=-1)
    qc, kc, vc, lxc, lgc, zc, xbcc, dtc, gatesc = jnp.split(hc @ w_in, IN_SPLITS, axis=-1)

    lam = (jnp.exp(jnp.sum(lam_q[0] * lam_k[0])) - jnp.exp(jnp.sum(lam_q[1] * lam_k[1]))).astype(jnp.float32) + lam_init
    q = apply_rope(q.reshape(B, L, ATT_HEADS, 2, ATT_DH), cos, sin)
    k = apply_rope(k.reshape(B, L, ATT_HEADS, 2, ATT_DH), cos, sin)
    kc = kc.reshape(B, Lc, ATT_HEADS, 2, ATT_DH)
    vc = vc.reshape(B, Lc, ATT_HEADS, ATT_DV)
    k_all = jnp.concatenate([k, kc], axis=1)
    v_all = jnp.concatenate([v.reshape(B, L, ATT_HEADS, ATT_DV), vc], axis=1)
    ya = diff_attn_head_norm(diff_attn_sweep(q, k_all, v_all, lam), attn_g, lam_init)

    yb, ybc = rglru_bidir(dwconv_centred(lx, lru_cw, lru_cb), dwconv_centred(lxc, lru_cw, lru_cb),
                          lru_wa, lru_ba, lru_wi, lru_bi, lru_lam)
    yb = yb.astype(h.dtype) * jax.nn.gelu(lg)

    yc, ycc = ssd_bidir(xbc, dt, xbcc, dtc, ssd_cw, ssd_cb, ssd_dtb, ssd_alog, ssd_d)
    yc = gated_group_rmsnorm(yc, z, ssd_g)

    out = merge_branches(ya, yb, yc, gates, w_branch, w_out)
    if not need_ctx:
        return out, None
    qc = qc.reshape(B, Lc, ATT_HEADS, 2, ATT_DH)
    yac = diff_attn_head_norm(diff_attn_block(qc, kc, vc, lam), attn_g, lam_init)
    ybc = ybc.astype(hc.dtype) * jax.nn.gelu(lgc)
    ycc = gated_group_rmsnorm(ycc, zc, ssd_g)
    return out, merge_branches(yac, ybc, ycc, gatesc, w_branch, w_out)


def swiglu(h, w_up, w_down):
    g, u = jnp.split(h @ w_up, 2, axis=-1)
    return (jax.nn.silu(g) * u) @ w_down


def routed_experts(h, eidx, w, w_up, w_down):
    N, D = h.shape
    NK = N * TOP_K
    nb = (NK + N_EXPERTS * (MOE_BLOCK - 1) + MOE_BLOCK - 1) // MOE_BLOCK
    P = nb * MOE_BLOCK
    e_flat = eidx.reshape(-1)
    tok_flat = jnp.repeat(jnp.arange(N, dtype=jnp.int32), TOP_K)
    w_flat = w.reshape(-1)
    counts = jnp.zeros((N_EXPERTS,), jnp.int32).at[e_flat].add(1)
    padded = (counts + MOE_BLOCK - 1) // MOE_BLOCK * MOE_BLOCK
    pend = jnp.cumsum(padded)
    pstart = pend - padded
    start = jnp.cumsum(counts) - counts
    order = jnp.argsort(e_flat)
    se = e_flat[order]
    dest = pstart[se] + jnp.arange(NK, dtype=jnp.int32) - start[se]
    row_tok = jnp.full((P,), N, jnp.int32).at[dest].set(tok_flat[order])
    row_w = jnp.zeros((P,), w.dtype).at[dest].set(w_flat[order])
    blk_e = jnp.minimum(jnp.searchsorted(pend, jnp.arange(nb, dtype=jnp.int32) * MOE_BLOCK, side='right'),
                        N_EXPERTS - 1)
    h_pad = jnp.concatenate([h, jnp.zeros((1, D), h.dtype)], axis=0)

    def expert_block(args):
        e, toks, ws = args
        return swiglu(h_pad[toks], w_up[e], w_down[e]) * ws[:, None].astype(h.dtype)

    ys = lax.map(expert_block, (blk_e, row_tok.reshape(nb, MOE_BLOCK), row_w.reshape(nb, MOE_BLOCK)))
    return jax.ops.segment_sum(ys.reshape(P, D), row_tok, num_segments=N + 1)[:N]


def moe_ffn(h, w_router, router_bias, w_up, w_down, ws_up, ws_down):
    N = h.shape[0]
    scores = jax.nn.sigmoid((h @ w_router).astype(jnp.float32))
    sel = scores + router_bias
    grp_score = lax.top_k(sel.reshape(N, N_GROUPS, N_EXPERTS // N_GROUPS), 2)[0].sum(-1)
    _, gidx = lax.top_k(grp_score, TOPK_GROUPS)
    gmask = jax.nn.one_hot(gidx, N_GROUPS).sum(-2) > 0
    sel = jnp.where(jnp.repeat(gmask, N_EXPERTS // N_GROUPS, axis=-1), sel, -jnp.inf)
    _, eidx = lax.top_k(sel, TOP_K)
    w = jnp.take_along_axis(scores, eidx, axis=-1)
    w = w / jnp.sum(w, axis=-1, keepdims=True) * ROUTED_SCALE
    return swiglu(h, ws_up, ws_down) + routed_experts(h, eidx, w, w_up, w_down)


def setup_inputs(seed: int = 0) -> dict:
    key = jax.random.key(seed)
    kit = iter(jax.random.split(key, 64))

    def nrm(shape, scale):
        return jax.random.normal(next(kit), shape, jnp.float32) * scale

    L, D = DEPTH, D_MODEL
    v0 = 2 * ATT_QK
    col_scale = np.ones((IN_TOTAL,), np.float32)
    col_scale[v0:v0 + ATT_HEADS * ATT_DV] = DN_BETA
    s_lru = jax.random.uniform(next(kit), (L, N_DIR, BRANCH_W), jnp.float32,
                               minval=0.9, maxval=0.999) ** (1.0 / LRU_C)
    dt0 = jnp.exp(jax.random.uniform(next(kit), (L, N_DIR, SSD_HEADS), jnp.float32,
                                     minval=math.log(1e-3), maxval=math.log(1e-1)))
    a0 = jax.random.uniform(next(kit), (L, N_DIR, SSD_HEADS), jnp.float32, minval=1.0, maxval=16.0)
    return {
        'x': nrm((BATCH, SEQ, D), 1.0),
        'c': nrm((BATCH, D), 1.0),
        'ctx': nrm((BATCH, CTX_LEN, D), 1.0),
        'c_ctx': nrm((D,), 1.0),
        'w_mod': nrm((L, D, 6 * D), 0.5 * D ** -0.5),
        'b_mod': nrm((L, 6 * D), 0.01),
        'w_in': nrm((L, D, IN_TOTAL), D ** -0.5) * jnp.asarray(col_scale),
        'lam_q': nrm((L, 2, ATT_DH), 0.1),
        'lam_k': nrm((L, 2, ATT_DH), 0.1),
        'attn_norm_g': 1.0 + nrm((L, ATT_HEADS, ATT_DV), 0.02),
        'lru_conv_w': nrm((L, CONV_W, BRANCH_W), CONV_W ** -0.5),
        'lru_conv_b': nrm((L, BRANCH_W), 0.01),
        'lru_wa': nrm((L, N_DIR, LRU_BLOCKS, LRU_BS, LRU_BS), LRU_BS ** -0.5),
        'lru_ba': nrm((L, N_DIR, BRANCH_W), 0.01),
        'lru_wi': nrm((L, N_DIR, LRU_BLOCKS, LRU_BS, LRU_BS), LRU_BS ** -0.5),
        'lru_bi': nrm((L, N_DIR, BRANCH_W), 0.01),
        'lru_lambda': jnp.log(s_lru) - jnp.log1p(-s_lru),
        'ssd_conv_w': nrm((L, CONV_W, SSD_XBC), CONV_W ** -0.5),
        'ssd_conv_b': nrm((L, SSD_XBC), 0.01),
        'ssd_dt_bias': dt0 + jnp.log(-jnp.expm1(-dt0)),
        'ssd_a_log': jnp.log(a0),
        'ssd_d': 1.0 + nrm((L, SSD_HEADS), 0.02),
        'ssd_norm_g': 1.0 + nrm((L, BRANCH_W), 0.02),
        'w_branch': nrm((L, N_BRANCH, BRANCH_W, D), BRANCH_W ** -0.5 * DN_BETA),
        'w_out': nrm((L, D, D), D ** -0.5 * DN_BETA),
        'ln1_g': 1.0 + nrm((L, D), 0.02),
        'ln1_b': nrm((L, D), 0.01),
        'w_router': nrm((L, D, N_EXPERTS), D ** -0.5),
        'router_bias': nrm((L, N_EXPERTS), 0.01),
        'w_up': nrm((L, N_EXPERTS, D, 2 * EXPERT_F), D ** -0.5),
        'w_down': nrm((L, N_EXPERTS, EXPERT_F, D), EXPERT_F ** -0.5 * DN_BETA),
        'ws_up': nrm((L, D, 2 * SHARED_F), D ** -0.5),
        'ws_down': nrm((L, SHARED_F, D), SHARED_F ** -0.5 * DN_BETA),
        'ln2_g': 1.0 + nrm((L, D), 0.02),
        'ln2_b': nrm((L, D), 0.01),
    }


def reference(x, c, ctx, c_ctx, w_mod, b_mod, w_in, lam_q, lam_k, attn_norm_g,
              lru_conv_w, lru_conv_b, lru_wa, lru_ba, lru_wi, lru_bi, lru_lambda,
              ssd_conv_w, ssd_conv_b, ssd_dt_bias, ssd_a_log, ssd_d, ssd_norm_g,
              w_branch, w_out, ln1_g, ln1_b, w_router, router_bias, w_up, w_down,
              ws_up, ws_down, ln2_g, ln2_b):
    B, L, D = x.shape
    Lc = ctx.shape[1]
    cos, sin = axial_rope(L)
    s_c = jax.nn.silu(c)
    s_cc = jax.nn.silu(c_ctx)
    for l in range(DEPTH):
        last = l == DEPTH - 1
        lam_init = 0.8 - 0.6 * math.exp(-0.3 * l)
        m = jnp.split((s_c @ w_mod[l] + b_mod[l])[:, None, :], 6, axis=-1)
        mc = jnp.split(s_cc @ w_mod[l] + b_mod[l], 6, axis=-1)
        h = x * (1.0 + m[1]) + m[0]
        hc = ctx * (1.0 + mc[1]) + mc[0]
        mix, mix_c = token_mixers(
            h, hc, cos, sin, w_in[l], lam_q[l], lam_k[l], lam_init, attn_norm_g[l],
            lru_conv_w[l], lru_conv_b[l], lru_wa[l], lru_ba[l], lru_wi[l], lru_bi[l], lru_lambda[l],
            ssd_conv_w[l], ssd_conv_b[l], ssd_dt_bias[l], ssd_a_log[l], ssd_d[l], ssd_norm_g[l],
            w_branch[l], w_out[l], not last)
        x = layer_norm(DN_ALPHA * x + m[2] * mix, ln1_g[l], ln1_b[l])
        h2 = (x * (1.0 + m[4]) + m[3]).reshape(B * L, D)
        if last:
            f = moe_ffn(h2, w_router[l], router_bias[l], w_up[l], w_down[l], ws_up[l], ws_down[l])
        else:
            ctx = layer_norm(DN_ALPHA * ctx + mc[2] * mix_c, ln1_g[l], ln1_b[l])
            hc2 = (ctx * (1.0 + mc[4]) + mc[3]).reshape(B * Lc, D)
            f_all = moe_ffn(jnp.concatenate([h2, hc2], axis=0), w_router[l], router_bias[l],
                            w_up[l], w_down[l], ws_up[l], ws_down[l])
            f = f_all[:B * L]
            ctx = layer_norm(DN_ALPHA * ctx + mc[5] * f_all[B * L:].reshape(B, Lc, D), ln2_g[l], ln2_b[l])
        x = layer_norm(DN_ALPHA * x + m[5] * f.reshape(B, L, D), ln2_g[l], ln2_b[l])
    return x
```

```python
import functools
import math

import jax
import jax.numpy as jnp
from jax import lax
from jax.experimental import pallas as pl
from jax.experimental.pallas import tpu as pltpu

F32 = jnp.float32
BF16 = jnp.bfloat16
HIGHEST = lax.Precision.HIGHEST

D_MODEL = 1024
GRID_W = 64
CTX_LEN = 256
BRANCH_W = 512
ATT_HEADS = 4
ATT_DH = 64
ATT_DV = 128
ROPE_BASE = 10000.0
LRU_C = 8.0
SSD_HEADS = 8
SSD_HEADDIM = 64
SSD_HPG = 4
SSD_GROUPS = 2
SSD_STATE = 128
N_EXPERTS = 64
N_GROUPS = 8
GROUP_SIZE = N_EXPERTS // N_GROUPS
TOP_K = 8
TOPK_GROUPS = 4
EXPERT_F = 256
ROUTED_SCALE = 2.5
LN_EPS = 1e-5
RMS_EPS = 1e-6

ROW_TILE = 256
HALO = 16
VT_ROWS = 144
MOE_TILE = 2048
MOE_BLOCK = 128
SUBLANES = 8
LANES = 128
VMEM_LIMIT = 56 * 1024 * 1024
LOG2E = 1.4426950408889634


def _cparams(sem):
    return pltpu.CompilerParams(dimension_semantics=sem, vmem_limit_bytes=VMEM_LIMIT)


def _resident(shape):
    nd = len(shape)
    return pl.BlockSpec(shape, lambda *_: (0,) * nd, pipeline_mode=pl.Buffered(1))


def _silu(x):
    return x * jax.nn.sigmoid(x)


def _softplus(x):
    return jnp.maximum(x, 0.0) + jnp.log1p(jnp.exp(-jnp.abs(x)))


def _gelu_tanh(x):
    return 0.5 * x * (1.0 + jnp.tanh(math.sqrt(2.0 / math.pi) * (x + 0.044715 * (x * x * x))))


def _layer_norm(x, g, b):
    mu = jnp.mean(x, axis=-1, keepdims=True)
    xc = x - mu
    var = jnp.mean(xc * xc, axis=-1, keepdims=True)
    return xc * lax.rsqrt(var + LN_EPS) * g + b


def _mod_kernel(c_ref, w_ref, b_ref, o_ref):
    s = _silu(c_ref[...])
    o_ref[0] = jnp.dot(s, w_ref[0], preferred_element_type=F32, precision=HIGHEST) + b_ref[0]


def _modulation(cc, w_mod, b_mod):
    depth = w_mod.shape[0]
    nblk = 6
    return pl.pallas_call(
        _mod_kernel,
        grid=(depth, nblk),
        in_specs=[
            pl.BlockSpec((16, D_MODEL), lambda l, j: (0, 0)),
            pl.BlockSpec((1, D_MODEL, D_MODEL), lambda l, j: (l, 0, j)),
            pl.BlockSpec((1, 1, D_MODEL), lambda l, j: (l, 0, j)),
        ],
        out_specs=pl.BlockSpec((1, 16, D_MODEL), lambda l, j: (l, 0, j)),
        out_shape=jax.ShapeDtypeStruct((depth, 16, nblk * D_MODEL), F32),
        compiler_params=_cparams(("arbitrary", "arbitrary")),
    )(cc, w_mod, b_mod.reshape(depth, 1, nblk * D_MODEL))


def _mod_spec(nb):
    return pl.BlockSpec((1, 6, D_MODEL), lambda b, i: (jnp.where(i == 0, nb, b), 0, 0))


def _inproj_kernel(x_ref, m_ref, cos_ref, sin_ref, wqk_ref, wvt_ref, ones_ref, wlxg_ref, wz_ref,
                   wxbc_ref, wdt_ref, wdtt_ref, wg_ref,
                   qk_ref, vt_ref, lxg_ref, z_ref, xbc_ref, dt_ref, dtt_ref, g_ref):
    x = x_ref[0]
    h = (x * (1.0 + m_ref[0, 1:2, :]) + m_ref[0, 0:1, :]).astype(BF16)
    nt = (((1,), (1,)), ((), ()))

    qk = jnp.dot(h, wqk_ref[...], preferred_element_type=F32)
    cos = cos_ref[...]
    sin = sin_ref[...]
    lane = lax.broadcasted_iota(jnp.int32, cos.shape, 1)
    first_half = (lane % ATT_DH) < (ATT_DH // 2)
    for j in range(2 * ATT_HEADS):
        blk = qk[:, j * LANES:(j + 1) * LANES]
        partner = jnp.where(first_half, pltpu.roll(blk, LANES - ATT_DH // 2, 1),
                            pltpu.roll(blk, ATT_DH // 2, 1))
        r = blk * cos + partner * sin
        if j < ATT_HEADS:
            r = r * (ATT_DH ** -0.5 * LOG2E)
        qk_ref[0, :, j * LANES:(j + 1) * LANES] = r.astype(BF16)

    vt = lax.dot_general(wvt_ref[...], h, nt, preferred_element_type=F32) + ones_ref[...]
    vt_ref[0] = vt.astype(BF16)
    lxg_ref[0] = jnp.dot(h, wlxg_ref[...], preferred_element_type=F32).astype(BF16)
    z_ref[0] = jnp.dot(h, wz_ref[...], preferred_element_type=F32).astype(BF16)
    xbc_ref[0] = jnp.dot(h, wxbc_ref[...], preferred_element_type=F32).astype(BF16)
    dt_ref[0] = jnp.dot(h, wdt_ref[...], preferred_element_type=F32)
    dtt_ref[0] = lax.dot_general(wdtt_ref[...], h, nt, preferred_element_type=F32)
    g_ref[0] = jnp.dot(h, wg_ref[...], preferred_element_type=F32).astype(BF16)


def _inproj(xc, mods, cos_t, sin_t, w):
    nb, s, _ = xc.shape
    nt = s // ROW_TILE
    row = lambda width: pl.BlockSpec((1, ROW_TILE, width), lambda b, i: (b, i, 0))
    col = lambda rows: pl.BlockSpec((1, rows, ROW_TILE), lambda b, i: (b, 0, i))
    vt_rows = ATT_HEADS * VT_ROWS
    outs = [
        (jax.ShapeDtypeStruct((nb, s, 2 * BRANCH_W), BF16), row(2 * BRANCH_W)),
        (jax.ShapeDtypeStruct((nb, vt_rows, s), BF16), col(vt_rows)),
        (jax.ShapeDtypeStruct((nb, s, 2 * BRANCH_W), BF16), row(2 * BRANCH_W)),
        (jax.ShapeDtypeStruct((nb, s, BRANCH_W), BF16), row(BRANCH_W)),
        (jax.ShapeDtypeStruct((nb, s, 2 * BRANCH_W), BF16), row(2 * BRANCH_W)),
        (jax.ShapeDtypeStruct((nb, s, LANES), F32), row(LANES)),
        (jax.ShapeDtypeStruct((nb, 2 * SSD_HEADS, s), F32), col(2 * SSD_HEADS)),
        (jax.ShapeDtypeStruct((nb, s, 3 * D_MODEL), BF16), row(3 * D_MODEL)),
    ]
    weights = [w["qk"], w["vt"], w["ones"], w["lxg"], w["z"], w["xbc"], w["dt"], w["dtt"], w["g"]]
    return pl.pallas_call(
        _inproj_kernel,
        grid=(nb, nt),
        in_specs=[row(D_MODEL), _mod_spec(nb),
                  pl.BlockSpec((ROW_TILE, LANES), lambda b, i: (i, 0)),
                  pl.BlockSpec((ROW_TILE, LANES), lambda b, i: (i, 0))]
                 + [_resident(a.shape) for a in weights],
        out_specs=[o[1] for o in outs],
        out_shape=[o[0] for o in outs],
        compiler_params=_cparams(("arbitrary", "arbitrary")),
    )(xc, mods, cos_t, sin_t, *weights)


def _attn_block(q, k, vt, lam, gcol, out_scale):
    tq = q.shape[0]
    lane = lax.broadcasted_iota(jnp.int32, q.shape, 1)
    zero = jnp.zeros_like(q)
    q2 = jnp.concatenate([jnp.where(lane < ATT_DH, q, zero), jnp.where(lane >= ATT_DH, q, zero)], axis=0)
    st = lax.dot_general(k, q2, (((1,), (1,)), ((), ())), preferred_element_type=F32)
    m = jnp.max(st, axis=0, keepdims=True)
    e = jnp.exp2(st - m).astype(BF16)
    ot = jnp.dot(vt, e, preferred_element_type=F32)
    r = 1.0 / ot[ATT_DV:ATT_DV + 1, :]
    o = ot[:ATT_DV, :tq] * r[:, :tq] - lam * (ot[:ATT_DV, tq:] * r[:, tq:])
    ms = jnp.mean(o * o, axis=0, keepdims=True)
    o = o * lax.rsqrt(ms + RMS_EPS) * (gcol * out_scale)
    return o.T.astype(BF16)


def _attn_kernel(lq_ref, lk_ref, g_ref, q_ref, k_ref, vt_ref, o_ref, *, lam_init):
    qi = pl.program_id(2)
    prod = lq_ref[...] * lk_ref[...]
    s0 = jnp.sum(prod[0:1, :], axis=1, keepdims=True)
    s1 = jnp.sum(prod[1:2, :], axis=1, keepdims=True)
    lam = jnp.exp(s0) - jnp.exp(s1) + lam_init
    gcol = g_ref[0]
    out_scale = 1.0 - lam_init

    @pl.when(qi == 0)
    def _():
        o_ref[0] = _attn_block(q_ref[0], k_ref[0, :CTX_LEN, :], vt_ref[0, :, :CTX_LEN], lam, gcol, out_scale)

    @pl.when(qi > 0)
    def _():
        o_ref[0] = _attn_block(q_ref[0], k_ref[0], vt_ref[0], lam, gcol, out_scale)


def _attention(qk, vt, lam_q, lam_k, attn_g, lam_init):
    nb, s, _ = qk.shape
    nq = s // ROW_TILE
    return pl.pallas_call(
        functools.partial(_attn_kernel, lam_init=lam_init),
        grid=(nb, ATT_HEADS, nq),
        in_specs=[
            pl.BlockSpec((2, ATT_DH), lambda b, h, i: (0, 0)),
            pl.BlockSpec((2, ATT_DH), lambda b, h, i: (0, 0)),
            pl.BlockSpec((1, ATT_DV, 1), lambda b, h, i: (h, 0, 0)),
            pl.BlockSpec((1, ROW_TILE, LANES), lambda b, h, i: (b, i, h)),
            pl.BlockSpec((1, s, LANES), lambda b, h, i: (b, 0, ATT_HEADS + h)),
            pl.BlockSpec((1, VT_ROWS, s), lambda b, h, i: (b, h, 0)),
        ],
        out_specs=pl.BlockSpec((1, ROW_TILE, LANES), lambda b, h, i: (b, i, h)),
        out_shape=jax.ShapeDtypeStruct((nb, s, BRANCH_W), BF16),
        compiler_params=_cparams(("arbitrary", "arbitrary", "arbitrary")),
    )(lam_q, lam_k, attn_g.reshape(ATT_HEADS, ATT_DV, 1), qk, qk, vt)


def _scan_chunk(j, nc, reverse):
    if not reverse:
        return j
    return jnp.where(j == 0, 0, nc - j)


def _conv4(x, prev, nxt, c, nc, w_ref, b_ref):
    t = x.shape[0]
    row = lax.broadcasted_iota(jnp.int32, x.shape, 0)
    prev_ok = (c >= 2).astype(F32)
    next_ok = jnp.logical_and(c >= 1, c < nc - 1).astype(F32)
    p1 = prev[HALO - 1:HALO, :] * prev_ok
    p2 = prev[HALO - 2:HALO - 1, :] * prev_ok
    n0 = nxt[0:1, :] * next_ok
    xm1 = jnp.where(row == 0, p1, pltpu.roll(x, 1, 0))
    xm2 = jnp.where(row == 0, p2, jnp.where(row == 1, p1, pltpu.roll(x, 2, 0)))
    xp1 = jnp.where(row == t - 1, n0, pltpu.roll(x, t - 1, 0))
    return w_ref[0:1, :] * xm2 + w_ref[1:2, :] * xm1 + w_ref[2:3, :] * x + w_ref[3:4, :] * xp1 + b_ref[...]


def _tile_specs(width, nc, reverse, blk=0):
    per = ROW_TILE // HALO
    last = nc * per - 1
    ch = lambda j: _scan_chunk(j, nc, reverse)
    cur = pl.BlockSpec((1, ROW_TILE, width), lambda b, j: (b, ch(j), blk))
    prev = pl.BlockSpec((1, HALO, width), lambda b, j: (b, jnp.maximum(ch(j) * per - 1, 0), blk))
    nxt = pl.BlockSpec((1, HALO, width), lambda b, j: (b, jnp.minimum((ch(j) + 1) * per, last), blk))
    return cur, prev, nxt


def _rglru_kernel(*refs, reverse, nc):
    if reverse:
        (x_ref, xp_ref, xn_ref, cw_ref, cb_ref, wg_ref, ba_ref, bi_ref, lam_ref,
         yf_ref, lg_ref, o_ref, h_ref) = refs
    else:
        (x_ref, xp_ref, xn_ref, cw_ref, cb_ref, wg_ref, ba_ref, bi_ref, lam_ref, o_ref, h_ref) = refs
    j = pl.program_id(1)
    c = _scan_chunk(j, nc, reverse)

    @pl.when(j == 0)
    def _():
        h_ref[...] = jnp.zeros_like(h_ref)

    u = _conv4(x_ref[0].astype(F32), xp_ref[0].astype(F32), xn_ref[0].astype(F32), c, nc, cw_ref, cb_ref)
    t = u.shape[0]
    pre = jnp.dot(u.astype(BF16), wg_ref[...], preferred_element_type=F32)
    r = jax.nn.sigmoid(pre[:, :BRANCH_W] + ba_ref[...])
    gi = jax.nn.sigmoid(pre[:, BRANCH_W:] + bi_ref[...])
    log_a = (-LRU_C) * r * _softplus(-lam_ref[...])
    a = jnp.exp(log_a)
    bv = jnp.sqrt(1.0 - jnp.exp(2.0 * log_a)) * (gi * u)

    row = lax.broadcasted_iota(jnp.int32, u.shape, 0)
    d = 1
    while d < t:
        if reverse:
            keep = row < t - d
            sh = t - d
        else:
            keep = row >= d
            sh = d
        a_s = jnp.where(keep, pltpu.roll(a, sh, 0), 1.0)
        b_s = jnp.where(keep, pltpu.roll(bv, sh, 0), 0.0)
        bv = a * b_s + bv
        a = a * a_s
        d *= 2
    hs = a * h_ref[...] + bv
    h_ref[...] = hs[0:1, :] if reverse else hs[t - 1:t, :]
    if reverse:
        o_ref[0] = ((yf_ref[0] + hs) * _gelu_tanh(lg_ref[0].astype(F32))).astype(BF16)
    else:
        o_ref[0] = hs


def _rglru(lxg, conv_w, conv_b, w_gate, b_a, b_i, lam, reverse, yf=None):
    nb, s, _ = lxg.shape
    nc = s // ROW_TILE
    cur, prev, nxt = _tile_specs(BRANCH_W, nc, reverse)
    params = [conv_w, conv_b, w_gate, b_a, b_i, lam]
    in_specs = [cur, prev, nxt] + [_resident(p.shape) for p in params]
    args = [lxg, lxg, lxg] + params
    if reverse:
        in_specs += [cur, _tile_specs(BRANCH_W, nc, reverse, blk=1)[0]]
        args += [yf, lxg]
    return pl.pallas_call(
        functools.partial(_rglru_kernel, reverse=reverse, nc=nc),
        grid=(nb, nc),
        in_specs=in_specs,
        out_specs=cur,
        out_shape=jax.ShapeDtypeStruct((nb, s, BRANCH_W), BF16 if reverse else F32),
        scratch_shapes=[pltpu.VMEM((1, BRANCH_W), F32)],
        compiler_params=_cparams(("arbitrary", "arbitrary")),
    )(*args)


def _ssd_kernel(*refs, reverse, nc, direction):
    if reverse:
        (x_ref, xp_ref, xn_ref, dt_ref, dtt_ref, cw_ref, cb_ref, dtb_ref, dtbc_ref, alog_ref, alogc_ref,
         exp_ref, yf_ref, z_ref, skip_ref, g_ref, o_ref, st_ref) = refs
    else:
        (x_ref, xp_ref, xn_ref, dt_ref, dtt_ref, cw_ref, cb_ref, dtb_ref, dtbc_ref, alog_ref, alogc_ref,
         exp_ref, o_ref, st_ref) = refs
    j = pl.program_id(1)
    c = _scan_chunk(j, nc, reverse)

    @pl.when(j == 0)
    def _():
        st_ref[...] = jnp.zeros_like(st_ref)

    u = _silu(_conv4(x_ref[0].astype(F32), xp_ref[0].astype(F32), xn_ref[0].astype(F32), c, nc, cw_ref, cb_ref))
    t = u.shape[0]
    xs = u[:, :BRANCH_W]
    gw = SSD_HPG * SSD_HEADDIM

    dt_c = _softplus(dt_ref[0] + dtb_ref[...])
    a_c = dt_c * (-jnp.exp(alog_ref[...]))
    a_r = _softplus(dtt_ref[0] + dtbc_ref[...]) * (-jnp.exp(alogc_ref[...]))
    ri = lax.broadcasted_iota(jnp.int32, (t, t), 0)
    ci = lax.broadcasted_iota(jnp.int32, (t, t), 1)
    lower = (ci <= ri).astype(F32)
    upper = (ci >= ri).astype(F32)
    cs_c = jnp.dot(upper if reverse else lower, a_c, preferred_element_type=F32, precision=HIGHEST)
    cs_r = jnp.dot(a_r, lower if reverse else upper, preferred_element_type=F32, precision=HIGHEST)
    tot = cs_c[0:1, :] if reverse else cs_c[t - 1:t, :]
    keep = (ci >= ri) if reverse else (ci <= ri)

    expand = exp_ref[...]
    xdt = xs * jnp.dot(dt_c, expand, preferred_element_type=F32, precision=HIGHEST)
    e_in = jnp.dot(jnp.exp(cs_c), expand, preferred_element_type=F32, precision=HIGHEST)
    e_out = jnp.dot(jnp.exp(tot - cs_c), expand, preferred_element_type=F32, precision=HIGHEST)
    e_tot = e_in[0:1, :] if reverse else e_in[t - 1:t, :]
    xdt_b = xdt.astype(BF16)
    xdec_b = (xdt * e_out).astype(BF16)
    lane_head = lax.broadcasted_iota(jnp.int32, (t, gw), 1) // SSD_HEADDIM
    zero_b = jnp.zeros((t, gw), BF16)

    ys = []
    for g in range(SSD_GROUPS):
        bm = u[:, BRANCH_W + g * SSD_STATE:BRANCH_W + (g + 1) * SSD_STATE].astype(BF16)
        cm = u[:, BRANCH_W + (SSD_GROUPS + g) * SSD_STATE:BRANCH_W + (SSD_GROUPS + g + 1) * SSD_STATE].astype(BF16)
        cb = lax.dot_general(cm, bm, (((1,), (1,)), ((), ())), preferred_element_type=F32)
        st_g = st_ref[:, g * gw:(g + 1) * gw]
        y = jnp.dot(cm, st_g.astype(BF16), preferred_element_type=F32) * e_in[:, g * gw:(g + 1) * gw]
        xg = xdt_b[:, g * gw:(g + 1) * gw]
        for rr in range(SSD_HPG):
            col = direction * SSD_HEADS + g * SSD_HPG + rr
            decay = jnp.where(keep, jnp.exp(cs_c[:, col:col + 1] - cs_r[col:col + 1, :]), 0.0)
            gm = (cb * decay).astype(BF16)
            y = y + jnp.dot(gm, jnp.where(lane_head == rr, xg, zero_b), preferred_element_type=F32)
        ys.append(y)
        upd = lax.dot_general(bm, xdec_b[:, g * gw:(g + 1) * gw], (((0,), (0,)), ((), ())),
                              preferred_element_type=F32)
        st_ref[:, g * gw:(g + 1) * gw] = st_g * e_tot[:, g * gw:(g + 1) * gw] + upd
    y = jnp.concatenate(ys, axis=1)

    if reverse:
        y = yf_ref[0] + y + skip_ref[...] * xs
        tz = y * _silu(z_ref[0].astype(F32))
        outs = []
        for g in range(SSD_GROUPS):
            tg = tz[:, g * gw:(g + 1) * gw]
            outs.append(tg * lax.rsqrt(jnp.mean(tg * tg, axis=1, keepdims=True) + RMS_EPS))
        o_ref[0] = (jnp.concatenate(outs, axis=1) * g_ref[...]).astype(BF16)
    else:
        o_ref[0] = y


def _ssd(xbc, dt, dtt, p, direction, yf=None, z=None):
    reverse = direction == 1
    nb, s, _ = xbc.shape
    nc = s // ROW_TILE
    cur, prev, nxt = _tile_specs(2 * BRANCH_W, nc, reverse)
    ch = lambda j: _scan_chunk(j, nc, reverse)
    params = [p["conv_w"], p["conv_b"], p["dtb_row"], p["dtb_col"], p["alog_row"], p["alog_col"], p["expand"]]
    in_specs = [cur, prev, nxt,
                pl.BlockSpec((1, ROW_TILE, LANES), lambda b, j: (b, ch(j), 0)),
                pl.BlockSpec((1, 2 * SSD_HEADS, ROW_TILE), lambda b, j: (b, 0, ch(j)))]
    in_specs += [_resident(a.shape) for a in params]
    args = [xbc, xbc, xbc, dt, dtt] + params
    half = pl.BlockSpec((1, ROW_TILE, BRANCH_W), lambda b, j: (b, ch(j), 0))
    if reverse:
        in_specs += [half, half, _resident(p["skip"].shape), _resident(p["norm_g"].shape)]
        args += [yf, z, p["skip"], p["norm_g"]]
    return pl.pallas_call(
        functools.partial(_ssd_kernel, reverse=reverse, nc=nc, direction=direction),
        grid=(nb, nc),
        in_specs=in_specs,
        out_specs=half,
        out_shape=jax.ShapeDtypeStruct((nb, s, BRANCH_W), BF16 if reverse else F32),
        scratch_shapes=[pltpu.VMEM((SSD_STATE, BRANCH_W), F32)],
        compiler_params=_cparams(("arbitrary", "arbitrary")),
    )(*args)


def _merge_kernel(ya_ref, yb_ref, yc_ref, gt_ref, x_ref, m_ref, wb_ref, wo_ref, g_ref, b_ref,
                  x1_ref, h2_ref, *, alpha):
    acc = None
    for n, y_ref in enumerate((ya_ref, yb_ref, yc_ref)):
        proj = jnp.dot(y_ref[0], wb_ref[n], preferred_element_type=F32)
        gate = jax.nn.sigmoid(gt_ref[0, :, n * D_MODEL:(n + 1) * D_MODEL].astype(F32))
        acc = gate * proj if acc is None else acc + gate * proj
    mix = jnp.dot(acc.astype(BF16), wo_ref[...], preferred_element_type=F32)
    x1 = _layer_norm(alpha * x_ref[0] + m_ref[0, 2:3, :] * mix, g_ref[...], b_ref[...])
    x1_ref[0] = x1
    h2_ref[0] = x1 * (1.0 + m_ref[0, 4:5, :]) + m_ref[0, 3:4, :]


def _merge(ya, yb, yc, gates, xc, mods, w_branch, w_out, ln_g, ln_b, alpha):
    nb, s, _ = xc.shape
    nt = s // ROW_TILE
    row = lambda width: pl.BlockSpec((1, ROW_TILE, width), lambda b, i: (b, i, 0))
    out = jax.ShapeDtypeStruct((nb, s, D_MODEL), F32)
    return pl.pallas_call(
        functools.partial(_merge_kernel, alpha=alpha),
        grid=(nb, nt),
        in_specs=[row(BRANCH_W), row(BRANCH_W), row(BRANCH_W), row(3 * D_MODEL), row(D_MODEL), _mod_spec(nb),
                  _resident(w_branch.shape), _resident(w_out.shape), _resident(ln_g.shape), _resident(ln_b.shape)],
        out_specs=[row(D_MODEL), row(D_MODEL)],
        out_shape=[out, out],
        compiler_params=_cparams(("arbitrary", "arbitrary")),
    )(ya, yb, yc, gates, xc, mods, w_branch, w_out, ln_g, ln_b)


def _first_index(hit, ridx, n):
    return jnp.min(jnp.where(hit, ridx, n), axis=0, keepdims=True)


def _router_kernel(h_ref, wr_ref, bias_ref, cnt_ref, idx_ref, wl_ref):
    tm = h_ref.shape[0]
    logits = lax.dot_general(wr_ref[...], h_ref[...], (((1,), (1,)), ((), ())),
                             preferred_element_type=F32, precision=HIGHEST)
    scores = jax.nn.sigmoid(logits)
    sel = scores + bias_ref[...]
    neg = -jnp.inf

    r8 = lax.broadcasted_iota(jnp.int32, (GROUP_SIZE, tm), 0)
    grp = jnp.zeros((N_GROUPS, tm), F32)
    for g in range(N_GROUPS):
        blk = sel[g * GROUP_SIZE:(g + 1) * GROUP_SIZE, :]
        m1 = jnp.max(blk, axis=0, keepdims=True)
        first = _first_index(blk == m1, r8, GROUP_SIZE)
        m2 = jnp.max(jnp.where(r8 == first, neg, blk), axis=0, keepdims=True)
        grp = jnp.where(r8 == g, m1 + m2, grp)
    gsel = jnp.zeros((N_GROUPS, tm), jnp.int32)
    for _ in range(TOPK_GROUPS):
        m = jnp.max(grp, axis=0, keepdims=True)
        hit = r8 == _first_index(grp == m, r8, N_GROUPS)
        gsel = jnp.where(hit, 1, gsel)
        grp = jnp.where(hit, neg, grp)
    cand = jnp.concatenate(
        [jnp.where(gsel[g:g + 1, :] > 0, sel[g * GROUP_SIZE:(g + 1) * GROUP_SIZE, :], neg) for g in range(N_GROUPS)],
        axis=0)

    re = lax.broadcasted_iota(jnp.int32, (N_EXPERTS, tm), 0)
    chosen = jnp.zeros((N_EXPERTS, tm), jnp.int32)
    for _ in range(TOP_K):
        m = jnp.max(cand, axis=0, keepdims=True)
        hit = re == _first_index(cand == m, re, N_EXPERTS)
        chosen = jnp.where(hit, 1, chosen)
        cand = jnp.where(hit, neg, cand)
    picked = jnp.where(chosen > 0, scores, 0.0)
    wgt = picked / jnp.sum(picked, axis=0, keepdims=True) * ROUTED_SCALE

    li = lax.broadcasted_iota(jnp.int32, (LANES, LANES), 0)
    lj = lax.broadcasted_iota(jnp.int32, (LANES, LANES), 1)
    strict = (li < lj).astype(BF16)
    chosen_b = chosen.astype(F32).astype(BF16)
    base = jnp.zeros((N_EXPERTS, 1), F32)
    pos = []
    for kb in range(tm // LANES):
        cblk = chosen_b[:, kb * LANES:(kb + 1) * LANES]
        pos.append(jnp.dot(cblk, strict, preferred_element_type=F32) + base)
        base = base + jnp.sum(cblk.astype(F32), axis=1, keepdims=True)
    pos = jnp.concatenate(pos, axis=1).astype(jnp.int32)
    count = base.astype(jnp.int32)

    lane = lax.broadcasted_iota(jnp.int32, (N_EXPERTS, tm), 1)
    live = chosen
    dist = lane - pos
    val = lane
    wv = wgt
    step = 1
    while step < tm:
        move = jnp.where(jnp.logical_and(live > 0, (dist & step) != 0), 1, 0)
        sh = tm - step
        mv_in = pltpu.roll(move, sh, 1)
        take = mv_in > 0
        val = jnp.where(take, pltpu.roll(val, sh, 1), val)
        wv = jnp.where(take, pltpu.roll(wv, sh, 1), wv)
        dist = jnp.where(take, pltpu.roll(dist, sh, 1), dist)
        live = jnp.where(take, 1, jnp.where(move > 0, 0, live))
        step *= 2
    valid = lane < count
    cnt_ref[0] = count
    idx_ref[0] = jnp.where(valid, val, tm)
    wl_ref[0] = jnp.where(valid, wv, 0.0)


def _router(h2_flat, w_router_t, bias_col):
    n = h2_flat.shape[0]
    nt = n // MOE_TILE
    return pl.pallas_call(
        _router_kernel,
        grid=(nt,),
        in_specs=[pl.BlockSpec((MOE_TILE, D_MODEL), lambda t: (t, 0)),
                  _resident(w_router_t.shape), _resident(bias_col.shape)],
        out_specs=[pl.BlockSpec((1, N_EXPERTS, 1), lambda t: (t, 0, 0)),
                   pl.BlockSpec((1, N_EXPERTS, MOE_TILE), lambda t: (t, 0, 0)),
                   pl.BlockSpec((1, N_EXPERTS, MOE_TILE), lambda t: (t, 0, 0))],
        out_shape=[jax.ShapeDtypeStruct((nt, N_EXPERTS, 1), jnp.int32),
                   jax.ShapeDtypeStruct((nt, N_EXPERTS, MOE_TILE), jnp.int32),
                   jax.ShapeDtypeStruct((nt, N_EXPERTS, MOE_TILE), F32)],
        compiler_params=_cparams(("arbitrary",)),
    )(h2_flat, w_router_t, bias_col)


SCATTER_GROUP = 16


def _experts_kernel(cnt_ref, idx_ref, wl_ref, h_ref, wu_ref, wd_ref, o_ref, acc_ref, xg_ref, yg_ref):
    t = pl.program_id(0)
    e = pl.program_id(1)
    nchunk = D_MODEL // LANES

    @pl.when(e == 0)
    def _():
        acc_ref[...] = jnp.zeros_like(acc_ref)

    count = cnt_ref[t * N_EXPERTS + e]

    def block(i, carry):
        base = i * MOE_BLOCK
        for r in range(MOE_BLOCK):
            src = jnp.minimum(idx_ref[0, 0, base + r], MOE_TILE - 1)
            xg_ref[pl.ds(r * nchunk, nchunk), :] = h_ref[pl.ds(pl.multiple_of(src * nchunk, nchunk), nchunk), :]
        x = jnp.concatenate([xg_ref[pl.ds(s, MOE_BLOCK, stride=nchunk), :] for s in range(nchunk)], axis=1)
        gu = jnp.dot(x.astype(BF16), wu_ref[0], preferred_element_type=F32)
        act = _silu(gu[:, :EXPERT_F]) * gu[:, EXPERT_F:]
        y = jnp.dot(act.astype(BF16), wd_ref[0], preferred_element_type=F32)
        for s in range(nchunk):
            yg_ref[pl.ds(s, MOE_BLOCK, stride=nchunk), :] = y[:, s * LANES:(s + 1) * LANES]
        for g0 in range(0, MOE_BLOCK, SCATTER_GROUP):
            rows = range(g0, g0 + SCATTER_GROUP)
            dst = [pl.multiple_of(idx_ref[0, 0, base + r] * nchunk, nchunk) for r in rows]
            old = [acc_ref[pl.ds(d, nchunk), :] for d in dst]
            new = [o + wl_ref[0, 0, base + r] * yg_ref[pl.ds(r * nchunk, nchunk), :] for o, r in zip(old, rows)]
            for d, v in zip(dst, new):
                acc_ref[pl.ds(d, nchunk), :] = v
        return carry

    lax.fori_loop(0, (count + MOE_BLOCK - 1) // MOE_BLOCK, block, 0)

    @pl.when(e == N_EXPERTS - 1)
    def _():
        o_ref[...] = acc_ref[pl.ds(0, MOE_TILE * nchunk), :]


def _experts(h2_rows, counts, idx, wl, w_up, w_down):
    nrow = h2_rows.shape[0]
    nchunk = D_MODEL // LANES
    nt = nrow // (MOE_TILE * nchunk)
    tile = pl.BlockSpec((MOE_TILE * nchunk, LANES), lambda t, e, c: (t, 0))
    lst = pl.BlockSpec((1, 1, MOE_TILE), lambda t, e, c: (t * N_EXPERTS + e, 0, 0), memory_space=pltpu.SMEM)
    grid_spec = pltpu.PrefetchScalarGridSpec(
        num_scalar_prefetch=1,
        grid=(nt, N_EXPERTS),
        in_specs=[lst, lst, tile,
                  pl.BlockSpec((1, D_MODEL, 2 * EXPERT_F), lambda t, e, c: (e, 0, 0)),
                  pl.BlockSpec((1, EXPERT_F, D_MODEL), lambda t, e, c: (e, 0, 0))],
        out_specs=tile,
        scratch_shapes=[pltpu.VMEM(((MOE_TILE + 1) * nchunk, LANES), F32),
                        pltpu.VMEM((MOE_BLOCK * nchunk, LANES), F32),
                        pltpu.VMEM((MOE_BLOCK * nchunk, LANES), F32)],
    )
    return pl.pallas_call(
        _experts_kernel,
        grid_spec=grid_spec,
        out_shape=jax.ShapeDtypeStruct((nrow, LANES), F32),
        compiler_params=_cparams(("arbitrary", "arbitrary")),
    )(counts, idx, wl, h2_rows, w_up, w_down)


def _ffn_out_kernel(x1_ref, h2_ref, fr_ref, m_ref, wu_ref, wd_ref, g_ref, b_ref, o_ref, *, alpha):
    gu = jnp.dot(h2_ref[0].astype(BF16), wu_ref[...], preferred_element_type=F32)
    act = _silu(gu[:, :EXPERT_F]) * gu[:, EXPERT_F:]
    f = jnp.dot(act.astype(BF16), wd_ref[...], preferred_element_type=F32) + fr_ref[0]
    o_ref[0] = _layer_norm(alpha * x1_ref[0] + m_ref[0, 5:6, :] * f, g_ref[...], b_ref[...])


def _ffn_out(x1, h2, fr, mods, ws_up, ws_down, ln_g, ln_b, alpha, latent_only):
    nb, s, _ = x1.shape
    nt = s // ROW_TILE
    skip = CTX_LEN // ROW_TILE if latent_only else 0
    row = pl.BlockSpec((1, ROW_TILE, D_MODEL), lambda b, i: (b, i + skip, 0))
    mod = pl.BlockSpec((1, 6, D_MODEL), lambda b, i: (jnp.where(i + skip == 0, nb, b), 0, 0))
    return pl.pallas_call(
        functools.partial(_ffn_out_kernel, alpha=alpha),
        grid=(nb, nt - skip),
        in_specs=[row, row, row, mod, _resident(ws_up.shape), _resident(ws_down.shape),
                  _resident(ln_g.shape), _resident(ln_b.shape)],
        out_specs=pl.BlockSpec((1, ROW_TILE, D_MODEL), lambda b, i: (b, i, 0)),
        out_shape=jax.ShapeDtypeStruct((nb, s - skip * ROW_TILE, D_MODEL), F32),
        compiler_params=_cparams(("arbitrary", "arbitrary")),
    )(x1, h2, fr, mods, ws_up, ws_down, ln_g, ln_b)


def _rope_tables(n_lat):
    t = jnp.arange(n_lat)
    rowp = (t // GRID_W).astype(F32)
    colp = (t % GRID_W).astype(F32)
    n_freq = ATT_DH // 4
    inv = ROPE_BASE ** (-jnp.arange(n_freq, dtype=F32) / n_freq)
    ang = jnp.concatenate([rowp[:, None] * inv, colp[:, None] * inv], axis=-1)
    lane = jnp.arange(LANES)
    cos = jnp.cos(ang)[:, lane % (ATT_DH // 2)]
    sign = jnp.where((lane % ATT_DH) < ATT_DH // 2, -1.0, 1.0).astype(F32)
    sin = jnp.sin(ang)[:, lane % (ATT_DH // 2)] * sign
    cos = jnp.concatenate([jnp.ones((CTX_LEN, LANES), F32), cos], axis=0)
    sin = jnp.concatenate([jnp.zeros((CTX_LEN, LANES), F32), sin], axis=0)
    return cos, sin


def _block_diag(w):
    n, k, _ = w.shape
    eye = jnp.eye(n, dtype=w.dtype)
    return (eye[:, None, :, None] * w[:, :, None, :]).reshape(n * k, n * k)


def _inproj_weights(w):
    o_v = 2 * BRANCH_W
    o_lx = o_v + ATT_HEADS * ATT_DV
    o_z = o_lx + 2 * BRANCH_W
    o_xbc = o_z + BRANCH_W
    o_dt = o_xbc + 2 * BRANCH_W
    o_g = o_dt + 2 * SSD_HEADS
    wv = w[:, o_v:o_lx].T.reshape(ATT_HEADS, ATT_DV, D_MODEL)
    wv = jnp.pad(wv, ((0, 0), (0, VT_ROWS - ATT_DV), (0, 0))).reshape(ATT_HEADS * VT_ROWS, D_MODEL)
    ones = jnp.zeros((ATT_HEADS, VT_ROWS, 1), F32).at[:, ATT_DV, 0].set(1.0).reshape(ATT_HEADS * VT_ROWS, 1)
    wdt = w[:, o_dt:o_g]
    return {
        "qk": w[:, :o_v].astype(BF16),
        "vt": wv.astype(BF16),
        "ones": ones,
        "lxg": w[:, o_lx:o_z].astype(BF16),
        "z": w[:, o_z:o_xbc].astype(BF16),
        "xbc": w[:, o_xbc:o_dt].astype(BF16),
        "dt": jnp.pad(wdt, ((0, 0), (0, LANES - 2 * SSD_HEADS))).astype(BF16),
        "dtt": wdt.T.astype(BF16),
        "g": w[:, o_g:].astype(BF16),
    }


def _pad_row(v):
    return jnp.pad(v.reshape(1, -1), ((0, 0), (0, LANES - v.size)))


def kernel(x, c, ctx, c_ctx, w_mod, b_mod, w_in, lam_q, lam_k, attn_norm_g, lru_conv_w, lru_conv_b, lru_wa, lru_ba, lru_wi, lru_bi, lru_lambda, ssd_conv_w, ssd_conv_b, ssd_dt_bias, ssd_a_log, ssd_d, ssd_norm_g, w_branch, w_out, ln1_g, ln1_b, w_router, router_bias, w_up, w_down, ws_up, ws_down, ln2_g, ln2_b):
    nb, n_lat, _ = x.shape
    depth = w_mod.shape[0]
    assert ctx.shape[1] == CTX_LEN and n_lat % ROW_TILE == 0 and nb + 1 <= 16
    s = CTX_LEN + n_lat
    assert (nb * s) % MOE_TILE == 0
    alpha = (2 * depth) ** 0.25

    xc = jnp.concatenate([ctx, x], axis=1)
    cc = jnp.zeros((16, D_MODEL), F32).at[:nb].set(c).at[nb].set(c_ctx)
    mods_all = _modulation(cc, w_mod, b_mod).reshape(depth, 16, 6, D_MODEL)
    cos_t, sin_t = _rope_tables(n_lat)
    head_of_channel = jnp.arange(BRANCH_W) // SSD_HEADDIM

    for l in range(depth):
        last = l == depth - 1
        lam_init = 0.8 - 0.6 * math.exp(-0.3 * l)
        mods = mods_all[l]
        qk, vt, lxg, z, xbc, dt, dtt, gates = _inproj(xc, mods, cos_t, sin_t, _inproj_weights(w_in[l]))

        ya = _attention(qk, vt, lam_q[l], lam_k[l], attn_norm_g[l], lam_init)

        yb = None
        for d in range(2):
            w_gate = jnp.concatenate([_block_diag(lru_wa[l, d]), _block_diag(lru_wi[l, d])], axis=1).astype(BF16)
            yb = _rglru(lxg, lru_conv_w[l], lru_conv_b[l].reshape(1, -1), w_gate,
                        lru_ba[l, d].reshape(1, -1), lru_bi[l, d].reshape(1, -1), lru_lambda[l, d].reshape(1, -1),
                        reverse=(d == 1), yf=yb)

        ssd_p = {
            "conv_w": ssd_conv_w[l], "conv_b": ssd_conv_b[l].reshape(1, -1),
            "dtb_row": _pad_row(ssd_dt_bias[l]), "dtb_col": ssd_dt_bias[l].reshape(-1, 1),
            "alog_row": _pad_row(ssd_a_log[l]), "alog_col": ssd_a_log[l].reshape(-1, 1),
            "skip": jnp.repeat(ssd_d[l], SSD_HEADDIM).reshape(1, -1), "norm_g": ssd_norm_g[l].reshape(1, -1),
        }
        yc = None
        for d in range(2):
            ssd_p["expand"] = (jnp.arange(LANES)[:, None] == d * SSD_HEADS + head_of_channel[None, :]).astype(F32)
            yc = _ssd(xbc, dt, dtt, ssd_p, d, yf=yc, z=z)

        x1, h2 = _merge(ya, yb, yc, gates, xc, mods, w_branch[l].astype(BF16), w_out[l].astype(BF16),
                        ln1_g[l].reshape(1, -1), ln1_b[l].reshape(1, -1), alpha)

        n_tok = nb * s
        nchunk = D_MODEL // LANES
        counts, idx, wl = _router(h2.reshape(n_tok, D_MODEL), w_router[l].T, router_bias[l].reshape(-1, 1))
        nt = n_tok // MOE_TILE
        fr = _experts(h2.reshape(n_tok * nchunk, LANES), counts.reshape(nt * N_EXPERTS),
                      idx.reshape(nt * N_EXPERTS, 1, MOE_TILE), wl.reshape(nt * N_EXPERTS, 1, MOE_TILE),
                      w_up[l].astype(BF16), w_down[l].astype(BF16))
        xc = _ffn_out(x1, h2, fr.reshape(nb, s, D_MODEL), mods, ws_up[l].astype(BF16), ws_down[l].astype(BF16),
                      ln2_g[l].reshape(1, -1), ln2_b[l].reshape(1, -1), alpha, latent_only=last)
    return xc
```

```python
import functools
import math

import jax
import jax.numpy as jnp
from jax import lax
from jax.experimental import pallas as pl
from jax.experimental.pallas import tpu as pltpu

F32 = jnp.float32
BF16 = jnp.bfloat16
HIGHEST = lax.Precision.HIGHEST

D_MODEL = 1024
GRID_W = 64
CTX_LEN = 256
BRANCH_W = 512
ATT_HEADS = 4
ATT_DH = 64
ATT_DV = 128
ROPE_BASE = 10000.0
LRU_C = 8.0
SSD_HEADS = 8
SSD_HEADDIM = 64
SSD_HPG = 4
SSD_GROUPS = 2
SSD_STATE = 128
N_EXPERTS = 64
N_GROUPS = 8
GROUP_SIZE = N_EXPERTS // N_GROUPS
TOP_K = 8
TOPK_GROUPS = 4
EXPERT_F = 256
ROUTED_SCALE = 2.5
LN_EPS = 1e-5
RMS_EPS = 1e-6

ROW_TILE = 256
HALO = 16
VT_ROWS = 144
KEY_CHUNK = 256
MOE_TILE = 2048
MOE_BLOCK = 128
MOE_LIST = MOE_TILE + MOE_BLOCK
SUBLANES = 8
LANES = 128
VMEM_LIMIT = 56 * 1024 * 1024
LOG2E = 1.4426950408889634


def _cparams(sem):
    return pltpu.CompilerParams(dimension_semantics=sem, vmem_limit_bytes=VMEM_LIMIT)


def _resident(shape):
    nd = len(shape)
    return pl.BlockSpec(shape, lambda *_: (0,) * nd, pipeline_mode=pl.Buffered(1))


def _silu(x):
    return x * jax.nn.sigmoid(x)


def _softplus(x):
    return jnp.maximum(x, 0.0) + jnp.log1p(jnp.exp(-jnp.abs(x)))


def _gelu_tanh(x):
    return 0.5 * x * (1.0 + jnp.tanh(math.sqrt(2.0 / math.pi) * (x + 0.044715 * (x * x * x))))


NCHUNK = D_MODEL // LANES


def _load_token_rows(ref, n):
    return jnp.concatenate([ref[pl.ds(s, n, stride=NCHUNK), :] for s in range(NCHUNK)], axis=1)


def _store_token_rows(ref, x):
    n = x.shape[0]
    for s in range(NCHUNK):
        ref[pl.ds(s, n, stride=NCHUNK), :] = x[:, s * LANES:(s + 1) * LANES]


def _layer_norm(x, g, b):
    mu = jnp.mean(x, axis=-1, keepdims=True)
    xc = x - mu
    var = jnp.mean(xc * xc, axis=-1, keepdims=True)
    return xc * lax.rsqrt(var + LN_EPS) * g + b


def _mod_kernel(c_ref, w_ref, b_ref, o_ref):
    s = _silu(c_ref[...])
    o_ref[0] = jnp.dot(s, w_ref[0], preferred_element_type=F32, precision=HIGHEST) + b_ref[0]


def _modulation(cc, w_mod, b_mod):
    depth = w_mod.shape[0]
    nblk = 6
    return pl.pallas_call(
        _mod_kernel,
        grid=(depth, nblk),
        in_specs=[
            pl.BlockSpec((16, D_MODEL), lambda l, j: (0, 0)),
            pl.BlockSpec((1, D_MODEL, D_MODEL), lambda l, j: (l, 0, j)),
            pl.BlockSpec((1, 1, D_MODEL), lambda l, j: (l, 0, j)),
        ],
        out_specs=pl.BlockSpec((1, 16, D_MODEL), lambda l, j: (l, 0, j)),
        out_shape=jax.ShapeDtypeStruct((depth, 16, nblk * D_MODEL), F32),
        compiler_params=_cparams(("arbitrary", "arbitrary")),
    )(cc, w_mod, b_mod.reshape(depth, 1, nblk * D_MODEL))


def _mod_spec(nb):
    return pl.BlockSpec((1, 6, D_MODEL), lambda b, i: (jnp.where(i == 0, nb, b), 0, 0))


def _inproj_kernel(x_ref, m_ref, cos_ref, sin_ref, wqk_ref, wvt_ref, ones_ref, wlxg_ref, wz_ref,
                   wxbc_ref, wdt_ref, wdtt_ref, wg_ref,
                   qk_ref, vt_ref, lxg_ref, z_ref, xbc_ref, dt_ref, dtt_ref, g_ref):
    x = x_ref[0]
    h = (x * (1.0 + m_ref[0, 1:2, :]) + m_ref[0, 0:1, :]).astype(BF16)
    nt = (((1,), (1,)), ((), ()))

    qk = jnp.dot(h, wqk_ref[...], preferred_element_type=F32)
    cos = cos_ref[...]
    sin = sin_ref[...]
    lane = lax.broadcasted_iota(jnp.int32, cos.shape, 1)
    first_half = (lane % ATT_DH) < (ATT_DH // 2)
    for j in range(2 * ATT_HEADS):
        blk = qk[:, j * LANES:(j + 1) * LANES]
        partner = jnp.where(first_half, pltpu.roll(blk, LANES - ATT_DH // 2, 1),
                            pltpu.roll(blk, ATT_DH // 2, 1))
        r = blk * cos + partner * sin
        if j < ATT_HEADS:
            r = r * (ATT_DH ** -0.5 * LOG2E)
        qk_ref[0, :, j * LANES:(j + 1) * LANES] = r.astype(BF16)

    vt = lax.dot_general(wvt_ref[...], h, nt, preferred_element_type=F32) + ones_ref[...]
    vt_ref[0] = vt.astype(BF16)
    lxg_ref[0] = jnp.dot(h, wlxg_ref[...], preferred_element_type=F32).astype(BF16)
    z_ref[0] = jnp.dot(h, wz_ref[...], preferred_element_type=F32).astype(BF16)
    xbc_ref[0] = jnp.dot(h, wxbc_ref[...], preferred_element_type=F32).astype(BF16)
    dt_ref[0] = jnp.dot(h, wdt_ref[...], preferred_element_type=F32)
    dtt_ref[0] = lax.dot_general(wdtt_ref[...], h, nt, preferred_element_type=F32)
    g_ref[0] = jnp.dot(h, wg_ref[...], preferred_element_type=F32).astype(BF16)


def _inproj(xc, mods, cos_t, sin_t, w):
    nb, s, _ = xc.shape
    nt = s // ROW_TILE
    row = lambda width: pl.BlockSpec((1, ROW_TILE, width), lambda b, i: (b, i, 0))
    col = lambda rows: pl.BlockSpec((1, rows, ROW_TILE), lambda b, i: (b, 0, i))
    vt_rows = ATT_HEADS * VT_ROWS
    outs = [
        (jax.ShapeDtypeStruct((nb, s, 2 * BRANCH_W), BF16), row(2 * BRANCH_W)),
        (jax.ShapeDtypeStruct((nb, vt_rows, s), BF16), col(vt_rows)),
        (jax.ShapeDtypeStruct((nb, s, 2 * BRANCH_W), BF16), row(2 * BRANCH_W)),
        (jax.ShapeDtypeStruct((nb, s, BRANCH_W), BF16), row(BRANCH_W)),
        (jax.ShapeDtypeStruct((nb, s, 2 * BRANCH_W), BF16), row(2 * BRANCH_W)),
        (jax.ShapeDtypeStruct((nb, s, LANES), F32), row(LANES)),
        (jax.ShapeDtypeStruct((nb, 2 * SSD_HEADS, s), F32), col(2 * SSD_HEADS)),
        (jax.ShapeDtypeStruct((nb, s, 3 * D_MODEL), BF16), row(3 * D_MODEL)),
    ]
    weights = [w["qk"], w["vt"], w["ones"], w["lxg"], w["z"], w["xbc"], w["dt"], w["dtt"], w["g"]]
    return pl.pallas_call(
        _inproj_kernel,
        grid=(nb, nt),
        in_specs=[row(D_MODEL), _mod_spec(nb),
                  pl.BlockSpec((ROW_TILE, LANES), lambda b, i: (i, 0)),
                  pl.BlockSpec((ROW_TILE, LANES), lambda b, i: (i, 0))]
                 + [_resident(a.shape) for a in weights],
        out_specs=[o[1] for o in outs],
        out_shape=[o[0] for o in outs],
        compiler_params=_cparams(("arbitrary", "arbitrary")),
    )(xc, mods, cos_t, sin_t, *weights)


def _attn_kernel(lq_ref, lk_ref, g_ref, q_ref, qn_ref, k_ref, vt_ref, o_ref, sa_ref, sb_ref, m_ref, *, lam_init):
    i = pl.program_id(2)
    tq = q_ref.shape[1]
    nck = k_ref.shape[1] // KEY_CHUNK
    prod = lq_ref[...] * lk_ref[...]
    d0 = jnp.sum(prod[0:1, :], axis=1, keepdims=True)
    d1 = jnp.sum(prod[1:2, :], axis=1, keepdims=True)
    lam = jnp.exp(d0) - jnp.exp(d1) + lam_init
    gcol = g_ref[0] * (1.0 - lam_init)

    def stack_maps(q):
        lane = lax.broadcasted_iota(jnp.int32, q.shape, 1)
        zero = jnp.zeros_like(q)
        return jnp.concatenate([jnp.where(lane < ATT_DH, q, zero), jnp.where(lane >= ATT_DH, q, zero)], axis=0)

    def scores(c, q2, s_ref, m):
        rows = slice(c * KEY_CHUNK, (c + 1) * KEY_CHUNK)
        st = lax.dot_general(k_ref[0, rows, :], q2, (((1,), (1,)), ((), ())),
                             preferred_element_type=F32)
        s_ref[rows, :] = st
        mc = jnp.max(st, axis=0, keepdims=True)
        return mc if m is None else jnp.maximum(m, mc)

    def weigh(c, s_ref, m, acc):
        rows = slice(c * KEY_CHUNK, (c + 1) * KEY_CHUNK)
        e = jnp.exp2((s_ref[rows, :] - m).astype(BF16))
        pv = jnp.dot(vt_ref[0, :, rows], e, preferred_element_type=F32)
        return pv if acc is None else acc + pv

    def finish(acc):
        r = 1.0 / acc[ATT_DV:ATT_DV + 1, :]
        o = acc[:ATT_DV, :tq] * r[:, :tq] - lam * (acc[:ATT_DV, tq:] * r[:, tq:])
        ms = jnp.mean(o * o, axis=0, keepdims=True)
        o_ref[0] = (o * lax.rsqrt(ms + RMS_EPS) * gcol).T.astype(BF16)

    @pl.when(i == 0)
    def _():
        q2 = stack_maps(q_ref[0])
        finish(weigh(0, sa_ref, scores(0, q2, sa_ref, None), None))
        q2n = stack_maps(qn_ref[0])
        mn = None
        for c in range(nck):
            mn = scores(c, q2n, sb_ref, mn)
        m_ref[1:2, :] = mn

    def step(cur_ref, nxt_ref, cur_slot, nxt_slot):
        q2n = stack_maps(qn_ref[0])
        m_cur = m_ref[cur_slot:cur_slot + 1, :]
        mn = acc = None
        for c in range(nck):
            mn = scores(c, q2n, nxt_ref, mn)
            acc = weigh(c, cur_ref, m_cur, acc)
        finish(acc)
        m_ref[nxt_slot:nxt_slot + 1, :] = mn

    @pl.when(i % 2 == 1)
    def _():
        step(sb_ref, sa_ref, 1, 0)

    @pl.when(jnp.logical_and(i % 2 == 0, i > 0))
    def _():
        step(sa_ref, sb_ref, 0, 1)


def _attention(qk, vt, lam_q, lam_k, attn_g, lam_init):
    nb, s, _ = qk.shape
    nq = s // ROW_TILE
    return pl.pallas_call(
        functools.partial(_attn_kernel, lam_init=lam_init),
        grid=(nb, ATT_HEADS, nq),
        in_specs=[
            pl.BlockSpec((2, ATT_DH), lambda b, h, i: (0, 0)),
            pl.BlockSpec((2, ATT_DH), lambda b, h, i: (0, 0)),
            pl.BlockSpec((1, ATT_DV, 1), lambda b, h, i: (h, 0, 0)),
            pl.BlockSpec((1, ROW_TILE, LANES), lambda b, h, i: (b, i, h)),
            pl.BlockSpec((1, ROW_TILE, LANES), lambda b, h, i: (b, jnp.minimum(i + 1, nq - 1), h)),
            pl.BlockSpec((1, s, LANES), lambda b, h, i: (b, 0, ATT_HEADS + h)),
            pl.BlockSpec((1, VT_ROWS, s), lambda b, h, i: (b, h, 0)),
        ],
        out_specs=pl.BlockSpec((1, ROW_TILE, LANES), lambda b, h, i: (b, i, h)),
        out_shape=jax.ShapeDtypeStruct((nb, s, BRANCH_W), BF16),
        scratch_shapes=[pltpu.VMEM((s, 2 * ROW_TILE), F32), pltpu.VMEM((s, 2 * ROW_TILE), F32),
                        pltpu.VMEM((SUBLANES, 2 * ROW_TILE), F32)],
        compiler_params=_cparams(("arbitrary", "arbitrary", "arbitrary")),
    )(lam_q, lam_k, attn_g.reshape(ATT_HEADS, ATT_DV, 1), qk, qk, qk, vt)


def _scan_chunk(j, nc, reverse):
    if not reverse:
        return j
    return jnp.where(j == 0, 0, nc - j)


def _conv4(x, prev, nxt, c, nc, w_ref, b_ref):
    t = x.shape[0]
    row = lax.broadcasted_iota(jnp.int32, x.shape, 0)
    prev_ok = (c >= 2).astype(F32)
    next_ok = jnp.logical_and(c >= 1, c < nc - 1).astype(F32)
    p1 = prev[HALO - 1:HALO, :] * prev_ok
    p2 = prev[HALO - 2:HALO - 1, :] * prev_ok
    n0 = nxt[0:1, :] * next_ok
    xm1 = jnp.where(row == 0, p1, pltpu.roll(x, 1, 0))
    xm2 = jnp.where(row == 0, p2, jnp.where(row == 1, p1, pltpu.roll(x, 2, 0)))
    xp1 = jnp.where(row == t - 1, n0, pltpu.roll(x, t - 1, 0))
    return w_ref[0:1, :] * xm2 + w_ref[1:2, :] * xm1 + w_ref[2:3, :] * x + w_ref[3:4, :] * xp1 + b_ref[...]


def _tile_specs(width, nc, reverse, blk=0):
    per = ROW_TILE // HALO
    last = nc * per - 1
    ch = lambda j: _scan_chunk(j, nc, reverse)
    cur = pl.BlockSpec((1, ROW_TILE, width), lambda b, j: (b, ch(j), blk))
    prev = pl.BlockSpec((1, HALO, width), lambda b, j: (b, jnp.maximum(ch(j) * per - 1, 0), blk))
    nxt = pl.BlockSpec((1, HALO, width), lambda b, j: (b, jnp.minimum((ch(j) + 1) * per, last), blk))
    return cur, prev, nxt


def _rglru_kernel(*refs, reverse, nc):
    if reverse:
        (x_ref, xp_ref, xn_ref, cw_ref, cb_ref, wg_ref, ba_ref, bi_ref, lam_ref,
         yf_ref, lg_ref, o_ref, h_ref) = refs
    else:
        (x_ref, xp_ref, xn_ref, cw_ref, cb_ref, wg_ref, ba_ref, bi_ref, lam_ref, o_ref, h_ref) = refs
    j = pl.program_id(1)
    c = _scan_chunk(j, nc, reverse)

    @pl.when(j == 0)
    def _():
        h_ref[...] = jnp.zeros_like(h_ref)

    u = _conv4(x_ref[0].astype(F32), xp_ref[0].astype(F32), xn_ref[0].astype(F32), c, nc, cw_ref, cb_ref)
    t = u.shape[0]
    pre = jnp.dot(u.astype(BF16), wg_ref[...], preferred_element_type=F32)
    r = jax.nn.sigmoid(pre[:, :BRANCH_W] + ba_ref[...])
    gi = jax.nn.sigmoid(pre[:, BRANCH_W:] + bi_ref[...])
    log_a = (-LRU_C) * r * _softplus(-lam_ref[...])
    a = jnp.exp(log_a)
    bv = jnp.sqrt(1.0 - jnp.exp(2.0 * log_a)) * (gi * u)

    row = lax.broadcasted_iota(jnp.int32, u.shape, 0)
    d = 1
    while d < t:
        if reverse:
            keep = row < t - d
            sh = t - d
        else:
            keep = row >= d
            sh = d
        a_s = jnp.where(keep, pltpu.roll(a, sh, 0), 1.0)
        b_s = jnp.where(keep, pltpu.roll(bv, sh, 0), 0.0)
        bv = a * b_s + bv
        a = a * a_s
        d *= 2
    hs = a * h_ref[...] + bv
    h_ref[...] = hs[0:1, :] if reverse else hs[t - 1:t, :]
    if reverse:
        o_ref[0] = ((yf_ref[0] + hs) * _gelu_tanh(lg_ref[0].astype(F32))).astype(BF16)
    else:
        o_ref[0] = hs


def _rglru(lxg, conv_w, conv_b, w_gate, b_a, b_i, lam, reverse, yf=None):
    nb, s, _ = lxg.shape
    nc = s // ROW_TILE
    cur, prev, nxt = _tile_specs(BRANCH_W, nc, reverse)
    params = [conv_w, conv_b, w_gate, b_a, b_i, lam]
    in_specs = [cur, prev, nxt] + [_resident(p.shape) for p in params]
    args = [lxg, lxg, lxg] + params
    if reverse:
        in_specs += [cur, _tile_specs(BRANCH_W, nc, reverse, blk=1)[0]]
        args += [yf, lxg]
    return pl.pallas_call(
        functools.partial(_rglru_kernel, reverse=reverse, nc=nc),
        grid=(nb, nc),
        in_specs=in_specs,
        out_specs=cur,
        out_shape=jax.ShapeDtypeStruct((nb, s, BRANCH_W), BF16 if reverse else F32),
        scratch_shapes=[pltpu.VMEM((1, BRANCH_W), F32)],
        compiler_params=_cparams(("arbitrary", "arbitrary")),
    )(*args)


def _ssd_kernel(*refs, reverse, nc, direction):
    if reverse:
        (x_ref, xp_ref, xn_ref, dt_ref, dtt_ref, cw_ref, cb_ref, dtb_ref, dtbc_ref, alog_ref, alogc_ref,
         exp_ref, yf_ref, z_ref, skip_ref, g_ref, o_ref, st_ref) = refs
    else:
        (x_ref, xp_ref, xn_ref, dt_ref, dtt_ref, cw_ref, cb_ref, dtb_ref, dtbc_ref, alog_ref, alogc_ref,
         exp_ref, o_ref, st_ref) = refs
    j = pl.program_id(1)
    c = _scan_chunk(j, nc, reverse)

    @pl.when(j == 0)
    def _():
        st_ref[...] = jnp.zeros_like(st_ref)

    u = _silu(_conv4(x_ref[0].astype(F32), xp_ref[0].astype(F32), xn_ref[0].astype(F32), c, nc, cw_ref, cb_ref))
    t = u.shape[0]
    xs = u[:, :BRANCH_W]
    gw = SSD_HPG * SSD_HEADDIM

    dt_c = _softplus(dt_ref[0] + dtb_ref[...])
    a_c = dt_c * (-jnp.exp(alog_ref[...]))
    a_r = _softplus(dtt_ref[0] + dtbc_ref[...]) * (-jnp.exp(alogc_ref[...]))
    ri = lax.broadcasted_iota(jnp.int32, (t, t), 0)
    ci = lax.broadcasted_iota(jnp.int32, (t, t), 1)
    lower = (ci <= ri).astype(F32)
    upper = (ci >= ri).astype(F32)
    cs_c = jnp.dot(upper if reverse else lower, a_c, preferred_element_type=F32, precision=HIGHEST)
    cs_r = jnp.dot(a_r, lower if reverse else upper, preferred_element_type=F32, precision=HIGHEST)
    tot = cs_c[0:1, :] if reverse else cs_c[t - 1:t, :]
    keep = (ci >= ri) if reverse else (ci <= ri)

    expand = exp_ref[...]
    xdt = xs * jnp.dot(dt_c, expand, preferred_element_type=F32, precision=HIGHEST)
    e_in = jnp.dot(jnp.exp(cs_c), expand, preferred_element_type=F32, precision=HIGHEST)
    e_out = jnp.dot(jnp.exp(tot - cs_c), expand, preferred_element_type=F32, precision=HIGHEST)
    e_tot = e_in[0:1, :] if reverse else e_in[t - 1:t, :]
    xdt_b = xdt.astype(BF16)
    xdec_b = (xdt * e_out).astype(BF16)
    lane_head = lax.broadcasted_iota(jnp.int32, (t, gw), 1) // SSD_HEADDIM
    zero_b = jnp.zeros((t, gw), BF16)

    ys = []
    for g in range(SSD_GROUPS):
        bm = u[:, BRANCH_W + g * SSD_STATE:BRANCH_W + (g + 1) * SSD_STATE].astype(BF16)
        cm = u[:, BRANCH_W + (SSD_GROUPS + g) * SSD_STATE:BRANCH_W + (SSD_GROUPS + g + 1) * SSD_STATE].astype(BF16)
        cb = lax.dot_general(cm, bm, (((1,), (1,)), ((), ())), preferred_element_type=F32)
        st_g = st_ref[:, g * gw:(g + 1) * gw]
        y = jnp.dot(cm, st_g.astype(BF16), preferred_element_type=F32) * e_in[:, g * gw:(g + 1) * gw]
        xg = xdt_b[:, g * gw:(g + 1) * gw]
        for rr in range(SSD_HPG):
            col = direction * SSD_HEADS + g * SSD_HPG + rr
            decay = jnp.where(keep, jnp.exp(cs_c[:, col:col + 1] - cs_r[col:col + 1, :]), 0.0)
            gm = (cb * decay).astype(BF16)
            y = y + jnp.dot(gm, jnp.where(lane_head == rr, xg, zero_b), preferred_element_type=F32)
        ys.append(y)
        upd = lax.dot_general(bm, xdec_b[:, g * gw:(g + 1) * gw], (((0,), (0,)), ((), ())),
                              preferred_element_type=F32)
        st_ref[:, g * gw:(g + 1) * gw] = st_g * e_tot[:, g * gw:(g + 1) * gw] + upd
    y = jnp.concatenate(ys, axis=1)

    if reverse:
        y = yf_ref[0] + y + skip_ref[...] * xs
        tz = y * _silu(z_ref[0].astype(F32))
        outs = []
        for g in range(SSD_GROUPS):
            tg = tz[:, g * gw:(g + 1) * gw]
            outs.append(tg * lax.rsqrt(jnp.mean(tg * tg, axis=1, keepdims=True) + RMS_EPS))
        o_ref[0] = (jnp.concatenate(outs, axis=1) * g_ref[...]).astype(BF16)
    else:
        o_ref[0] = y


def _ssd(xbc, dt, dtt, p, direction, yf=None, z=None):
    reverse = direction == 1
    nb, s, _ = xbc.shape
    nc = s // ROW_TILE
    cur, prev, nxt = _tile_specs(2 * BRANCH_W, nc, reverse)
    ch = lambda j: _scan_chunk(j, nc, reverse)
    params = [p["conv_w"], p["conv_b"], p["dtb_row"], p["dtb_col"], p["alog_row"], p["alog_col"], p["expand"]]
    in_specs = [cur, prev, nxt,
                pl.BlockSpec((1, ROW_TILE, LANES), lambda b, j: (b, ch(j), 0)),
                pl.BlockSpec((1, 2 * SSD_HEADS, ROW_TILE), lambda b, j: (b, 0, ch(j)))]
    in_specs += [_resident(a.shape) for a in params]
    args = [xbc, xbc, xbc, dt, dtt] + params
    half = pl.BlockSpec((1, ROW_TILE, BRANCH_W), lambda b, j: (b, ch(j), 0))
    if reverse:
        in_specs += [half, half, _resident(p["skip"].shape), _resident(p["norm_g"].shape)]
        args += [yf, z, p["skip"], p["norm_g"]]
    return pl.pallas_call(
        functools.partial(_ssd_kernel, reverse=reverse, nc=nc, direction=direction),
        grid=(nb, nc),
        in_specs=in_specs,
        out_specs=half,
        out_shape=jax.ShapeDtypeStruct((nb, s, BRANCH_W), BF16 if reverse else F32),
        scratch_shapes=[pltpu.VMEM((SSD_STATE, BRANCH_W), F32)],
        compiler_params=_cparams(("arbitrary", "arbitrary")),
    )(*args)


def _merge_kernel(ya_ref, yb_ref, yc_ref, gt_ref, x_ref, m_ref, wb_ref, wo_ref, g_ref, b_ref,
                  x1_ref, h2_ref, *, alpha):
    acc = None
    for n, y_ref in enumerate((ya_ref, yb_ref, yc_ref)):
        proj = jnp.dot(y_ref[0], wb_ref[n], preferred_element_type=F32)
        gate = jax.nn.sigmoid(gt_ref[0, :, n * D_MODEL:(n + 1) * D_MODEL].astype(F32))
        acc = gate * proj if acc is None else acc + gate * proj
    mix = jnp.dot(acc.astype(BF16), wo_ref[...], preferred_element_type=F32)
    x1 = _layer_norm(alpha * x_ref[0] + m_ref[0, 2:3, :] * mix, g_ref[...], b_ref[...])
    x1_ref[0] = x1
    _store_token_rows(h2_ref, x1 * (1.0 + m_ref[0, 4:5, :]) + m_ref[0, 3:4, :])


def _merge(ya, yb, yc, gates, xc, mods, w_branch, w_out, ln_g, ln_b, alpha):
    nb, s, _ = xc.shape
    nt = s // ROW_TILE
    row = lambda width: pl.BlockSpec((1, ROW_TILE, width), lambda b, i: (b, i, 0))
    out = jax.ShapeDtypeStruct((nb, s, D_MODEL), F32)
    return pl.pallas_call(
        functools.partial(_merge_kernel, alpha=alpha),
        grid=(nb, nt),
        in_specs=[row(BRANCH_W), row(BRANCH_W), row(BRANCH_W), row(3 * D_MODEL), row(D_MODEL), _mod_spec(nb),
                  _resident(w_branch.shape), _resident(w_out.shape), _resident(ln_g.shape), _resident(ln_b.shape)],
        out_specs=[row(D_MODEL), pl.BlockSpec((ROW_TILE * NCHUNK, LANES), lambda b, i: (b * nt + i, 0))],
        out_shape=[out, jax.ShapeDtypeStruct((nb * s * NCHUNK, LANES), F32)],
        compiler_params=_cparams(("arbitrary", "arbitrary")),
    )(ya, yb, yc, gates, xc, mods, w_branch, w_out, ln_g, ln_b)


def _first_index(hit, ridx, n):
    return jnp.min(jnp.where(hit, ridx, n), axis=0, keepdims=True)


def _router_kernel(h_ref, wr_ref, bias_ref, cnt_ref, idx_ref, wl_ref):
    tm = h_ref.shape[0] // NCHUNK
    logits = lax.dot_general(wr_ref[...], _load_token_rows(h_ref, tm), (((1,), (1,)), ((), ())),
                             preferred_element_type=F32, precision=HIGHEST)
    scores = jax.nn.sigmoid(logits)
    sel = scores + bias_ref[...]
    neg = -jnp.inf

    r8 = lax.broadcasted_iota(jnp.int32, (GROUP_SIZE, tm), 0)
    grp = jnp.zeros((N_GROUPS, tm), F32)
    for g in range(N_GROUPS):
        blk = sel[g * GROUP_SIZE:(g + 1) * GROUP_SIZE, :]
        m1 = jnp.max(blk, axis=0, keepdims=True)
        first = _first_index(blk == m1, r8, GROUP_SIZE)
        m2 = jnp.max(jnp.where(r8 == first, neg, blk), axis=0, keepdims=True)
        grp = jnp.where(r8 == g, m1 + m2, grp)
    gsel = jnp.zeros((N_GROUPS, tm), jnp.int32)
    for _ in range(TOPK_GROUPS):
        m = jnp.max(grp, axis=0, keepdims=True)
        hit = r8 == _first_index(grp == m, r8, N_GROUPS)
        gsel = jnp.where(hit, 1, gsel)
        grp = jnp.where(hit, neg, grp)
    cand = jnp.concatenate(
        [jnp.where(gsel[g:g + 1, :] > 0, sel[g * GROUP_SIZE:(g + 1) * GROUP_SIZE, :], neg) for g in range(N_GROUPS)],
        axis=0)

    re = lax.broadcasted_iota(jnp.int32, (N_EXPERTS, tm), 0)
    chosen = jnp.zeros((N_EXPERTS, tm), jnp.int32)
    for _ in range(TOP_K):
        m = jnp.max(cand, axis=0, keepdims=True)
        hit = re == _first_index(cand == m, re, N_EXPERTS)
        chosen = jnp.where(hit, 1, chosen)
        cand = jnp.where(hit, neg, cand)
    picked = jnp.where(chosen > 0, scores, 0.0)
    wgt = picked / jnp.sum(picked, axis=0, keepdims=True) * ROUTED_SCALE

    li = lax.broadcasted_iota(jnp.int32, (LANES, LANES), 0)
    lj = lax.broadcasted_iota(jnp.int32, (LANES, LANES), 1)
    strict = (li < lj).astype(BF16)
    chosen_b = chosen.astype(F32).astype(BF16)
    base = jnp.zeros((N_EXPERTS, 1), F32)
    pos = []
    for kb in range(tm // LANES):
        cblk = chosen_b[:, kb * LANES:(kb + 1) * LANES]
        pos.append(jnp.dot(cblk, strict, preferred_element_type=F32) + base)
        base = base + jnp.sum(cblk.astype(F32), axis=1, keepdims=True)
    pos = jnp.concatenate(pos, axis=1).astype(jnp.int32)
    count = base.astype(jnp.int32)

    lane = lax.broadcasted_iota(jnp.int32, (N_EXPERTS, tm), 1)
    live = chosen
    dist = lane - pos
    val = lane
    wv = wgt
    step = 1
    while step < tm:
        move = jnp.where(jnp.logical_and(live > 0, (dist & step) != 0), 1, 0)
        sh = tm - step
        mv_in = pltpu.roll(move, sh, 1)
        take = mv_in > 0
        val = jnp.where(take, pltpu.roll(val, sh, 1), val)
        wv = jnp.where(take, pltpu.roll(wv, sh, 1), wv)
        dist = jnp.where(take, pltpu.roll(dist, sh, 1), dist)
        live = jnp.where(take, 1, jnp.where(move > 0, 0, live))
        step *= 2
    valid = lane < count
    cnt_ref[0] = count
    spare = jnp.full((N_EXPERTS, MOE_BLOCK), tm, jnp.int32)
    idx_ref[0] = jnp.concatenate([jnp.where(valid, val, tm), spare], axis=1)
    wl_ref[0] = jnp.where(valid, wv, 0.0)


def _router(h2_rows, w_router_t, bias_col):
    nt = h2_rows.shape[0] // (MOE_TILE * NCHUNK)
    return pl.pallas_call(
        _router_kernel,
        grid=(nt,),
        in_specs=[pl.BlockSpec((MOE_TILE * NCHUNK, LANES), lambda t: (t, 0)),
                  _resident(w_router_t.shape), _resident(bias_col.shape)],
        out_specs=[pl.BlockSpec((1, N_EXPERTS, 1), lambda t: (t, 0, 0)),
                   pl.BlockSpec((1, N_EXPERTS, MOE_LIST), lambda t: (t, 0, 0)),
                   pl.BlockSpec((1, N_EXPERTS, MOE_TILE), lambda t: (t, 0, 0))],
        out_shape=[jax.ShapeDtypeStruct((nt, N_EXPERTS, 1), jnp.int32),
                   jax.ShapeDtypeStruct((nt, N_EXPERTS, MOE_LIST), jnp.int32),
                   jax.ShapeDtypeStruct((nt, N_EXPERTS, MOE_TILE), F32)],
        compiler_params=_cparams(("arbitrary",)),
    )(h2_rows, w_router_t, bias_col)


SCATTER_GROUP = 16


def _experts_kernel(cnt_ref, idx_ref, wl_ref, h_ref, wu_ref, wd_ref, o_ref,
                    acc_ref, xa_ref, xb_ref, ya_ref, yb_ref):
    t = pl.program_id(0)
    e = pl.program_id(1)

    @pl.when(e == 0)
    def _():
        acc_ref[...] = jnp.zeros_like(acc_ref)

    count = cnt_ref[t * N_EXPERTS + e]
    nblk = (count + MOE_BLOCK - 1) // MOE_BLOCK
    ri = lax.broadcasted_iota(jnp.int32, (MOE_BLOCK, MOE_BLOCK), 0)
    ci = lax.broadcasted_iota(jnp.int32, (MOE_BLOCK, MOE_BLOCK), 1)

    def gather(blk, x_ref):
        base = blk * MOE_BLOCK
        for r in range(MOE_BLOCK):
            src = idx_ref[0, 0, base + r] & (MOE_TILE - 1)
            x_ref[pl.ds(r * NCHUNK, NCHUNK), :] = h_ref[pl.ds(pl.multiple_of(src * NCHUNK, NCHUNK), NCHUNK), :]

    def ffn(blk, x_ref, y_ref):
        x = _load_token_rows(x_ref, MOE_BLOCK)
        gu = jnp.dot(x.astype(BF16), wu_ref[0], preferred_element_type=F32)
        act = _silu(gu[:, :EXPERT_F]) * gu[:, EXPERT_F:]
        y = jnp.dot(act.astype(BF16), wd_ref[0], preferred_element_type=F32)
        w_row = wl_ref[0, pl.ds(blk, 1), :]
        w_col = jnp.sum(jnp.where(ri == ci, w_row, 0.0), axis=1, keepdims=True)
        _store_token_rows(y_ref, y * w_col)

    def scatter(blk, y_ref):
        base = blk * MOE_BLOCK
        for g0 in range(0, MOE_BLOCK, SCATTER_GROUP):
            rows = range(g0, g0 + SCATTER_GROUP)
            dst = [pl.multiple_of(idx_ref[0, 0, base + r] * NCHUNK, NCHUNK) for r in rows]
            new = [acc_ref[pl.ds(d, NCHUNK), :] + y_ref[pl.ds(r * NCHUNK, NCHUNK), :] for d, r in zip(dst, rows)]
            for d, v in zip(dst, new):
                acc_ref[pl.ds(d, NCHUNK), :] = v

    @pl.when(nblk > 0)
    def _():
        gather(0, xa_ref)

    def pair(j, carry):
        b0 = 2 * j
        ffn(b0, xa_ref, ya_ref)
        gather(b0 + 1, xb_ref)
        ffn(b0 + 1, xb_ref, yb_ref)
        scatter(b0, ya_ref)
        gather(b0 + 2, xa_ref)
        scatter(b0 + 1, yb_ref)
        return carry

    lax.fori_loop(0, nblk // 2, pair, 0)

    @pl.when(nblk % 2 == 1)
    def _():
        ffn(nblk - 1, xa_ref, ya_ref)
        scatter(nblk - 1, ya_ref)

    @pl.when(e == N_EXPERTS - 1)
    def _():
        o_ref[...] = acc_ref[pl.ds(0, MOE_TILE * NCHUNK), :]


def _experts(h2_rows, counts, idx, wl, w_up, w_down):
    nrow = h2_rows.shape[0]
    nt = nrow // (MOE_TILE * NCHUNK)
    tile = pl.BlockSpec((MOE_TILE * NCHUNK, LANES), lambda t, e, c: (t, 0))
    buf = pltpu.VMEM((MOE_BLOCK * NCHUNK, LANES), F32)
    grid_spec = pltpu.PrefetchScalarGridSpec(
        num_scalar_prefetch=1,
        grid=(nt, N_EXPERTS),
        in_specs=[pl.BlockSpec((1, 1, MOE_LIST), lambda t, e, c: (t * N_EXPERTS + e, 0, 0), memory_space=pltpu.SMEM),
                  pl.BlockSpec((1, MOE_TILE // MOE_BLOCK, MOE_BLOCK), lambda t, e, c: (t * N_EXPERTS + e, 0, 0)),
                  tile,
                  pl.BlockSpec((1, D_MODEL, 2 * EXPERT_F), lambda t, e, c: (e, 0, 0)),
                  pl.BlockSpec((1, EXPERT_F, D_MODEL), lambda t, e, c: (e, 0, 0))],
        out_specs=tile,
        scratch_shapes=[pltpu.VMEM(((MOE_TILE + 1) * NCHUNK, LANES), F32), buf, buf, buf, buf],
    )
    return pl.pallas_call(
        _experts_kernel,
        grid_spec=grid_spec,
        out_shape=jax.ShapeDtypeStruct((nrow, LANES), F32),
        compiler_params=_cparams(("arbitrary", "arbitrary")),
    )(counts, idx, wl, h2_rows, w_up, w_down)


def _ffn_out_kernel(x1_ref, h2_ref, fr_ref, m_ref, wu_ref, wd_ref, g_ref, b_ref, o_ref, *, alpha):
    gu = jnp.dot(_load_token_rows(h2_ref, ROW_TILE).astype(BF16), wu_ref[...], preferred_element_type=F32)
    act = _silu(gu[:, :EXPERT_F]) * gu[:, EXPERT_F:]
    f = jnp.dot(act.astype(BF16), wd_ref[...], preferred_element_type=F32) + _load_token_rows(fr_ref, ROW_TILE)
    o_ref[0] = _layer_norm(alpha * x1_ref[0] + m_ref[0, 5:6, :] * f, g_ref[...], b_ref[...])


def _ffn_out(x1, h2, fr, mods, ws_up, ws_down, ln_g, ln_b, alpha, latent_only):
    nb, s, _ = x1.shape
    nt = s // ROW_TILE
    skip = CTX_LEN // ROW_TILE if latent_only else 0
    row = pl.BlockSpec((1, ROW_TILE, D_MODEL), lambda b, i: (b, i + skip, 0))
    mod = pl.BlockSpec((1, 6, D_MODEL), lambda b, i: (jnp.where(i + skip == 0, nb, b), 0, 0))
    chunked = pl.BlockSpec((ROW_TILE * NCHUNK, LANES), lambda b, i: (b * nt + i + skip, 0))
    return pl.pallas_call(
        functools.partial(_ffn_out_kernel, alpha=alpha),
        grid=(nb, nt - skip),
        in_specs=[row, chunked, chunked, mod, _resident(ws_up.shape), _resident(ws_down.shape),
                  _resident(ln_g.shape), _resident(ln_b.shape)],
        out_specs=pl.BlockSpec((1, ROW_TILE, D_MODEL), lambda b, i: (b, i, 0)),
        out_shape=jax.ShapeDtypeStruct((nb, s - skip * ROW_TILE, D_MODEL), F32),
        compiler_params=_cparams(("arbitrary", "arbitrary")),
    )(x1, h2, fr, mods, ws_up, ws_down, ln_g, ln_b)


def _rope_tables(n_lat):
    t = jnp.arange(n_lat)
    rowp = (t // GRID_W).astype(F32)
    colp = (t % GRID_W).astype(F32)
    n_freq = ATT_DH // 4
    inv = ROPE_BASE ** (-jnp.arange(n_freq, dtype=F32) / n_freq)
    ang = jnp.concatenate([rowp[:, None] * inv, colp[:, None] * inv], axis=-1)
    lane = jnp.arange(LANES)
    cos = jnp.cos(ang)[:, lane % (ATT_DH // 2)]
    sign = jnp.where((lane % ATT_DH) < ATT_DH // 2, -1.0, 1.0).astype(F32)
    sin = jnp.sin(ang)[:, lane % (ATT_DH // 2)] * sign
    cos = jnp.concatenate([jnp.ones((CTX_LEN, LANES), F32), cos], axis=0)
    sin = jnp.concatenate([jnp.zeros((CTX_LEN, LANES), F32), sin], axis=0)
    return cos, sin


def _block_diag(w):
    n, k, _ = w.shape
    eye = jnp.eye(n, dtype=w.dtype)
    return (eye[:, None, :, None] * w[:, :, None, :]).reshape(n * k, n * k)


def _inproj_weights(w):
    o_v = 2 * BRANCH_W
    o_lx = o_v + ATT_HEADS * ATT_DV
    o_z = o_lx + 2 * BRANCH_W
    o_xbc = o_z + BRANCH_W
    o_dt = o_xbc + 2 * BRANCH_W
    o_g = o_dt + 2 * SSD_HEADS
    wv = w[:, o_v:o_lx].T.reshape(ATT_HEADS, ATT_DV, D_MODEL)
    wv = jnp.pad(wv, ((0, 0), (0, VT_ROWS - ATT_DV), (0, 0))).reshape(ATT_HEADS * VT_ROWS, D_MODEL)
    ones = jnp.zeros((ATT_HEADS, VT_ROWS, 1), F32).at[:, ATT_DV, 0].set(1.0).reshape(ATT_HEADS * VT_ROWS, 1)
    wdt = w[:, o_dt:o_g]
    return {
        "qk": w[:, :o_v].astype(BF16),
        "vt": wv.astype(BF16),
        "ones": ones,
        "lxg": w[:, o_lx:o_z].astype(BF16),
        "z": w[:, o_z:o_xbc].astype(BF16),
        "xbc": w[:, o_xbc:o_dt].astype(BF16),
        "dt": jnp.pad(wdt, ((0, 0), (0, LANES - 2 * SSD_HEADS))).astype(BF16),
        "dtt": wdt.T.astype(BF16),
        "g": w[:, o_g:].astype(BF16),
    }


def _pad_row(v):
    return jnp.pad(v.reshape(1, -1), ((0, 0), (0, LANES - v.size)))


def kernel(x, c, ctx, c_ctx, w_mod, b_mod, w_in, lam_q, lam_k, attn_norm_g, lru_conv_w, lru_conv_b, lru_wa, lru_ba, lru_wi, lru_bi, lru_lambda, ssd_conv_w, ssd_conv_b, ssd_dt_bias, ssd_a_log, ssd_d, ssd_norm_g, w_branch, w_out, ln1_g, ln1_b, w_router, router_bias, w_up, w_down, ws_up, ws_down, ln2_g, ln2_b):
    nb, n_lat, _ = x.shape
    depth = w_mod.shape[0]
    assert ctx.shape[1] == CTX_LEN and n_lat % ROW_TILE == 0 and nb + 1 <= 16
    s = CTX_LEN + n_lat
    assert (nb * s) % MOE_TILE == 0
    alpha = (2 * depth) ** 0.25

    xc = jnp.concatenate([ctx, x], axis=1)
    cc = jnp.zeros((16, D_MODEL), F32).at[:nb].set(c).at[nb].set(c_ctx)
    mods_all = _modulation(cc, w_mod, b_mod).reshape(depth, 16, 6, D_MODEL)
    cos_t, sin_t = _rope_tables(n_lat)
    head_of_channel = jnp.arange(BRANCH_W) // SSD_HEADDIM

    for l in range(depth):
        last = l == depth - 1
        lam_init = 0.8 - 0.6 * math.exp(-0.3 * l)
        mods = mods_all[l]
        qk, vt, lxg, z, xbc, dt, dtt, gates = _inproj(xc, mods, cos_t, sin_t, _inproj_weights(w_in[l]))

        ya = _attention(qk, vt, lam_q[l], lam_k[l], attn_norm_g[l], lam_init)

        yb = None
        for d in range(2):
            w_gate = jnp.concatenate([_block_diag(lru_wa[l, d]), _block_diag(lru_wi[l, d])], axis=1).astype(BF16)
            yb = _rglru(lxg, lru_conv_w[l], lru_conv_b[l].reshape(1, -1), w_gate,
                        lru_ba[l, d].reshape(1, -1), lru_bi[l, d].reshape(1, -1), lru_lambda[l, d].reshape(1, -1),
                        reverse=(d == 1), yf=yb)

        ssd_p = {
            "conv_w": ssd_conv_w[l], "conv_b": ssd_conv_b[l].reshape(1, -1),
            "dtb_row": _pad_row(ssd_dt_bias[l]), "dtb_col": ssd_dt_bias[l].reshape(-1, 1),
            "alog_row": _pad_row(ssd_a_log[l]), "alog_col": ssd_a_log[l].reshape(-1, 1),
            "skip": jnp.repeat(ssd_d[l], SSD_HEADDIM).reshape(1, -1), "norm_g": ssd_norm_g[l].reshape(1, -1),
        }
        yc = None
        for d in range(2):
            ssd_p["expand"] = (jnp.arange(LANES)[:, None] == d * SSD_HEADS + head_of_channel[None, :]).astype(F32)
            yc = _ssd(xbc, dt, dtt, ssd_p, d, yf=yc, z=z)

        x1, h2 = _merge(ya, yb, yc, gates, xc, mods, w_branch[l].astype(BF16), w_out[l].astype(BF16),
                        ln1_g[l].reshape(1, -1), ln1_b[l].reshape(1, -1), alpha)

        n_tok = nb * s
        counts, idx, wl = _router(h2, w_router[l].T, router_bias[l].reshape(-1, 1))
        nt = n_tok // MOE_TILE
        fr = _experts(h2, counts.reshape(nt * N_EXPERTS), idx.reshape(nt * N_EXPERTS, 1, MOE_LIST),
                      wl.reshape(nt * N_EXPERTS, MOE_TILE // MOE_BLOCK, MOE_BLOCK),
                      w_up[l].astype(BF16), w_down[l].astype(BF16))
        xc = _ffn_out(x1, h2, fr, mods, ws_up[l].astype(BF16), ws_down[l].astype(BF16),
                      ln2_g[l].reshape(1, -1), ln2_b[l].reshape(1, -1), alpha, latent_only=last)
    return xc
```

```python
import functools
import math

import jax
import jax.numpy as jnp
from jax import lax
from jax.experimental import pallas as pl
from jax.experimental.pallas import tpu as pltpu

F32 = jnp.float32
BF16 = jnp.bfloat16
HIGHEST = lax.Precision.HIGHEST

D_MODEL = 1024
GRID_W = 64
CTX_LEN = 256
BRANCH_W = 512
ATT_HEADS = 4
ATT_DH = 64
ATT_DV = 128
ROPE_BASE = 10000.0
LRU_C = 8.0
SSD_HEADS = 8
SSD_HEADDIM = 64
SSD_HPG = 4
SSD_GROUPS = 2
SSD_STATE = 128
N_EXPERTS = 64
N_GROUPS = 8
GROUP_SIZE = N_EXPERTS // N_GROUPS
TOP_K = 8
TOPK_GROUPS = 4
EXPERT_F = 256
ROUTED_SCALE = 2.5
LN_EPS = 1e-5
RMS_EPS = 1e-6

ROW_TILE = 256
HALO = 16
VT_ROWS = 144
KEY_CHUNK = 256
MOE_BLOCK = 128
MOE_PAD = ROW_TILE
ROUTER_CHUNK = 512
SUBLANES = 8
LANES = 128
VMEM_LIMIT = 56 * 1024 * 1024
LOG2E = 1.4426950408889634


def _cparams(sem):
    return pltpu.CompilerParams(dimension_semantics=sem, vmem_limit_bytes=VMEM_LIMIT)


def _resident(shape):
    nd = len(shape)
    return pl.BlockSpec(shape, lambda *_: (0,) * nd, pipeline_mode=pl.Buffered(1))


def _silu(x):
    return x * jax.nn.sigmoid(x)


def _softplus(x):
    return jnp.maximum(x, 0.0) + jnp.log1p(jnp.exp(-jnp.abs(x)))


def _gelu_tanh(x):
    return 0.5 * x * (1.0 + jnp.tanh(math.sqrt(2.0 / math.pi) * (x + 0.044715 * (x * x * x))))


NCHUNK = D_MODEL // LANES


def _load_token_rows(ref, n, first=0):
    return jnp.concatenate([ref[pl.ds(first * NCHUNK + s, n, stride=NCHUNK), :] for s in range(NCHUNK)], axis=1)


def _store_token_rows(ref, x):
    n = x.shape[0]
    for s in range(NCHUNK):
        ref[pl.ds(s, n, stride=NCHUNK), :] = x[:, s * LANES:(s + 1) * LANES]


def _layer_norm(x, g, b):
    mu = jnp.mean(x, axis=-1, keepdims=True)
    xc = x - mu
    var = jnp.mean(xc * xc, axis=-1, keepdims=True)
    return xc * lax.rsqrt(var + LN_EPS) * g + b


def _mod_kernel(c_ref, w_ref, b_ref, o_ref):
    s = _silu(c_ref[...])
    o_ref[0] = jnp.dot(s, w_ref[0], preferred_element_type=F32, precision=HIGHEST) + b_ref[0]


def _modulation(cc, w_mod, b_mod):
    depth = w_mod.shape[0]
    nblk = 6
    return pl.pallas_call(
        _mod_kernel,
        grid=(depth, nblk),
        in_specs=[
            pl.BlockSpec((16, D_MODEL), lambda l, j: (0, 0)),
            pl.BlockSpec((1, D_MODEL, D_MODEL), lambda l, j: (l, 0, j)),
            pl.BlockSpec((1, 1, D_MODEL), lambda l, j: (l, 0, j)),
        ],
        out_specs=pl.BlockSpec((1, 16, D_MODEL), lambda l, j: (l, 0, j)),
        out_shape=jax.ShapeDtypeStruct((depth, 16, nblk * D_MODEL), F32),
        compiler_params=_cparams(("arbitrary", "arbitrary")),
    )(cc, w_mod, b_mod.reshape(depth, 1, nblk * D_MODEL))


def _mod_spec(nb):
    return pl.BlockSpec((1, 6, D_MODEL), lambda b, i: (jnp.where(i == 0, nb, b), 0, 0))


def _inproj_kernel(x_ref, m_ref, cos_ref, sin_ref, wqk_ref, wvt_ref, ones_ref, wlxg_ref, wz_ref,
                   wxbc_ref, wdt_ref, wdtt_ref, wg_ref,
                   qk_ref, vt_ref, lxg_ref, z_ref, xbc_ref, dt_ref, dtt_ref, g_ref):
    x = x_ref[0]
    h = (x * (1.0 + m_ref[0, 1:2, :]) + m_ref[0, 0:1, :]).astype(BF16)
    nt = (((1,), (1,)), ((), ()))

    qk = jnp.dot(h, wqk_ref[...], preferred_element_type=F32)
    cos = cos_ref[...]
    sin = sin_ref[...]
    lane = lax.broadcasted_iota(jnp.int32, cos.shape, 1)
    first_half = (lane % ATT_DH) < (ATT_DH // 2)
    for j in range(2 * ATT_HEADS):
        blk = qk[:, j * LANES:(j + 1) * LANES]
        partner = jnp.where(first_half, pltpu.roll(blk, LANES - ATT_DH // 2, 1),
                            pltpu.roll(blk, ATT_DH // 2, 1))
        r = blk * cos + partner * sin
        if j < ATT_HEADS:
            r = r * (ATT_DH ** -0.5 * LOG2E)
        qk_ref[0, :, j * LANES:(j + 1) * LANES] = r.astype(BF16)

    vt = lax.dot_general(wvt_ref[...], h, nt, preferred_element_type=F32) + ones_ref[...]
    vt_ref[0] = vt.astype(BF16)
    lxg_ref[0] = jnp.dot(h, wlxg_ref[...], preferred_element_type=F32).astype(BF16)
    z_ref[0] = jnp.dot(h, wz_ref[...], preferred_element_type=F32).astype(BF16)
    xbc_ref[0] = jnp.dot(h, wxbc_ref[...], preferred_element_type=F32).astype(BF16)
    dt_ref[0] = jnp.dot(h, wdt_ref[...], preferred_element_type=F32)
    dtt_ref[0] = lax.dot_general(wdtt_ref[...], h, nt, preferred_element_type=F32)
    g_ref[0] = jnp.dot(h, wg_ref[...], preferred_element_type=F32).astype(BF16)


def _inproj(xc, mods, cos_t, sin_t, w):
    nb, s, _ = xc.shape
    nt = s // ROW_TILE
    row = lambda width: pl.BlockSpec((1, ROW_TILE, width), lambda b, i: (b, i, 0))
    col = lambda rows: pl.BlockSpec((1, rows, ROW_TILE), lambda b, i: (b, 0, i))
    vt_rows = ATT_HEADS * VT_ROWS
    outs = [
        (jax.ShapeDtypeStruct((nb, s, 2 * BRANCH_W), BF16), row(2 * BRANCH_W)),
        (jax.ShapeDtypeStruct((nb, vt_rows, s), BF16), col(vt_rows)),
        (jax.ShapeDtypeStruct((nb, s, 2 * BRANCH_W), BF16), row(2 * BRANCH_W)),
        (jax.ShapeDtypeStruct((nb, s, BRANCH_W), BF16), row(BRANCH_W)),
        (jax.ShapeDtypeStruct((nb, s, 2 * BRANCH_W), BF16), row(2 * BRANCH_W)),
        (jax.ShapeDtypeStruct((nb, s, LANES), F32), row(LANES)),
        (jax.ShapeDtypeStruct((nb, 2 * SSD_HEADS, s), F32), col(2 * SSD_HEADS)),
        (jax.ShapeDtypeStruct((nb, s, 3 * D_MODEL), BF16), row(3 * D_MODEL)),
    ]
    weights = [w["qk"], w["vt"], w["ones"], w["lxg"], w["z"], w["xbc"], w["dt"], w["dtt"], w["g"]]
    return pl.pallas_call(
        _inproj_kernel,
        grid=(nb, nt),
        in_specs=[row(D_MODEL), _mod_spec(nb),
                  pl.BlockSpec((ROW_TILE, LANES), lambda b, i: (i, 0)),
                  pl.BlockSpec((ROW_TILE, LANES), lambda b, i: (i, 0))]
                 + [_resident(a.shape) for a in weights],
        out_specs=[o[1] for o in outs],
        out_shape=[o[0] for o in outs],
        compiler_params=_cparams(("arbitrary", "arbitrary")),
    )(xc, mods, cos_t, sin_t, *weights)


def _attn_kernel(lq_ref, lk_ref, g_ref, q_ref, qn_ref, k_ref, vt_ref, o_ref, sa_ref, sb_ref, m_ref, *, lam_init):
    i = pl.program_id(2)
    tq = q_ref.shape[1]
    nck = k_ref.shape[1] // KEY_CHUNK
    prod = lq_ref[...] * lk_ref[...]
    d0 = jnp.sum(prod[0:1, :], axis=1, keepdims=True)
    d1 = jnp.sum(prod[1:2, :], axis=1, keepdims=True)
    lam = jnp.exp(d0) - jnp.exp(d1) + lam_init
    gcol = g_ref[0] * (1.0 - lam_init)

    def stack_maps(q):
        lane = lax.broadcasted_iota(jnp.int32, q.shape, 1)
        zero = jnp.zeros_like(q)
        return jnp.concatenate([jnp.where(lane < ATT_DH, q, zero), jnp.where(lane >= ATT_DH, q, zero)], axis=0)

    def scores(c, q2, s_ref, m):
        rows = slice(c * KEY_CHUNK, (c + 1) * KEY_CHUNK)
        st = lax.dot_general(k_ref[0, rows, :], q2, (((1,), (1,)), ((), ())),
                             preferred_element_type=F32)
        s_ref[rows, :] = st
        mc = jnp.max(st, axis=0, keepdims=True)
        return mc if m is None else jnp.maximum(m, mc)

    def weigh(c, s_ref, m, acc):
        rows = slice(c * KEY_CHUNK, (c + 1) * KEY_CHUNK)
        e = jnp.exp2((s_ref[rows, :] - m).astype(BF16))
        pv = jnp.dot(vt_ref[0, :, rows], e, preferred_element_type=F32)
        return pv if acc is None else acc + pv

    def finish(acc):
        r = 1.0 / acc[ATT_DV:ATT_DV + 1, :]
        o = acc[:ATT_DV, :tq] * r[:, :tq] - lam * (acc[:ATT_DV, tq:] * r[:, tq:])
        ms = jnp.mean(o * o, axis=0, keepdims=True)
        o_ref[0] = (o * lax.rsqrt(ms + RMS_EPS) * gcol).T.astype(BF16)

    @pl.when(i == 0)
    def _():
        q2 = stack_maps(q_ref[0])
        finish(weigh(0, sa_ref, scores(0, q2, sa_ref, None), None))
        q2n = stack_maps(qn_ref[0])
        mn = None
        for c in range(nck):
            mn = scores(c, q2n, sb_ref, mn)
        m_ref[1:2, :] = mn

    def step(cur_ref, nxt_ref, cur_slot, nxt_slot):
        q2n = stack_maps(qn_ref[0])
        m_cur = m_ref[cur_slot:cur_slot + 1, :]
        mn = acc = None
        for c in range(nck):
            mn = scores(c, q2n, nxt_ref, mn)
            acc = weigh(c, cur_ref, m_cur, acc)
        finish(acc)
        m_ref[nxt_slot:nxt_slot + 1, :] = mn

    @pl.when(i % 2 == 1)
    def _():
        step(sb_ref, sa_ref, 1, 0)

    @pl.when(jnp.logical_and(i % 2 == 0, i > 0))
    def _():
        step(sa_ref, sb_ref, 0, 1)


def _attention(qk, vt, lam_q, lam_k, attn_g, lam_init):
    nb, s, _ = qk.shape
    nq = s // ROW_TILE
    return pl.pallas_call(
        functools.partial(_attn_kernel, lam_init=lam_init),
        grid=(nb, ATT_HEADS, nq),
        in_specs=[
            pl.BlockSpec((2, ATT_DH), lambda b, h, i: (0, 0)),
            pl.BlockSpec((2, ATT_DH), lambda b, h, i: (0, 0)),
            pl.BlockSpec((1, ATT_DV, 1), lambda b, h, i: (h, 0, 0)),
            pl.BlockSpec((1, ROW_TILE, LANES), lambda b, h, i: (b, i, h)),
            pl.BlockSpec((1, ROW_TILE, LANES), lambda b, h, i: (b, jnp.minimum(i + 1, nq - 1), h)),
            pl.BlockSpec((1, s, LANES), lambda b, h, i: (b, 0, ATT_HEADS + h)),
            pl.BlockSpec((1, VT_ROWS, s), lambda b, h, i: (b, h, 0)),
        ],
        out_specs=pl.BlockSpec((1, ROW_TILE, LANES), lambda b, h, i: (b, i, h)),
        out_shape=jax.ShapeDtypeStruct((nb, s, BRANCH_W), BF16),
        scratch_shapes=[pltpu.VMEM((s, 2 * ROW_TILE), F32), pltpu.VMEM((s, 2 * ROW_TILE), F32),
                        pltpu.VMEM((SUBLANES, 2 * ROW_TILE), F32)],
        compiler_params=_cparams(("arbitrary", "arbitrary", "arbitrary")),
    )(lam_q, lam_k, attn_g.reshape(ATT_HEADS, ATT_DV, 1), qk, qk, qk, vt)


def _scan_chunk(j, nc, reverse):
    if not reverse:
        return j
    return jnp.where(j == 0, 0, nc - j)


def _conv4(x, prev, nxt, c, nc, w_ref, b_ref):
    t = x.shape[0]
    row = lax.broadcasted_iota(jnp.int32, x.shape, 0)
    prev_ok = (c >= 2).astype(F32)
    next_ok = jnp.logical_and(c >= 1, c < nc - 1).astype(F32)
    p1 = prev[HALO - 1:HALO, :] * prev_ok
    p2 = prev[HALO - 2:HALO - 1, :] * prev_ok
    n0 = nxt[0:1, :] * next_ok
    xm1 = jnp.where(row == 0, p1, pltpu.roll(x, 1, 0))
    xm2 = jnp.where(row == 0, p2, jnp.where(row == 1, p1, pltpu.roll(x, 2, 0)))
    xp1 = jnp.where(row == t - 1, n0, pltpu.roll(x, t - 1, 0))
    return w_ref[0:1, :] * xm2 + w_ref[1:2, :] * xm1 + w_ref[2:3, :] * x + w_ref[3:4, :] * xp1 + b_ref[...]


def _tile_specs(width, nc, reverse, blk=0):
    per = ROW_TILE // HALO
    last = nc * per - 1
    ch = lambda j: _scan_chunk(j, nc, reverse)
    cur = pl.BlockSpec((1, ROW_TILE, width), lambda b, j: (b, ch(j), blk))
    prev = pl.BlockSpec((1, HALO, width), lambda b, j: (b, jnp.maximum(ch(j) * per - 1, 0), blk))
    nxt = pl.BlockSpec((1, HALO, width), lambda b, j: (b, jnp.minimum((ch(j) + 1) * per, last), blk))
    return cur, prev, nxt


def _rglru_kernel(*refs, reverse, nc):
    if reverse:
        (x_ref, xp_ref, xn_ref, cw_ref, cb_ref, wg_ref, ba_ref, bi_ref, lam_ref,
         yf_ref, lg_ref, o_ref, h_ref) = refs
    else:
        (x_ref, xp_ref, xn_ref, cw_ref, cb_ref, wg_ref, ba_ref, bi_ref, lam_ref, o_ref, h_ref) = refs
    j = pl.program_id(1)
    c = _scan_chunk(j, nc, reverse)

    @pl.when(j == 0)
    def _():
        h_ref[...] = jnp.zeros_like(h_ref)

    u = _conv4(x_ref[0].astype(F32), xp_ref[0].astype(F32), xn_ref[0].astype(F32), c, nc, cw_ref, cb_ref)
    t = u.shape[0]
    pre = jnp.dot(u.astype(BF16), wg_ref[...], preferred_element_type=F32)
    r = jax.nn.sigmoid(pre[:, :BRANCH_W] + ba_ref[...])
    gi = jax.nn.sigmoid(pre[:, BRANCH_W:] + bi_ref[...])
    log_a = (-LRU_C) * r * _softplus(-lam_ref[...])
    a = jnp.exp(log_a)
    bv = jnp.sqrt(1.0 - jnp.exp(2.0 * log_a)) * (gi * u)

    in_group = lax.broadcasted_iota(jnp.int32, u.shape, 0) % SUBLANES
    d = 1
    while d < SUBLANES:
        if reverse:
            keep = in_group < SUBLANES - d
            sh = t - d
        else:
            keep = in_group >= d
            sh = d
        a_s = jnp.where(keep, pltpu.roll(a, sh, 0), 1.0)
        b_s = jnp.where(keep, pltpu.roll(bv, sh, 0), 0.0)
        bv = a * b_s + bv
        a = a * a_s
        d *= 2
    carry = jnp.broadcast_to(h_ref[...], (SUBLANES, BRANCH_W))
    ngroup = t // SUBLANES
    pieces = [None] * ngroup
    for gidx in (range(ngroup - 1, -1, -1) if reverse else range(ngroup)):
        rows = slice(gidx * SUBLANES, (gidx + 1) * SUBLANES)
        pieces[gidx] = a[rows, :] * carry + bv[rows, :]
        edge = pieces[gidx][0:1, :] if reverse else pieces[gidx][SUBLANES - 1:SUBLANES, :]
        carry = jnp.broadcast_to(edge, (SUBLANES, BRANCH_W))
    h_ref[...] = carry[0:1, :]
    hs = jnp.concatenate(pieces, axis=0)
    if reverse:
        o_ref[0] = ((yf_ref[0] + hs) * _gelu_tanh(lg_ref[0].astype(F32))).astype(BF16)
    else:
        o_ref[0] = hs


def _rglru(lxg, conv_w, conv_b, w_gate, b_a, b_i, lam, reverse, yf=None):
    nb, s, _ = lxg.shape
    nc = s // ROW_TILE
    cur, prev, nxt = _tile_specs(BRANCH_W, nc, reverse)
    params = [conv_w, conv_b, w_gate, b_a, b_i, lam]
    in_specs = [cur, prev, nxt] + [_resident(p.shape) for p in params]
    args = [lxg, lxg, lxg] + params
    if reverse:
        in_specs += [cur, _tile_specs(BRANCH_W, nc, reverse, blk=1)[0]]
        args += [yf, lxg]
    return pl.pallas_call(
        functools.partial(_rglru_kernel, reverse=reverse, nc=nc),
        grid=(nb, nc),
        in_specs=in_specs,
        out_specs=cur,
        out_shape=jax.ShapeDtypeStruct((nb, s, BRANCH_W), BF16 if reverse else F32),
        scratch_shapes=[pltpu.VMEM((1, BRANCH_W), F32)],
        compiler_params=_cparams(("arbitrary", "arbitrary")),
    )(*args)


def _ssd_kernel(*refs, reverse, nc, direction):
    if reverse:
        (x_ref, xp_ref, xn_ref, dt_ref, dtt_ref, cw_ref, cb_ref, dtb_ref, dtbc_ref, alog_ref, alogc_ref,
         exp_ref, yf_ref, z_ref, skip_ref, g_ref, o_ref, st_ref) = refs
    else:
        (x_ref, xp_ref, xn_ref, dt_ref, dtt_ref, cw_ref, cb_ref, dtb_ref, dtbc_ref, alog_ref, alogc_ref,
         exp_ref, o_ref, st_ref) = refs
    j = pl.program_id(1)
    c = _scan_chunk(j, nc, reverse)

    @pl.when(j == 0)
    def _():
        st_ref[...] = jnp.zeros_like(st_ref)

    u = _silu(_conv4(x_ref[0].astype(F32), xp_ref[0].astype(F32), xn_ref[0].astype(F32), c, nc, cw_ref, cb_ref))
    t = u.shape[0]
    xs = u[:, :BRANCH_W]
    gw = SSD_HPG * SSD_HEADDIM

    dt_c = _softplus(dt_ref[0] + dtb_ref[...])
    a_c = dt_c * (-jnp.exp(alog_ref[...]))
    a_r = _softplus(dtt_ref[0] + dtbc_ref[...]) * (-jnp.exp(alogc_ref[...]))
    ri = lax.broadcasted_iota(jnp.int32, (t, t), 0)
    ci = lax.broadcasted_iota(jnp.int32, (t, t), 1)
    lower = (ci <= ri).astype(F32)
    upper = (ci >= ri).astype(F32)
    cs_c = jnp.dot(upper if reverse else lower, a_c, preferred_element_type=F32, precision=HIGHEST)
    cs_r = jnp.dot(a_r, lower if reverse else upper, preferred_element_type=F32, precision=HIGHEST)
    tot = cs_c[0:1, :] if reverse else cs_c[t - 1:t, :]
    keep = (ci >= ri) if reverse else (ci <= ri)

    expand = exp_ref[...]
    xdt = xs * jnp.dot(dt_c.astype(BF16), expand, preferred_element_type=F32)
    e_in = jnp.dot(jnp.exp(cs_c).astype(BF16), expand, preferred_element_type=F32)
    e_out = jnp.dot(jnp.exp(tot - cs_c).astype(BF16), expand, preferred_element_type=F32)
    e_tot = jnp.dot(jnp.broadcast_to(jnp.exp(tot), (SUBLANES, LANES)), expand.astype(F32),
                    preferred_element_type=F32, precision=HIGHEST)[0:1, :]
    xdt_b = xdt.astype(BF16)
    xdec_b = (xdt * e_out).astype(BF16)
    lane_head = lax.broadcasted_iota(jnp.int32, (t, gw), 1) // SSD_HEADDIM
    zero_b = jnp.zeros((t, gw), BF16)

    ys = []
    for g in range(SSD_GROUPS):
        bm = u[:, BRANCH_W + g * SSD_STATE:BRANCH_W + (g + 1) * SSD_STATE].astype(BF16)
        cm = u[:, BRANCH_W + (SSD_GROUPS + g) * SSD_STATE:BRANCH_W + (SSD_GROUPS + g + 1) * SSD_STATE].astype(BF16)
        cb = lax.dot_general(cm, bm, (((1,), (1,)), ((), ())), preferred_element_type=F32)
        st_g = st_ref[:, g * gw:(g + 1) * gw]
        y = jnp.dot(cm, st_g.astype(BF16), preferred_element_type=F32) * e_in[:, g * gw:(g + 1) * gw]
        xg = xdt_b[:, g * gw:(g + 1) * gw]
        for rr in range(SSD_HPG):
            col = direction * SSD_HEADS + g * SSD_HPG + rr
            decay = jnp.where(keep, jnp.exp(cs_c[:, col:col + 1] - cs_r[col:col + 1, :]), 0.0)
            gm = (cb * decay).astype(BF16)
            y = y + jnp.dot(gm, jnp.where(lane_head == rr, xg, zero_b), preferred_element_type=F32)
        ys.append(y)
        upd = lax.dot_general(bm, xdec_b[:, g * gw:(g + 1) * gw], (((0,), (0,)), ((), ())),
                              preferred_element_type=F32)
        st_ref[:, g * gw:(g + 1) * gw] = st_g * e_tot[:, g * gw:(g + 1) * gw] + upd
    y = jnp.concatenate(ys, axis=1)

    if reverse:
        y = yf_ref[0] + y + skip_ref[...] * xs
        tz = y * _silu(z_ref[0].astype(F32))
        outs = []
        for g in range(SSD_GROUPS):
            tg = tz[:, g * gw:(g + 1) * gw]
            outs.append(tg * lax.rsqrt(jnp.mean(tg * tg, axis=1, keepdims=True) + RMS_EPS))
        o_ref[0] = (jnp.concatenate(outs, axis=1) * g_ref[...]).astype(BF16)
    else:
        o_ref[0] = y


def _ssd(xbc, dt, dtt, p, direction, yf=None, z=None):
    reverse = direction == 1
    nb, s, _ = xbc.shape
    nc = s // ROW_TILE
    cur, prev, nxt = _tile_specs(2 * BRANCH_W, nc, reverse)
    ch = lambda j: _scan_chunk(j, nc, reverse)
    params = [p["conv_w"], p["conv_b"], p["dtb_row"], p["dtb_col"], p["alog_row"], p["alog_col"], p["expand"]]
    in_specs = [cur, prev, nxt,
                pl.BlockSpec((1, ROW_TILE, LANES), lambda b, j: (b, ch(j), 0)),
                pl.BlockSpec((1, 2 * SSD_HEADS, ROW_TILE), lambda b, j: (b, 0, ch(j)))]
    in_specs += [_resident(a.shape) for a in params]
    args = [xbc, xbc, xbc, dt, dtt] + params
    half = pl.BlockSpec((1, ROW_TILE, BRANCH_W), lambda b, j: (b, ch(j), 0))
    if reverse:
        in_specs += [half, half, _resident(p["skip"].shape), _resident(p["norm_g"].shape)]
        args += [yf, z, p["skip"], p["norm_g"]]
    return pl.pallas_call(
        functools.partial(_ssd_kernel, reverse=reverse, nc=nc, direction=direction),
        grid=(nb, nc),
        in_specs=in_specs,
        out_specs=half,
        out_shape=jax.ShapeDtypeStruct((nb, s, BRANCH_W), BF16 if reverse else F32),
        scratch_shapes=[pltpu.VMEM((SSD_STATE, BRANCH_W), F32)],
        compiler_params=_cparams(("arbitrary", "arbitrary")),
    )(*args)


def _merge_kernel(ya_ref, yb_ref, yc_ref, gt_ref, x_ref, m_ref, wb_ref, wo_ref, g_ref, b_ref, h2_init_ref,
                  x1_ref, h2_ref, *, alpha):
    del h2_init_ref
    acc = None
    for n, y_ref in enumerate((ya_ref, yb_ref, yc_ref)):
        proj = jnp.dot(y_ref[0], wb_ref[n], preferred_element_type=F32)
        gate = jax.nn.sigmoid(gt_ref[0, :, n * D_MODEL:(n + 1) * D_MODEL].astype(F32))
        acc = gate * proj if acc is None else acc + gate * proj
    mix = jnp.dot(acc.astype(BF16), wo_ref[...], preferred_element_type=F32)
    x1 = _layer_norm(alpha * x_ref[0] + m_ref[0, 2:3, :] * mix, g_ref[...], b_ref[...])
    x1_ref[0] = x1
    _store_token_rows(h2_ref.at[0], x1 * (1.0 + m_ref[0, 4:5, :]) + m_ref[0, 3:4, :])


def _merge(ya, yb, yc, gates, xc, mods, w_branch, w_out, ln_g, ln_b, alpha):
    nb, s, _ = xc.shape
    nt = s // ROW_TILE
    row = lambda width: pl.BlockSpec((1, ROW_TILE, width), lambda b, i: (b, i, 0))
    h2_init = jnp.zeros((nb, (s + MOE_PAD) * NCHUNK, LANES), F32)
    return pl.pallas_call(
        functools.partial(_merge_kernel, alpha=alpha),
        grid=(nb, nt),
        in_specs=[row(BRANCH_W), row(BRANCH_W), row(BRANCH_W), row(3 * D_MODEL), row(D_MODEL), _mod_spec(nb),
                  _resident(w_branch.shape), _resident(w_out.shape), _resident(ln_g.shape), _resident(ln_b.shape),
                  pl.BlockSpec(memory_space=pl.ANY)],
        out_specs=[row(D_MODEL), pl.BlockSpec((1, ROW_TILE * NCHUNK, LANES), lambda b, i: (b, i, 0))],
        out_shape=[jax.ShapeDtypeStruct((nb, s, D_MODEL), F32), jax.ShapeDtypeStruct(h2_init.shape, F32)],
        input_output_aliases={10: 1},
        compiler_params=_cparams(("arbitrary", "arbitrary")),
    )(ya, yb, yc, gates, xc, mods, w_branch, w_out, ln_g, ln_b, h2_init)


def _first_index(hit, ridx, n):
    return jnp.min(jnp.where(hit, ridx, n), axis=0, keepdims=True)


def _route_kernel(h_ref, wr_ref, bias_ref, cnt_ref, rank_ref, wgt_ref, base_ref):
    tm = wgt_ref.shape[2]

    @pl.when(pl.program_id(1) == 0)
    def _():
        base_ref[...] = jnp.zeros_like(base_ref)

    logits = lax.dot_general(wr_ref[...], _load_token_rows(h_ref.at[0], tm), (((1,), (1,)), ((), ())),
                             preferred_element_type=F32, precision=HIGHEST)
    scores = jax.nn.sigmoid(logits)
    sel = scores + bias_ref[...]
    neg = -jnp.inf

    r8 = lax.broadcasted_iota(jnp.int32, (GROUP_SIZE, tm), 0)
    grp = jnp.zeros((N_GROUPS, tm), F32)
    for g in range(N_GROUPS):
        blk = sel[g * GROUP_SIZE:(g + 1) * GROUP_SIZE, :]
        m1 = jnp.max(blk, axis=0, keepdims=True)
        first = _first_index(blk == m1, r8, GROUP_SIZE)
        m2 = jnp.max(jnp.where(r8 == first, neg, blk), axis=0, keepdims=True)
        grp = jnp.where(r8 == g, m1 + m2, grp)
    gsel = jnp.zeros((N_GROUPS, tm), jnp.int32)
    for _ in range(TOPK_GROUPS):
        m = jnp.max(grp, axis=0, keepdims=True)
        hit = r8 == _first_index(grp == m, r8, N_GROUPS)
        gsel = jnp.where(hit, 1, gsel)
        grp = jnp.where(hit, neg, grp)
    cand = jnp.concatenate(
        [jnp.where(gsel[g:g + 1, :] > 0, sel[g * GROUP_SIZE:(g + 1) * GROUP_SIZE, :], neg) for g in range(N_GROUPS)],
        axis=0)

    re = lax.broadcasted_iota(jnp.int32, (N_EXPERTS, tm), 0)
    chosen = jnp.zeros((N_EXPERTS, tm), jnp.int32)
    for _ in range(TOP_K):
        m = jnp.max(cand, axis=0, keepdims=True)
        hit = re == _first_index(cand == m, re, N_EXPERTS)
        chosen = jnp.where(hit, 1, chosen)
        cand = jnp.where(hit, neg, cand)
    picked = jnp.where(chosen > 0, scores, 0.0)
    wgt = picked / jnp.sum(picked, axis=0, keepdims=True) * ROUTED_SCALE

    li = lax.broadcasted_iota(jnp.int32, (LANES, LANES), 0)
    lj = lax.broadcasted_iota(jnp.int32, (LANES, LANES), 1)
    strict = (li < lj).astype(BF16)
    chosen_b = chosen.astype(F32).astype(BF16)
    base = base_ref[...]
    pos = []
    for kb in range(tm // LANES):
        cblk = chosen_b[:, kb * LANES:(kb + 1) * LANES]
        pos.append(jnp.dot(cblk, strict, preferred_element_type=F32) + base)
        base = base + jnp.sum(cblk.astype(F32), axis=1, keepdims=True)
    base_ref[...] = base
    cnt_ref[0] = base.astype(jnp.int32)
    rank_ref[0] = jnp.where(chosen > 0, jnp.concatenate(pos, axis=1).astype(jnp.int32), -1)
    wgt_ref[0] = wgt


def _compact_kernel(cnt_ref, rank_ref, wgt_ref, idx_ref, wl_ref):
    rank = rank_ref[0]
    ne, tm = rank.shape
    lane = lax.broadcasted_iota(jnp.int32, (ne, tm), 1)
    live = jnp.where(rank >= 0, 1, 0)
    dist = lane - rank
    val = lane
    wv = wgt_ref[0]
    step = 1
    while step < tm:
        move = jnp.where(jnp.logical_and(live > 0, (dist & step) != 0), 1, 0)
        sh = tm - step
        take = pltpu.roll(move, sh, 1) > 0
        val = jnp.where(take, pltpu.roll(val, sh, 1), val)
        wv = jnp.where(take, pltpu.roll(wv, sh, 1), wv)
        dist = jnp.where(take, pltpu.roll(dist, sh, 1), dist)
        live = jnp.where(take, 1, jnp.where(move > 0, 0, live))
        step *= 2
    valid = lane < cnt_ref[0]
    spare = jnp.full((ne, MOE_BLOCK), tm, jnp.int32)
    idx_ref[0] = jnp.concatenate([jnp.where(valid, val, tm), spare], axis=1)
    wl_ref[0] = jnp.where(valid, wv, 0.0)


def _router(h2, s, w_router_t, bias_col):
    nb = h2.shape[0]
    dense = pl.BlockSpec((1, N_EXPERTS, ROW_TILE), lambda b, i: (b, 0, i))
    counts, rank, wgt = pl.pallas_call(
        _route_kernel,
        grid=(nb, s // ROW_TILE),
        in_specs=[pl.BlockSpec((1, ROW_TILE * NCHUNK, LANES), lambda b, i: (b, i, 0)),
                  _resident(w_router_t.shape), _resident(bias_col.shape)],
        out_specs=[pl.BlockSpec((1, N_EXPERTS, 1), lambda b, i: (b, 0, 0)), dense, dense],
        out_shape=[jax.ShapeDtypeStruct((nb, N_EXPERTS, 1), jnp.int32),
                   jax.ShapeDtypeStruct((nb, N_EXPERTS, s), jnp.int32),
                   jax.ShapeDtypeStruct((nb, N_EXPERTS, s), F32)],
        scratch_shapes=[pltpu.VMEM((N_EXPERTS, 1), F32)],
        compiler_params=_cparams(("arbitrary", "arbitrary")),
    )(h2, w_router_t, bias_col)
    rows = lambda width: pl.BlockSpec((1, SUBLANES, width), lambda b, g: (b, g, 0))
    idx, wl = pl.pallas_call(
        _compact_kernel,
        grid=(nb, N_EXPERTS // SUBLANES),
        in_specs=[rows(1), rows(s), rows(s)],
        out_specs=[rows(s + MOE_BLOCK), rows(s)],
        out_shape=[jax.ShapeDtypeStruct((nb, N_EXPERTS, s + MOE_BLOCK), jnp.int32),
                   jax.ShapeDtypeStruct((nb, N_EXPERTS, s), F32)],
        compiler_params=_cparams(("arbitrary", "arbitrary")),
    )(counts, rank, wgt)
    return counts, idx, wl


SCATTER_GROUP = 16


def _experts_kernel(cnt_ref, idx_ref, wl_ref, h_ref, wu_ref, wd_ref, o_ref, xa_ref, xb_ref, ya_ref, yb_ref):
    t = pl.program_id(0)
    e = pl.program_id(1)
    h_rows = h_ref.at[0]
    acc_ref = o_ref.at[0]

    @pl.when(e == 0)
    def _():
        o_ref[...] = jnp.zeros_like(o_ref)

    count = cnt_ref[t * N_EXPERTS + e]
    nblk = (count + MOE_BLOCK - 1) // MOE_BLOCK
    ri = lax.broadcasted_iota(jnp.int32, (MOE_BLOCK, MOE_BLOCK), 0)
    ci = lax.broadcasted_iota(jnp.int32, (MOE_BLOCK, MOE_BLOCK), 1)

    def gather(blk, x_ref):
        ids = idx_ref.at[0, 0, pl.ds(blk * MOE_BLOCK, MOE_BLOCK)]
        for r in range(MOE_BLOCK):
            x_ref[pl.ds(r * NCHUNK, NCHUNK), :] = h_rows[pl.ds(pl.multiple_of(ids[r] * NCHUNK, NCHUNK), NCHUNK), :]

    def ffn(blk, x_ref, y_ref):
        x = _load_token_rows(x_ref, MOE_BLOCK)
        gu = jnp.dot(x.astype(BF16), wu_ref[0], preferred_element_type=F32)
        act = _silu(gu[:, :EXPERT_F]) * gu[:, EXPERT_F:]
        y = jnp.dot(act.astype(BF16), wd_ref[0], preferred_element_type=F32)
        w_row = wl_ref[0, pl.ds(blk, 1), :]
        w_col = jnp.sum(jnp.where(ri == ci, w_row, 0.0), axis=1, keepdims=True)
        _store_token_rows(y_ref, y * w_col)

    def scatter(blk, y_ref):
        ids = idx_ref.at[0, 0, pl.ds(blk * MOE_BLOCK, MOE_BLOCK)]
        for g0 in range(0, MOE_BLOCK, SCATTER_GROUP):
            rows = range(g0, g0 + SCATTER_GROUP)
            dst = [pl.multiple_of(ids[r] * NCHUNK, NCHUNK) for r in rows]
            new = [acc_ref[pl.ds(d, NCHUNK), :] + y_ref[pl.ds(r * NCHUNK, NCHUNK), :] for d, r in zip(dst, rows)]
            for d, v in zip(dst, new):
                acc_ref[pl.ds(d, NCHUNK), :] = v

    @pl.when(nblk > 0)
    def _():
        gather(0, xa_ref)

    def pair(j, carry):
        b0 = 2 * j
        ffn(b0, xa_ref, ya_ref)
        gather(b0 + 1, xb_ref)
        ffn(b0 + 1, xb_ref, yb_ref)
        scatter(b0, ya_ref)
        gather(b0 + 2, xa_ref)
        scatter(b0 + 1, yb_ref)
        return carry

    lax.fori_loop(0, nblk // 2, pair, 0)

    @pl.when(nblk % 2 == 1)
    def _():
        ffn(nblk - 1, xa_ref, ya_ref)
        scatter(nblk - 1, ya_ref)


def _experts(h2, counts, idx, wl, w_up, w_down):
    nb, rows, _ = h2.shape
    nlist = idx.shape[-1]
    tile = pl.BlockSpec((1, rows, LANES), lambda t, e, c: (t, 0, 0), pipeline_mode=pl.Buffered(1))
    buf = pltpu.VMEM((MOE_BLOCK * NCHUNK, LANES), F32)
    grid_spec = pltpu.PrefetchScalarGridSpec(
        num_scalar_prefetch=1,
        grid=(nb, N_EXPERTS),
        in_specs=[pl.BlockSpec((1, 1, nlist), lambda t, e, c: (t * N_EXPERTS + e, 0, 0), memory_space=pltpu.SMEM),
                  pl.BlockSpec((1,) + wl.shape[1:], lambda t, e, c: (t * N_EXPERTS + e, 0, 0)),
                  tile,
                  pl.BlockSpec((1, D_MODEL, 2 * EXPERT_F), lambda t, e, c: (e, 0, 0)),
                  pl.BlockSpec((1, EXPERT_F, D_MODEL), lambda t, e, c: (e, 0, 0))],
        out_specs=tile,
        scratch_shapes=[buf, buf, buf, buf],
    )
    return pl.pallas_call(
        _experts_kernel,
        grid_spec=grid_spec,
        out_shape=jax.ShapeDtypeStruct(h2.shape, F32),
        compiler_params=_cparams(("arbitrary", "arbitrary")),
    )(counts, idx, wl, h2, w_up, w_down)


def _ffn_out_kernel(x1_ref, h2_ref, fr_ref, m_ref, wu_ref, wd_ref, g_ref, b_ref, o_ref, *, alpha):
    gu = jnp.dot(_load_token_rows(h2_ref.at[0], ROW_TILE).astype(BF16), wu_ref[...], preferred_element_type=F32)
    act = _silu(gu[:, :EXPERT_F]) * gu[:, EXPERT_F:]
    f = (jnp.dot(act.astype(BF16), wd_ref[...], preferred_element_type=F32)
         + _load_token_rows(fr_ref.at[0], ROW_TILE))
    o_ref[0] = _layer_norm(alpha * x1_ref[0] + m_ref[0, 5:6, :] * f, g_ref[...], b_ref[...])


def _ffn_out(x1, h2, fr, mods, ws_up, ws_down, ln_g, ln_b, alpha, latent_only):
    nb, s, _ = x1.shape
    nt = s // ROW_TILE
    skip = CTX_LEN // ROW_TILE if latent_only else 0
    row = pl.BlockSpec((1, ROW_TILE, D_MODEL), lambda b, i: (b, i + skip, 0))
    mod = pl.BlockSpec((1, 6, D_MODEL), lambda b, i: (jnp.where(i + skip == 0, nb, b), 0, 0))
    chunked = pl.BlockSpec((1, ROW_TILE * NCHUNK, LANES), lambda b, i: (b, i + skip, 0))
    return pl.pallas_call(
        functools.partial(_ffn_out_kernel, alpha=alpha),
        grid=(nb, nt - skip),
        in_specs=[row, chunked, chunked, mod, _resident(ws_up.shape), _resident(ws_down.shape),
                  _resident(ln_g.shape), _resident(ln_b.shape)],
        out_specs=pl.BlockSpec((1, ROW_TILE, D_MODEL), lambda b, i: (b, i, 0)),
        out_shape=jax.ShapeDtypeStruct((nb, s - skip * ROW_TILE, D_MODEL), F32),
        compiler_params=_cparams(("arbitrary", "arbitrary")),
    )(x1, h2, fr, mods, ws_up, ws_down, ln_g, ln_b)


def _rope_tables(n_lat):
    t = jnp.arange(n_lat)
    rowp = (t // GRID_W).astype(F32)
    colp = (t % GRID_W).astype(F32)
    n_freq = ATT_DH // 4
    inv = ROPE_BASE ** (-jnp.arange(n_freq, dtype=F32) / n_freq)
    ang = jnp.concatenate([rowp[:, None] * inv, colp[:, None] * inv], axis=-1)
    lane = jnp.arange(LANES)
    cos = jnp.cos(ang)[:, lane % (ATT_DH // 2)]
    sign = jnp.where((lane % ATT_DH) < ATT_DH // 2, -1.0, 1.0).astype(F32)
    sin = jnp.sin(ang)[:, lane % (ATT_DH // 2)] * sign
    cos = jnp.concatenate([jnp.ones((CTX_LEN, LANES), F32), cos], axis=0)
    sin = jnp.concatenate([jnp.zeros((CTX_LEN, LANES), F32), sin], axis=0)
    return cos, sin


def _block_diag(w):
    n, k, _ = w.shape
    eye = jnp.eye(n, dtype=w.dtype)
    return (eye[:, None, :, None] * w[:, :, None, :]).reshape(n * k, n * k)


def _inproj_weights(w):
    o_v = 2 * BRANCH_W
    o_lx = o_v + ATT_HEADS * ATT_DV
    o_z = o_lx + 2 * BRANCH_W
    o_xbc = o_z + BRANCH_W
    o_dt = o_xbc + 2 * BRANCH_W
    o_g = o_dt + 2 * SSD_HEADS
    wv = w[:, o_v:o_lx].T.reshape(ATT_HEADS, ATT_DV, D_MODEL)
    wv = jnp.pad(wv, ((0, 0), (0, VT_ROWS - ATT_DV), (0, 0))).reshape(ATT_HEADS * VT_ROWS, D_MODEL)
    ones = jnp.zeros((ATT_HEADS, VT_ROWS, 1), F32).at[:, ATT_DV, 0].set(1.0).reshape(ATT_HEADS * VT_ROWS, 1)
    wdt = w[:, o_dt:o_g]
    return {
        "qk": w[:, :o_v].astype(BF16),
        "vt": wv.astype(BF16),
        "ones": ones,
        "lxg": w[:, o_lx:o_z].astype(BF16),
        "z": w[:, o_z:o_xbc].astype(BF16),
        "xbc": w[:, o_xbc:o_dt].astype(BF16),
        "dt": jnp.pad(wdt, ((0, 0), (0, LANES - 2 * SSD_HEADS))).astype(BF16),
        "dtt": wdt.T.astype(BF16),
        "g": w[:, o_g:].astype(BF16),
    }


def _pad_row(v):
    return jnp.pad(v.reshape(1, -1), ((0, 0), (0, LANES - v.size)))


def kernel(x, c, ctx, c_ctx, w_mod, b_mod, w_in, lam_q, lam_k, attn_norm_g, lru_conv_w, lru_conv_b, lru_wa, lru_ba, lru_wi, lru_bi, lru_lambda, ssd_conv_w, ssd_conv_b, ssd_dt_bias, ssd_a_log, ssd_d, ssd_norm_g, w_branch, w_out, ln1_g, ln1_b, w_router, router_bias, w_up, w_down, ws_up, ws_down, ln2_g, ln2_b):
    nb, n_lat, _ = x.shape
    depth = w_mod.shape[0]
    assert ctx.shape[1] == CTX_LEN and n_lat % ROW_TILE == 0 and nb + 1 <= 16
    s = CTX_LEN + n_lat
    alpha = (2 * depth) ** 0.25

    xc = jnp.concatenate([ctx, x], axis=1)
    cc = jnp.zeros((16, D_MODEL), F32).at[:nb].set(c).at[nb].set(c_ctx)
    mods_all = _modulation(cc, w_mod, b_mod).reshape(depth, 16, 6, D_MODEL)
    cos_t, sin_t = _rope_tables(n_lat)
    head_of_channel = jnp.arange(BRANCH_W) // SSD_HEADDIM

    for l in range(depth):
        last = l == depth - 1
        lam_init = 0.8 - 0.6 * math.exp(-0.3 * l)
        mods = mods_all[l]
        qk, vt, lxg, z, xbc, dt, dtt, gates = _inproj(xc, mods, cos_t, sin_t, _inproj_weights(w_in[l]))

        ya = _attention(qk, vt, lam_q[l], lam_k[l], attn_norm_g[l], lam_init)

        yb = None
        for d in range(2):
            w_gate = jnp.concatenate([_block_diag(lru_wa[l, d]), _block_diag(lru_wi[l, d])], axis=1).astype(BF16)
            yb = _rglru(lxg, lru_conv_w[l], lru_conv_b[l].reshape(1, -1), w_gate,
                        lru_ba[l, d].reshape(1, -1), lru_bi[l, d].reshape(1, -1), lru_lambda[l, d].reshape(1, -1),
                        reverse=(d == 1), yf=yb)

        ssd_p = {
            "conv_w": ssd_conv_w[l], "conv_b": ssd_conv_b[l].reshape(1, -1),
            "dtb_row": _pad_row(ssd_dt_bias[l]), "dtb_col": ssd_dt_bias[l].reshape(-1, 1),
            "alog_row": _pad_row(ssd_a_log[l]), "alog_col": ssd_a_log[l].reshape(-1, 1),
            "skip": jnp.repeat(ssd_d[l], SSD_HEADDIM).reshape(1, -1), "norm_g": ssd_norm_g[l].reshape(1, -1),
        }
        yc = None
        for d in range(2):
            ssd_p["expand"] = (jnp.arange(LANES)[:, None] == d * SSD_HEADS + head_of_channel[None, :]).astype(BF16)
            yc = _ssd(xbc, dt, dtt, ssd_p, d, yf=yc, z=z)

        x1, h2 = _merge(ya, yb, yc, gates, xc, mods, w_branch[l].astype(BF16), w_out[l].astype(BF16),
                        ln1_g[l].reshape(1, -1), ln1_b[l].reshape(1, -1), alpha)

        counts, idx, wl = _router(h2, s, w_router[l].T, router_bias[l].reshape(-1, 1))
        fr = _experts(h2, counts.reshape(nb * N_EXPERTS), idx.reshape(nb * N_EXPERTS, 1, s + MOE_BLOCK),
                      wl.reshape(nb * N_EXPERTS, s // MOE_BLOCK, MOE_BLOCK),
                      w_up[l].astype(BF16), w_down[l].astype(BF16))
        xc = _ffn_out(x1, h2, fr, mods, ws_up[l].astype(BF16), ws_down[l].astype(BF16),
                      ln2_g[l].reshape(1, -1), ln2_b[l].reshape(1, -1), alpha, latent_only=last)
    return xc
```

```python
import functools
import math

import jax
import jax.numpy as jnp
from jax import lax
from jax.experimental import pallas as pl
from jax.experimental.pallas import tpu as pltpu

F32 = jnp.float32
BF16 = jnp.bfloat16
HIGHEST = lax.Precision.HIGHEST

D_MODEL = 1024
GRID_W = 64
CTX_LEN = 256
BRANCH_W = 512
ATT_HEADS = 4
ATT_DH = 64
ATT_DV = 128
ROPE_BASE = 10000.0
LRU_C = 8.0
SSD_HEADS = 8
SSD_HEADDIM = 64
SSD_HPG = 4
SSD_GROUPS = 2
SSD_STATE = 128
N_EXPERTS = 64
N_GROUPS = 8
GROUP_SIZE = N_EXPERTS // N_GROUPS
TOP_K = 8
TOPK_GROUPS = 4
EXPERT_F = 256
ROUTED_SCALE = 2.5
LN_EPS = 1e-5
RMS_EPS = 1e-6

ROW_TILE = 256
HALO = 16
VT_ROWS = 144
KEY_CHUNK = 256
MOE_BLOCK = 128
MOE_PAD = ROW_TILE
SUBLANES = 8
LANES = 128
VMEM_LIMIT = 56 * 1024 * 1024
LOG2E = 1.4426950408889634


def _cparams(sem):
    return pltpu.CompilerParams(dimension_semantics=sem, vmem_limit_bytes=VMEM_LIMIT)


def _resident(shape):
    nd = len(shape)
    return pl.BlockSpec(shape, lambda *_: (0,) * nd, pipeline_mode=pl.Buffered(1))


def _silu(x):
    return x * jax.nn.sigmoid(x)


def _softplus(x):
    return jnp.maximum(x, 0.0) + jnp.log1p(jnp.exp(-jnp.abs(x)))


def _gelu_tanh(x):
    return 0.5 * x * (1.0 + jnp.tanh(math.sqrt(2.0 / math.pi) * (x + 0.044715 * (x * x * x))))


NCHUNK = D_MODEL // LANES


def _load_token_rows(ref, n, first=0):
    return jnp.concatenate([ref[pl.ds(first * NCHUNK + s, n, stride=NCHUNK), :] for s in range(NCHUNK)], axis=1)


def _store_token_rows(ref, x):
    n = x.shape[0]
    for s in range(NCHUNK):
        ref[pl.ds(s, n, stride=NCHUNK), :] = x[:, s * LANES:(s + 1) * LANES]


def _layer_norm(x, g, b):
    mu = jnp.mean(x, axis=-1, keepdims=True)
    xc = x - mu
    var = jnp.mean(xc * xc, axis=-1, keepdims=True)
    return xc * lax.rsqrt(var + LN_EPS) * g + b


def _mod_kernel(c_ref, w_ref, b_ref, o_ref):
    s = _silu(c_ref[...])
    o_ref[0] = jnp.dot(s, w_ref[0], preferred_element_type=F32, precision=HIGHEST) + b_ref[0]


def _modulation(cc, w_mod, b_mod):
    depth = w_mod.shape[0]
    nblk = 6
    return pl.pallas_call(
        _mod_kernel,
        grid=(depth, nblk),
        in_specs=[
            pl.BlockSpec((16, D_MODEL), lambda l, j: (0, 0)),
            pl.BlockSpec((1, D_MODEL, D_MODEL), lambda l, j: (l, 0, j)),
            pl.BlockSpec((1, 1, D_MODEL), lambda l, j: (l, 0, j)),
        ],
        out_specs=pl.BlockSpec((1, 16, D_MODEL), lambda l, j: (l, 0, j)),
        out_shape=jax.ShapeDtypeStruct((depth, 16, nblk * D_MODEL), F32),
        compiler_params=_cparams(("arbitrary", "arbitrary")),
    )(cc, w_mod, b_mod.reshape(depth, 1, nblk * D_MODEL))


def _mod_spec(nb):
    return pl.BlockSpec((1, 6, D_MODEL), lambda b, i: (jnp.where(i == 0, nb, b), 0, 0))


def _inproj_kernel(x_ref, m_ref, cos_ref, sin_ref, wqk_ref, wvt_ref, ones_ref, wlxg_ref, wz_ref,
                   wxbc_ref, wdt_ref, wdtt_ref, wg_ref,
                   qk_ref, vt_ref, lxg_ref, z_ref, xbc_ref, dt_ref, dtt_ref, g_ref):
    x = x_ref[0]
    h = (x * (1.0 + m_ref[0, 1:2, :]) + m_ref[0, 0:1, :]).astype(BF16)
    nt = (((1,), (1,)), ((), ()))

    qk = jnp.dot(h, wqk_ref[...], preferred_element_type=F32)
    cos = cos_ref[...]
    sin = sin_ref[...]
    lane = lax.broadcasted_iota(jnp.int32, cos.shape, 1)
    first_half = (lane % ATT_DH) < (ATT_DH // 2)
    for j in range(2 * ATT_HEADS):
        blk = qk[:, j * LANES:(j + 1) * LANES]
        partner = jnp.where(first_half, pltpu.roll(blk, LANES - ATT_DH // 2, 1),
                            pltpu.roll(blk, ATT_DH // 2, 1))
        r = blk * cos + partner * sin
        if j < ATT_HEADS:
            r = r * (ATT_DH ** -0.5 * LOG2E)
        qk_ref[0, :, j * LANES:(j + 1) * LANES] = r.astype(BF16)

    vt = lax.dot_general(wvt_ref[...], h, nt, preferred_element_type=F32) + ones_ref[...]
    vt_ref[0] = vt.astype(BF16)
    lxg_ref[0] = jnp.dot(h, wlxg_ref[...], preferred_element_type=F32).astype(BF16)
    z_ref[0] = jnp.dot(h, wz_ref[...], preferred_element_type=F32).astype(BF16)
    xbc_ref[0] = jnp.dot(h, wxbc_ref[...], preferred_element_type=F32).astype(BF16)
    dt_ref[0] = jnp.dot(h, wdt_ref[...], preferred_element_type=F32)
    dtt_ref[0] = lax.dot_general(wdtt_ref[...], h, nt, preferred_element_type=F32)
    g_ref[0] = jnp.dot(h, wg_ref[...], preferred_element_type=F32).astype(BF16)


def _inproj(xc, mods, cos_t, sin_t, w):
    nb, s, _ = xc.shape
    nt = s // ROW_TILE
    row = lambda width: pl.BlockSpec((1, ROW_TILE, width), lambda b, i: (b, i, 0))
    col = lambda rows: pl.BlockSpec((1, rows, ROW_TILE), lambda b, i: (b, 0, i))
    vt_rows = ATT_HEADS * VT_ROWS
    outs = [
        (jax.ShapeDtypeStruct((nb, s, 2 * BRANCH_W), BF16), row(2 * BRANCH_W)),
        (jax.ShapeDtypeStruct((nb, vt_rows, s), BF16), col(vt_rows)),
        (jax.ShapeDtypeStruct((nb, s, 2 * BRANCH_W), BF16), row(2 * BRANCH_W)),
        (jax.ShapeDtypeStruct((nb, s, BRANCH_W), BF16), row(BRANCH_W)),
        (jax.ShapeDtypeStruct((nb, s, 2 * BRANCH_W), BF16), row(2 * BRANCH_W)),
        (jax.ShapeDtypeStruct((nb, s, LANES), F32), row(LANES)),
        (jax.ShapeDtypeStruct((nb, 2 * SSD_HEADS, s), F32), col(2 * SSD_HEADS)),
        (jax.ShapeDtypeStruct((nb, s, 3 * D_MODEL), BF16), row(3 * D_MODEL)),
    ]
    weights = [w["qk"], w["vt"], w["ones"], w["lxg"], w["z"], w["xbc"], w["dt"], w["dtt"], w["g"]]
    return pl.pallas_call(
        _inproj_kernel,
        grid=(nb, nt),
        in_specs=[row(D_MODEL), _mod_spec(nb),
                  pl.BlockSpec((ROW_TILE, LANES), lambda b, i: (i, 0)),
                  pl.BlockSpec((ROW_TILE, LANES), lambda b, i: (i, 0))]
                 + [_resident(a.shape) for a in weights],
        out_specs=[o[1] for o in outs],
        out_shape=[o[0] for o in outs],
        compiler_params=_cparams(("arbitrary", "arbitrary")),
    )(xc, mods, cos_t, sin_t, *weights)


def _attn_kernel(lq_ref, lk_ref, g_ref, q_ref, qn_ref, k_ref, vt_ref, o_ref, sa_ref, sb_ref, m_ref, *, lam_init):
    i = pl.program_id(2)
    tq = q_ref.shape[1]
    nck = k_ref.shape[1] // KEY_CHUNK
    prod = lq_ref[...] * lk_ref[...]
    d0 = jnp.sum(prod[0:1, :], axis=1, keepdims=True)
    d1 = jnp.sum(prod[1:2, :], axis=1, keepdims=True)
    lam = jnp.exp(d0) - jnp.exp(d1) + lam_init
    gcol = g_ref[0] * (1.0 - lam_init)

    def stack_maps(q):
        lane = lax.broadcasted_iota(jnp.int32, q.shape, 1)
        zero = jnp.zeros_like(q)
        return jnp.concatenate([jnp.where(lane < ATT_DH, q, zero), jnp.where(lane >= ATT_DH, q, zero)], axis=0)

    def scores(c, q2, s_ref, m):
        rows = slice(c * KEY_CHUNK, (c + 1) * KEY_CHUNK)
        st = lax.dot_general(k_ref[0, rows, :], q2, (((1,), (1,)), ((), ())),
                             preferred_element_type=F32)
        s_ref[rows, :] = st
        mc = jnp.max(st, axis=0, keepdims=True)
        return mc if m is None else jnp.maximum(m, mc)

    def weigh(c, s_ref, m, acc):
        rows = slice(c * KEY_CHUNK, (c + 1) * KEY_CHUNK)
        e = jnp.exp2((s_ref[rows, :] - m).astype(BF16))
        pv = jnp.dot(vt_ref[0, :, rows], e, preferred_element_type=F32)
        return pv if acc is None else acc + pv

    def finish(acc):
        r = 1.0 / acc[ATT_DV:ATT_DV + 1, :]
        o = acc[:ATT_DV, :tq] * r[:, :tq] - lam * (acc[:ATT_DV, tq:] * r[:, tq:])
        ms = jnp.mean(o * o, axis=0, keepdims=True)
        o_ref[0] = (o * lax.rsqrt(ms + RMS_EPS) * gcol).T.astype(BF16)

    @pl.when(i == 0)
    def _():
        q2 = stack_maps(q_ref[0])
        finish(weigh(0, sa_ref, scores(0, q2, sa_ref, None), None))
        q2n = stack_maps(qn_ref[0])
        mn = None
        for c in range(nck):
            mn = scores(c, q2n, sb_ref, mn)
        m_ref[1:2, :] = mn

    def step(cur_ref, nxt_ref, cur_slot, nxt_slot):
        q2n = stack_maps(qn_ref[0])
        m_cur = m_ref[cur_slot:cur_slot + 1, :]
        mn = acc = None
        for c in range(nck):
            mn = scores(c, q2n, nxt_ref, mn)
            acc = weigh(c, cur_ref, m_cur, acc)
        finish(acc)
        m_ref[nxt_slot:nxt_slot + 1, :] = mn

    @pl.when(i % 2 == 1)
    def _():
        step(sb_ref, sa_ref, 1, 0)

    @pl.when(jnp.logical_and(i % 2 == 0, i > 0))
    def _():
        step(sa_ref, sb_ref, 0, 1)


def _attention(qk, vt, lam_q, lam_k, attn_g, lam_init):
    nb, s, _ = qk.shape
    nq = s // ROW_TILE
    return pl.pallas_call(
        functools.partial(_attn_kernel, lam_init=lam_init),
        grid=(nb, ATT_HEADS, nq),
        in_specs=[
            pl.BlockSpec((2, ATT_DH), lambda b, h, i: (0, 0)),
            pl.BlockSpec((2, ATT_DH), lambda b, h, i: (0, 0)),
            pl.BlockSpec((1, ATT_DV, 1), lambda b, h, i: (h, 0, 0)),
            pl.BlockSpec((1, ROW_TILE, LANES), lambda b, h, i: (b, i, h)),
            pl.BlockSpec((1, ROW_TILE, LANES), lambda b, h, i: (b, jnp.minimum(i + 1, nq - 1), h)),
            pl.BlockSpec((1, s, LANES), lambda b, h, i: (b, 0, ATT_HEADS + h)),
            pl.BlockSpec((1, VT_ROWS, s), lambda b, h, i: (b, h, 0)),
        ],
        out_specs=pl.BlockSpec((1, ROW_TILE, LANES), lambda b, h, i: (b, i, h)),
        out_shape=jax.ShapeDtypeStruct((nb, s, BRANCH_W), BF16),
        scratch_shapes=[pltpu.VMEM((s, 2 * ROW_TILE), F32), pltpu.VMEM((s, 2 * ROW_TILE), F32),
                        pltpu.VMEM((SUBLANES, 2 * ROW_TILE), F32)],
        compiler_params=_cparams(("arbitrary", "arbitrary", "arbitrary")),
    )(lam_q, lam_k, attn_g.reshape(ATT_HEADS, ATT_DV, 1), qk, qk, qk, vt)


def _scan_chunk(j, nc, reverse):
    if not reverse:
        return j
    return jnp.where(j == 0, 0, nc - j)


def _conv4(x_ref, xp_ref, xn_ref, c, nc, w_ref, b_ref):
    x = x_ref[0].astype(F32)
    t = x.shape[0]
    row = lax.broadcasted_iota(jnp.int32, x.shape, 0)
    prev_ok = (c >= 2).astype(F32)
    next_ok = jnp.logical_and(c >= 1, c < nc - 1).astype(F32)
    prev = xp_ref[0].astype(F32)
    p1 = prev[HALO - 1:HALO, :] * prev_ok
    p2 = prev[HALO - 2:HALO - 1, :] * prev_ok
    n0 = xn_ref[0].astype(F32)[0:1, :] * next_ok
    xm1 = jnp.where(row == 0, p1, pltpu.roll(x, 1, 0))
    xm2 = jnp.where(row == 0, p2, jnp.where(row == 1, p1, pltpu.roll(x, 2, 0)))
    xp1 = jnp.where(row == t - 1, n0, pltpu.roll(x, t - 1, 0))
    return w_ref[0:1, :] * xm2 + w_ref[1:2, :] * xm1 + w_ref[2:3, :] * x + w_ref[3:4, :] * xp1 + b_ref[...]


def _tile_specs(width, nc, reverse, blk=0):
    per = ROW_TILE // HALO
    last = nc * per - 1
    ch = lambda j: _scan_chunk(j, nc, reverse)
    cur = pl.BlockSpec((1, ROW_TILE, width), lambda b, j: (b, ch(j), blk))
    prev = pl.BlockSpec((1, HALO, width), lambda b, j: (b, jnp.maximum(ch(j) * per - 1, 0), blk))
    nxt = pl.BlockSpec((1, HALO, width), lambda b, j: (b, jnp.minimum((ch(j) + 1) * per, last), blk))
    return cur, prev, nxt


def _rglru_kernel(*refs, reverse, nc):
    if reverse:
        (x_ref, xp_ref, xn_ref, cw_ref, cb_ref, wg_ref, ba_ref, bi_ref, lam_ref,
         yf_ref, lg_ref, o_ref, h_ref) = refs
    else:
        (x_ref, xp_ref, xn_ref, cw_ref, cb_ref, wg_ref, ba_ref, bi_ref, lam_ref, o_ref, h_ref) = refs
    j = pl.program_id(1)
    c = _scan_chunk(j, nc, reverse)

    @pl.when(j == 0)
    def _():
        h_ref[...] = jnp.zeros_like(h_ref)

    u = _conv4(x_ref, xp_ref, xn_ref, c, nc, cw_ref, cb_ref)
    t = u.shape[0]
    pre = jnp.dot(u.astype(BF16), wg_ref[...], preferred_element_type=F32)
    r = jax.nn.sigmoid(pre[:, :BRANCH_W] + ba_ref[...])
    gi = jax.nn.sigmoid(pre[:, BRANCH_W:] + bi_ref[...])
    log_a = (-LRU_C) * r * _softplus(-lam_ref[...])
    a = jnp.exp(log_a)
    var = 1.0 - jnp.exp(2.0 * log_a)
    bv = jnp.where(var > 0.0, var * lax.rsqrt(var), 0.0) * (gi * u)

    in_group = lax.broadcasted_iota(jnp.int32, u.shape, 0) % SUBLANES
    d = 1
    while d < SUBLANES:
        if reverse:
            keep = in_group < SUBLANES - d
            sh = t - d
        else:
            keep = in_group >= d
            sh = d
        a_s = jnp.where(keep, pltpu.roll(a, sh, 0), 1.0)
        b_s = jnp.where(keep, pltpu.roll(bv, sh, 0), 0.0)
        bv = a * b_s + bv
        a = a * a_s
        d *= 2
    carry = jnp.broadcast_to(h_ref[...], (SUBLANES, BRANCH_W))
    ngroup = t // SUBLANES
    pieces = [None] * ngroup
    for gidx in (range(ngroup - 1, -1, -1) if reverse else range(ngroup)):
        rows = slice(gidx * SUBLANES, (gidx + 1) * SUBLANES)
        pieces[gidx] = a[rows, :] * carry + bv[rows, :]
        edge = pieces[gidx][0:1, :] if reverse else pieces[gidx][SUBLANES - 1:SUBLANES, :]
        carry = jnp.broadcast_to(edge, (SUBLANES, BRANCH_W))
    h_ref[...] = carry[0:1, :]
    hs = jnp.concatenate(pieces, axis=0)
    if reverse:
        o_ref[0] = ((yf_ref[0] + hs) * _gelu_tanh(lg_ref[0].astype(F32))).astype(BF16)
    else:
        o_ref[0] = hs


def _rglru(lxg, conv_w, conv_b, w_gate, b_a, b_i, lam, reverse, yf=None):
    nb, s, _ = lxg.shape
    nc = s // ROW_TILE
    cur, prev, nxt = _tile_specs(BRANCH_W, nc, reverse)
    params = [conv_w, conv_b, w_gate, b_a, b_i, lam]
    in_specs = [cur, prev, nxt] + [_resident(p.shape) for p in params]
    args = [lxg, lxg, lxg] + params
    if reverse:
        in_specs += [cur, _tile_specs(BRANCH_W, nc, reverse, blk=1)[0]]
        args += [yf, lxg]
    return pl.pallas_call(
        functools.partial(_rglru_kernel, reverse=reverse, nc=nc),
        grid=(nb, nc),
        in_specs=in_specs,
        out_specs=cur,
        out_shape=jax.ShapeDtypeStruct((nb, s, BRANCH_W), BF16 if reverse else F32),
        scratch_shapes=[pltpu.VMEM((1, BRANCH_W), F32)],
        compiler_params=_cparams(("arbitrary", "arbitrary")),
    )(*args)


def _ssd_kernel(*refs, reverse, nc, direction):
    if reverse:
        (x_ref, xp_ref, xn_ref, dt_ref, dtt_ref, cw_ref, cb_ref, dtb_ref, dtbc_ref, alog_ref, alogc_ref,
         exp_ref, yf_ref, z_ref, skip_ref, g_ref, o_ref, st_ref) = refs
    else:
        (x_ref, xp_ref, xn_ref, dt_ref, dtt_ref, cw_ref, cb_ref, dtb_ref, dtbc_ref, alog_ref, alogc_ref,
         exp_ref, o_ref, st_ref) = refs
    j = pl.program_id(1)
    c = _scan_chunk(j, nc, reverse)

    @pl.when(j == 0)
    def _():
        st_ref[...] = jnp.zeros_like(st_ref)

    u = _silu(_conv4(x_ref, xp_ref, xn_ref, c, nc, cw_ref, cb_ref))
    t = u.shape[0]
    xs = u[:, :BRANCH_W]
    gw = SSD_HPG * SSD_HEADDIM

    dt_c = _softplus(dt_ref[0] + dtb_ref[...])
    a_c = dt_c * (-jnp.exp(alog_ref[...]))
    a_r = _softplus(dtt_ref[0] + dtbc_ref[...]) * (-jnp.exp(alogc_ref[...]))
    ri = lax.broadcasted_iota(jnp.int32, (t, t), 0)
    ci = lax.broadcasted_iota(jnp.int32, (t, t), 1)
    lower = (ci <= ri).astype(F32)
    upper = (ci >= ri).astype(F32)
    cs_c = jnp.dot(upper if reverse else lower, a_c, preferred_element_type=F32, precision=HIGHEST)
    cs_r = jnp.dot(a_r, lower if reverse else upper, preferred_element_type=F32, precision=HIGHEST)
    tot = cs_c[0:1, :] if reverse else cs_c[t - 1:t, :]
    keep = (ci >= ri) if reverse else (ci <= ri)

    expand = exp_ref[...]
    xdt = xs * jnp.dot(dt_c.astype(BF16), expand, preferred_element_type=F32)
    e_in = jnp.dot(jnp.exp(cs_c).astype(BF16), expand, preferred_element_type=F32)
    e_out = jnp.dot(jnp.exp(tot - cs_c).astype(BF16), expand, preferred_element_type=F32)
    e_tot = jnp.dot(jnp.broadcast_to(jnp.exp(tot), (SUBLANES, LANES)), expand.astype(F32),
                    preferred_element_type=F32, precision=HIGHEST)[0:1, :]
    xdt_b = xdt.astype(BF16)
    xdec_b = (xdt * e_out).astype(BF16)
    lane_head = lax.broadcasted_iota(jnp.int32, (t, gw), 1) // SSD_HEADDIM
    zero_b = jnp.zeros((t, gw), BF16)

    ys = []
    for g in range(SSD_GROUPS):
        bm = u[:, BRANCH_W + g * SSD_STATE:BRANCH_W + (g + 1) * SSD_STATE].astype(BF16)
        cm = u[:, BRANCH_W + (SSD_GROUPS + g) * SSD_STATE:BRANCH_W + (SSD_GROUPS + g + 1) * SSD_STATE].astype(BF16)
        cb = lax.dot_general(cm, bm, (((1,), (1,)), ((), ())), preferred_element_type=F32)
        st_g = st_ref[:, g * gw:(g + 1) * gw]
        y = jnp.dot(cm, st_g.astype(BF16), preferred_element_type=F32) * e_in[:, g * gw:(g + 1) * gw]
        xg = xdt_b[:, g * gw:(g + 1) * gw]
        for rr in range(SSD_HPG):
            col = direction * SSD_HEADS + g * SSD_HPG + rr
            decay = jnp.where(keep, jnp.exp(cs_c[:, col:col + 1] - cs_r[col:col + 1, :]), 0.0)
            gm = (cb * decay).astype(BF16)
            y = y + jnp.dot(gm, jnp.where(lane_head == rr, xg, zero_b), preferred_element_type=F32)
        ys.append(y)
        upd = lax.dot_general(bm, xdec_b[:, g * gw:(g + 1) * gw], (((0,), (0,)), ((), ())),
                              preferred_element_type=F32)
        st_ref[:, g * gw:(g + 1) * gw] = st_g * e_tot[:, g * gw:(g + 1) * gw] + upd
    y = jnp.concatenate(ys, axis=1)

    if reverse:
        y = yf_ref[0] + y + skip_ref[...] * xs
        tz = y * _silu(z_ref[0].astype(F32))
        outs = []
        for g in range(SSD_GROUPS):
            tg = tz[:, g * gw:(g + 1) * gw]
            outs.append(tg * lax.rsqrt(jnp.mean(tg * tg, axis=1, keepdims=True) + RMS_EPS))
        o_ref[0] = (jnp.concatenate(outs, axis=1) * g_ref[...]).astype(BF16)
    else:
        o_ref[0] = y


def _ssd(xbc, dt, dtt, p, direction, yf=None, z=None):
    reverse = direction == 1
    nb, s, _ = xbc.shape
    nc = s // ROW_TILE
    cur, prev, nxt = _tile_specs(2 * BRANCH_W, nc, reverse)
    ch = lambda j: _scan_chunk(j, nc, reverse)
    params = [p["conv_w"], p["conv_b"], p["dtb_row"], p["dtb_col"], p["alog_row"], p["alog_col"], p["expand"]]
    in_specs = [cur, prev, nxt,
                pl.BlockSpec((1, ROW_TILE, LANES), lambda b, j: (b, ch(j), 0)),
                pl.BlockSpec((1, 2 * SSD_HEADS, ROW_TILE), lambda b, j: (b, 0, ch(j)))]
    in_specs += [_resident(a.shape) for a in params]
    args = [xbc, xbc, xbc, dt, dtt] + params
    half = pl.BlockSpec((1, ROW_TILE, BRANCH_W), lambda b, j: (b, ch(j), 0))
    if reverse:
        in_specs += [half, half, _resident(p["skip"].shape), _resident(p["norm_g"].shape)]
        args += [yf, z, p["skip"], p["norm_g"]]
    return pl.pallas_call(
        functools.partial(_ssd_kernel, reverse=reverse, nc=nc, direction=direction),
        grid=(nb, nc),
        in_specs=in_specs,
        out_specs=half,
        out_shape=jax.ShapeDtypeStruct((nb, s, BRANCH_W), BF16 if reverse else F32),
        scratch_shapes=[pltpu.VMEM((SSD_STATE, BRANCH_W), F32)],
        compiler_params=_cparams(("arbitrary", "arbitrary")),
    )(*args)


def _merge_kernel(ya_ref, yb_ref, yc_ref, gt_ref, x_ref, m_ref, wb_ref, wo_ref, g_ref, b_ref,
                  x1_ref, h2_ref, *, alpha):
    acc = None
    for n, y_ref in enumerate((ya_ref, yb_ref, yc_ref)):
        proj = jnp.dot(y_ref[0], wb_ref[n], preferred_element_type=F32)
        gate = jax.nn.sigmoid(gt_ref[0, :, n * D_MODEL:(n + 1) * D_MODEL].astype(F32))
        acc = gate * proj if acc is None else acc + gate * proj
    mix = jnp.dot(acc.astype(BF16), wo_ref[...], preferred_element_type=F32)
    x1 = _layer_norm(alpha * x_ref[0] + m_ref[0, 2:3, :] * mix, g_ref[...], b_ref[...])
    x1_ref[0] = x1
    _store_token_rows(h2_ref.at[0], x1 * (1.0 + m_ref[0, 4:5, :]) + m_ref[0, 3:4, :])


def _zero_tile_kernel(h_ref, o_ref):
    del h_ref
    o_ref[...] = jnp.zeros_like(o_ref)


def _merge(ya, yb, yc, gates, xc, mods, w_branch, w_out, ln_g, ln_b, alpha):
    nb, s, _ = xc.shape
    nt = s // ROW_TILE
    row = lambda width: pl.BlockSpec((1, ROW_TILE, width), lambda b, i: (b, i, 0))
    h2_shape = jax.ShapeDtypeStruct((nb, (s + MOE_PAD) * NCHUNK, LANES), F32)
    x1, h2 = pl.pallas_call(
        functools.partial(_merge_kernel, alpha=alpha),
        grid=(nb, nt),
        in_specs=[row(BRANCH_W), row(BRANCH_W), row(BRANCH_W), row(3 * D_MODEL), row(D_MODEL), _mod_spec(nb),
                  _resident(w_branch.shape), _resident(w_out.shape), _resident(ln_g.shape), _resident(ln_b.shape)],
        out_specs=[row(D_MODEL), pl.BlockSpec((1, ROW_TILE * NCHUNK, LANES), lambda b, i: (b, i, 0))],
        out_shape=[jax.ShapeDtypeStruct((nb, s, D_MODEL), F32), h2_shape],
        compiler_params=_cparams(("arbitrary", "arbitrary")),
    )(ya, yb, yc, gates, xc, mods, w_branch, w_out, ln_g, ln_b)
    h2 = pl.pallas_call(
        _zero_tile_kernel,
        grid=(nb,),
        in_specs=[pl.BlockSpec(memory_space=pl.ANY)],
        out_specs=pl.BlockSpec((1, NCHUNK, LANES), lambda b: (b, s, 0)),
        out_shape=h2_shape,
        input_output_aliases={0: 0},
        compiler_params=_cparams(("arbitrary",)),
    )(h2)
    return x1, h2


def _first_index(hit, ridx, n):
    return jnp.min(jnp.where(hit, ridx, n), axis=0, keepdims=True)


def _route_kernel(h_ref, wr_ref, bias_ref, cnt_ref, rank_ref, wgt_ref, base_ref):
    tm = wgt_ref.shape[2]

    @pl.when(pl.program_id(1) == 0)
    def _():
        base_ref[...] = jnp.zeros_like(base_ref)

    logits = lax.dot_general(wr_ref[...], _load_token_rows(h_ref.at[0], tm), (((1,), (1,)), ((), ())),
                             preferred_element_type=F32, precision=HIGHEST)
    scores = jax.nn.sigmoid(logits)
    sel = scores + bias_ref[...]
    neg = -jnp.inf

    r8 = lax.broadcasted_iota(jnp.int32, (GROUP_SIZE, tm), 0)
    grp = jnp.zeros((N_GROUPS, tm), F32)
    for g in range(N_GROUPS):
        blk = sel[g * GROUP_SIZE:(g + 1) * GROUP_SIZE, :]
        m1 = jnp.max(blk, axis=0, keepdims=True)
        first = _first_index(blk == m1, r8, GROUP_SIZE)
        m2 = jnp.max(jnp.where(r8 == first, neg, blk), axis=0, keepdims=True)
        grp = jnp.where(r8 == g, m1 + m2, grp)
    gsel = jnp.zeros((N_GROUPS, tm), jnp.int32)
    for _ in range(TOPK_GROUPS):
        m = jnp.max(grp, axis=0, keepdims=True)
        hit = r8 == _first_index(grp == m, r8, N_GROUPS)
        gsel = jnp.where(hit, 1, gsel)
        grp = jnp.where(hit, neg, grp)
    cand = jnp.concatenate(
        [jnp.where(gsel[g:g + 1, :] > 0, sel[g * GROUP_SIZE:(g + 1) * GROUP_SIZE, :], neg) for g in range(N_GROUPS)],
        axis=0)

    re = lax.broadcasted_iota(jnp.int32, (N_EXPERTS, tm), 0)
    chosen = jnp.zeros((N_EXPERTS, tm), jnp.int32)
    for _ in range(TOP_K):
        m = jnp.max(cand, axis=0, keepdims=True)
        hit = re == _first_index(cand == m, re, N_EXPERTS)
        chosen = jnp.where(hit, 1, chosen)
        cand = jnp.where(hit, neg, cand)
    picked = jnp.where(chosen > 0, scores, 0.0)
    wgt = picked / jnp.sum(picked, axis=0, keepdims=True) * ROUTED_SCALE

    li = lax.broadcasted_iota(jnp.int32, (LANES, LANES), 0)
    lj = lax.broadcasted_iota(jnp.int32, (LANES, LANES), 1)
    strict = (li < lj).astype(BF16)
    chosen_b = chosen.astype(F32).astype(BF16)
    base = base_ref[...]
    pos = []
    for kb in range(tm // LANES):
        cblk = chosen_b[:, kb * LANES:(kb + 1) * LANES]
        pos.append(jnp.dot(cblk, strict, preferred_element_type=F32) + base)
        base = base + jnp.sum(cblk.astype(F32), axis=1, keepdims=True)
    base_ref[...] = base
    cnt_ref[0] = base.astype(jnp.int32)
    rank_ref[0] = jnp.where(chosen > 0, jnp.concatenate(pos, axis=1).astype(jnp.int32), -1)
    wgt_ref[0] = wgt


def _compact_kernel(cnt_ref, rank_ref, wgt_ref, idx_ref, wl_ref):
    rank = rank_ref[0]
    ne, tm = rank.shape
    lane = lax.broadcasted_iota(jnp.int32, (ne, tm), 1)
    live = jnp.where(rank >= 0, 1, 0)
    dist = lane - rank
    val = lane
    wv = wgt_ref[0]
    step = 1
    while step < tm:
        move = jnp.where(jnp.logical_and(live > 0, (dist & step) != 0), 1, 0)
        sh = tm - step
        take = pltpu.roll(move, sh, 1) > 0
        val = jnp.where(take, pltpu.roll(val, sh, 1), val)
        wv = jnp.where(take, pltpu.roll(wv, sh, 1), wv)
        dist = jnp.where(take, pltpu.roll(dist, sh, 1), dist)
        live = jnp.where(take, 1, jnp.where(move > 0, 0, live))
        step *= 2
    valid = lane < cnt_ref[0]
    spare = jnp.full((ne, MOE_BLOCK), tm * NCHUNK, jnp.int32)
    idx_ref[0] = jnp.concatenate([jnp.where(valid, val * NCHUNK, tm * NCHUNK), spare], axis=1)
    wl_ref[0] = jnp.where(valid, wv, 0.0)


def _router(h2, s, w_router_t, bias_col):
    nb = h2.shape[0]
    dense = pl.BlockSpec((1, N_EXPERTS, ROW_TILE), lambda b, i: (b, 0, i))
    counts, rank, wgt = pl.pallas_call(
        _route_kernel,
        grid=(nb, s // ROW_TILE),
        in_specs=[pl.BlockSpec((1, ROW_TILE * NCHUNK, LANES), lambda b, i: (b, i, 0)),
                  _resident(w_router_t.shape), _resident(bias_col.shape)],
        out_specs=[pl.BlockSpec((1, N_EXPERTS, 1), lambda b, i: (b, 0, 0)), dense, dense],
        out_shape=[jax.ShapeDtypeStruct((nb, N_EXPERTS, 1), jnp.int32),
                   jax.ShapeDtypeStruct((nb, N_EXPERTS, s), jnp.int32),
                   jax.ShapeDtypeStruct((nb, N_EXPERTS, s), F32)],
        scratch_shapes=[pltpu.VMEM((N_EXPERTS, 1), F32)],
        compiler_params=_cparams(("arbitrary", "arbitrary")),
    )(h2, w_router_t, bias_col)
    rows = lambda width: pl.BlockSpec((1, SUBLANES, width), lambda b, g: (b, g, 0))
    idx, wl = pl.pallas_call(
        _compact_kernel,
        grid=(nb, N_EXPERTS // SUBLANES),
        in_specs=[rows(1), rows(s), rows(s)],
        out_specs=[rows(s + MOE_BLOCK), rows(s)],
        out_shape=[jax.ShapeDtypeStruct((nb, N_EXPERTS, s + MOE_BLOCK), jnp.int32),
                   jax.ShapeDtypeStruct((nb, N_EXPERTS, s), F32)],
        compiler_params=_cparams(("arbitrary", "arbitrary")),
    )(counts, rank, wgt)
    return counts, idx, wl


SCATTER_GROUP = 16


def _experts_kernel(cnt_ref, idx_ref, wl_ref, h_ref, wu_ref, wd_ref, o_ref, xa_ref, xb_ref, ya_ref, yb_ref):
    t = pl.program_id(0)
    e = pl.program_id(1)
    h_rows = h_ref.at[0]
    acc_ref = o_ref.at[0]

    @pl.when(e == 0)
    def _():
        o_ref[...] = jnp.zeros_like(o_ref)

    count = cnt_ref[t * N_EXPERTS + e]
    nblk = (count + MOE_BLOCK - 1) // MOE_BLOCK
    ri = lax.broadcasted_iota(jnp.int32, (MOE_BLOCK, MOE_BLOCK), 0)
    ci = lax.broadcasted_iota(jnp.int32, (MOE_BLOCK, MOE_BLOCK), 1)

    def gather(blk, x_ref):
        ids = idx_ref.at[0, 0, pl.ds(blk * MOE_BLOCK, MOE_BLOCK)]
        for r in range(MOE_BLOCK):
            x_ref[pl.ds(r * NCHUNK, NCHUNK), :] = h_rows[pl.ds(pl.multiple_of(ids[r], NCHUNK), NCHUNK), :]

    def ffn(blk, x_ref, y_ref):
        x = _load_token_rows(x_ref, MOE_BLOCK)
        gu = jnp.dot(x.astype(BF16), wu_ref[0], preferred_element_type=F32)
        act = _silu(gu[:, :EXPERT_F]) * gu[:, EXPERT_F:]
        y = jnp.dot(act.astype(BF16), wd_ref[0], preferred_element_type=F32)
        w_row = wl_ref[0, pl.ds(blk, 1), :]
        w_col = jnp.sum(jnp.where(ri == ci, w_row, 0.0), axis=1, keepdims=True)
        _store_token_rows(y_ref, y * w_col)

    def scatter(blk, y_ref):
        ids = idx_ref.at[0, 0, pl.ds(blk * MOE_BLOCK, MOE_BLOCK)]
        for g0 in range(0, MOE_BLOCK, SCATTER_GROUP):
            rows = range(g0, g0 + SCATTER_GROUP)
            dst = [pl.multiple_of(ids[r], NCHUNK) for r in rows]
            new = [acc_ref[pl.ds(d, NCHUNK), :] + y_ref[pl.ds(r * NCHUNK, NCHUNK), :] for d, r in zip(dst, rows)]
            for d, v in zip(dst, new):
                acc_ref[pl.ds(d, NCHUNK), :] = v

    @pl.when(nblk > 0)
    def _():
        gather(0, xa_ref)

    def pair(j, carry):
        b0 = 2 * j
        ffn(b0, xa_ref, ya_ref)
        gather(b0 + 1, xb_ref)
        ffn(b0 + 1, xb_ref, yb_ref)
        scatter(b0, ya_ref)
        gather(b0 + 2, xa_ref)
        scatter(b0 + 1, yb_ref)
        return carry

    lax.fori_loop(0, nblk // 2, pair, 0)

    @pl.when(nblk % 2 == 1)
    def _():
        ffn(nblk - 1, xa_ref, ya_ref)
        scatter(nblk - 1, ya_ref)


def _experts(h2, counts, idx, wl, w_up, w_down):
    nb, rows, _ = h2.shape
    nlist = idx.shape[-1]
    tile = pl.BlockSpec((1, rows, LANES), lambda t, e, c: (t, 0, 0), pipeline_mode=pl.Buffered(1))
    buf = pltpu.VMEM((MOE_BLOCK * NCHUNK, LANES), F32)
    grid_spec = pltpu.PrefetchScalarGridSpec(
        num_scalar_prefetch=1,
        grid=(nb, N_EXPERTS),
        in_specs=[pl.BlockSpec((1, 1, nlist), lambda t, e, c: (t * N_EXPERTS + e, 0, 0), memory_space=pltpu.SMEM),
                  pl.BlockSpec((1,) + wl.shape[1:], lambda t, e, c: (t * N_EXPERTS + e, 0, 0)),
                  tile,
                  pl.BlockSpec((1, D_MODEL, 2 * EXPERT_F), lambda t, e, c: (e, 0, 0)),
                  pl.BlockSpec((1, EXPERT_F, D_MODEL), lambda t, e, c: (e, 0, 0))],
        out_specs=tile,
        scratch_shapes=[buf, buf, buf, buf],
    )
    return pl.pallas_call(
        _experts_kernel,
        grid_spec=grid_spec,
        out_shape=jax.ShapeDtypeStruct(h2.shape, F32),
        compiler_params=_cparams(("arbitrary", "arbitrary")),
    )(counts, idx, wl, h2, w_up, w_down)


def _ffn_out_kernel(x1_ref, h2_ref, fr_ref, m_ref, wu_ref, wd_ref, g_ref, b_ref, o_ref, *, alpha):
    gu = jnp.dot(_load_token_rows(h2_ref.at[0], ROW_TILE).astype(BF16), wu_ref[...], preferred_element_type=F32)
    act = _silu(gu[:, :EXPERT_F]) * gu[:, EXPERT_F:]
    f = (jnp.dot(act.astype(BF16), wd_ref[...], preferred_element_type=F32)
         + _load_token_rows(fr_ref.at[0], ROW_TILE))
    o_ref[0] = _layer_norm(alpha * x1_ref[0] + m_ref[0, 5:6, :] * f, g_ref[...], b_ref[...])


def _ffn_out(x1, h2, fr, mods, ws_up, ws_down, ln_g, ln_b, alpha, latent_only):
    nb, s, _ = x1.shape
    nt = s // ROW_TILE
    skip = CTX_LEN // ROW_TILE if latent_only else 0
    row = pl.BlockSpec((1, ROW_TILE, D_MODEL), lambda b, i: (b, i + skip, 0))
    mod = pl.BlockSpec((1, 6, D_MODEL), lambda b, i: (jnp.where(i + skip == 0, nb, b), 0, 0))
    chunked = pl.BlockSpec((1, ROW_TILE * NCHUNK, LANES), lambda b, i: (b, i + skip, 0))
    return pl.pallas_call(
        functools.partial(_ffn_out_kernel, alpha=alpha),
        grid=(nb, nt - skip),
        in_specs=[row, chunked, chunked, mod, _resident(ws_up.shape), _resident(ws_down.shape),
                  _resident(ln_g.shape), _resident(ln_b.shape)],
        out_specs=pl.BlockSpec((1, ROW_TILE, D_MODEL), lambda b, i: (b, i, 0)),
        out_shape=jax.ShapeDtypeStruct((nb, s - skip * ROW_TILE, D_MODEL), F32),
        compiler_params=_cparams(("arbitrary", "arbitrary")),
    )(x1, h2, fr, mods, ws_up, ws_down, ln_g, ln_b)


def _rope_tables(n_lat):
    t = jnp.arange(n_lat)
    rowp = (t // GRID_W).astype(F32)
    colp = (t % GRID_W).astype(F32)
    n_freq = ATT_DH // 4
    inv = ROPE_BASE ** (-jnp.arange(n_freq, dtype=F32) / n_freq)
    ang = jnp.concatenate([rowp[:, None] * inv, colp[:, None] * inv], axis=-1)
    lane = jnp.arange(LANES)
    cos = jnp.cos(ang)[:, lane % (ATT_DH // 2)]
    sign = jnp.where((lane % ATT_DH) < ATT_DH // 2, -1.0, 1.0).astype(F32)
    sin = jnp.sin(ang)[:, lane % (ATT_DH // 2)] * sign
    cos = jnp.concatenate([jnp.ones((CTX_LEN, LANES), F32), cos], axis=0)
    sin = jnp.concatenate([jnp.zeros((CTX_LEN, LANES), F32), sin], axis=0)
    return cos, sin


def _block_diag(w):
    n, k, _ = w.shape
    eye = jnp.eye(n, dtype=w.dtype)
    return (eye[:, None, :, None] * w[:, :, None, :]).reshape(n * k, n * k)


def _inproj_weights(w):
    o_v = 2 * BRANCH_W
    o_lx = o_v + ATT_HEADS * ATT_DV
    o_z = o_lx + 2 * BRANCH_W
    o_xbc = o_z + BRANCH_W
    o_dt = o_xbc + 2 * BRANCH_W
    o_g = o_dt + 2 * SSD_HEADS
    wv = w[:, o_v:o_lx].T.reshape(ATT_HEADS, ATT_DV, D_MODEL)
    wv = jnp.pad(wv, ((0, 0), (0, VT_ROWS - ATT_DV), (0, 0))).reshape(ATT_HEADS * VT_ROWS, D_MODEL)
    ones = jnp.zeros((ATT_HEADS, VT_ROWS, 1), F32).at[:, ATT_DV, 0].set(1.0).reshape(ATT_HEADS * VT_ROWS, 1)
    wdt = w[:, o_dt:o_g]
    return {
        "qk": w[:, :o_v].astype(BF16),
        "vt": wv.astype(BF16),
        "ones": ones,
        "lxg": w[:, o_lx:o_z].astype(BF16),
        "z": w[:, o_z:o_xbc].astype(BF16),
        "xbc": w[:, o_xbc:o_dt].astype(BF16),
        "dt": jnp.pad(wdt, ((0, 0), (0, LANES - 2 * SSD_HEADS))).astype(BF16),
        "dtt": wdt.T.astype(BF16),
        "g": w[:, o_g:].astype(BF16),
    }


def _pad_row(v):
    return jnp.pad(v.reshape(1, -1), ((0, 0), (0, LANES - v.size)))


def kernel(x, c, ctx, c_ctx, w_mod, b_mod, w_in, lam_q, lam_k, attn_norm_g, lru_conv_w, lru_conv_b, lru_wa, lru_ba, lru_wi, lru_bi, lru_lambda, ssd_conv_w, ssd_conv_b, ssd_dt_bias, ssd_a_log, ssd_d, ssd_norm_g, w_branch, w_out, ln1_g, ln1_b, w_router, router_bias, w_up, w_down, ws_up, ws_down, ln2_g, ln2_b):
    nb, n_lat, _ = x.shape
    depth = w_mod.shape[0]
    assert ctx.shape[1] == CTX_LEN and n_lat % ROW_TILE == 0 and nb + 1 <= 16
    s = CTX_LEN + n_lat
    alpha = (2 * depth) ** 0.25

    xc = jnp.concatenate([ctx, x], axis=1)
    cc = jnp.zeros((16, D_MODEL), F32).at[:nb].set(c).at[nb].set(c_ctx)
    mods_all = _modulation(cc, w_mod, b_mod).reshape(depth, 16, 6, D_MODEL)
    cos_t, sin_t = _rope_tables(n_lat)
    head_of_channel = jnp.arange(BRANCH_W) // SSD_HEADDIM

    for l in range(depth):
        last = l == depth - 1
        lam_init = 0.8 - 0.6 * math.exp(-0.3 * l)
        mods = mods_all[l]
        qk, vt, lxg, z, xbc, dt, dtt, gates = _inproj(xc, mods, cos_t, sin_t, _inproj_weights(w_in[l]))

        ya = _attention(qk, vt, lam_q[l], lam_k[l], attn_norm_g[l], lam_init)

        yb = None
        for d in range(2):
            w_gate = jnp.concatenate([_block_diag(lru_wa[l, d]), _block_diag(lru_wi[l, d])], axis=1).astype(BF16)
            yb = _rglru(lxg, lru_conv_w[l], lru_conv_b[l].reshape(1, -1), w_gate,
                        lru_ba[l, d].reshape(1, -1), lru_bi[l, d].reshape(1, -1), lru_lambda[l, d].reshape(1, -1),
                        reverse=(d == 1), yf=yb)

        ssd_p = {
            "conv_w": ssd_conv_w[l], "conv_b": ssd_conv_b[l].reshape(1, -1),
            "dtb_row": _pad_row(ssd_dt_bias[l]), "dtb_col": ssd_dt_bias[l].reshape(-1, 1),
            "alog_row": _pad_row(ssd_a_log[l]), "alog_col": ssd_a_log[l].reshape(-1, 1),
            "skip": jnp.repeat(ssd_d[l], SSD_HEADDIM).reshape(1, -1), "norm_g": ssd_norm_g[l].reshape(1, -1),
        }
        yc = None
        for d in range(2):
            ssd_p["expand"] = (jnp.arange(LANES)[:, None] == d * SSD_HEADS + head_of_channel[None, :]).astype(BF16)
            yc = _ssd(xbc, dt, dtt, ssd_p, d, yf=yc, z=z)

        x1, h2 = _merge(ya, yb, yc, gates, xc, mods, w_branch[l].astype(BF16), w_out[l].astype(BF16),
                        ln1_g[l].reshape(1, -1), ln1_b[l].reshape(1, -1), alpha)

        counts, idx, wl = _router(h2, s, w_router[l].T, router_bias[l].reshape(-1, 1))
        fr = _experts(h2, counts.reshape(nb * N_EXPERTS), idx.reshape(nb * N_EXPERTS, 1, s + MOE_BLOCK),
                      wl.reshape(nb * N_EXPERTS, s // MOE_BLOCK, MOE_BLOCK),
                      w_up[l].astype(BF16), w_down[l].astype(BF16))
        xc = _ffn_out(x1, h2, fr, mods, ws_up[l].astype(BF16), ws_down[l].astype(BF16),
                      ln2_g[l].reshape(1, -1), ln2_b[l].reshape(1, -1), alpha, latent_only=last)
    return xc
```

```python
import functools
import math

import jax
import jax.numpy as jnp
from jax import lax
from jax.experimental import pallas as pl
from jax.experimental.pallas import tpu as pltpu

F32 = jnp.float32
BF16 = jnp.bfloat16
HIGHEST = lax.Precision.HIGHEST

D_MODEL = 1024
GRID_W = 64
CTX_LEN = 256
BRANCH_W = 512
ATT_HEADS = 4
ATT_DH = 64
ATT_DV = 128
ROPE_BASE = 10000.0
LRU_C = 8.0
SSD_HEADS = 8
SSD_HEADDIM = 64
SSD_HPG = 4
SSD_GROUPS = 2
SSD_STATE = 128
N_EXPERTS = 64
N_GROUPS = 8
GROUP_SIZE = N_EXPERTS // N_GROUPS
TOP_K = 8
TOPK_GROUPS = 4
EXPERT_F = 256
ROUTED_SCALE = 2.5
LN_EPS = 1e-5
RMS_EPS = 1e-6

ROW_TILE = 256
HALO = 16
VT_ROWS = 144
KEY_CHUNK = 256
MOE_BLOCK = 128
MOE_PAD = ROW_TILE
SUBLANES = 8
LANES = 128
VMEM_LIMIT = 56 * 1024 * 1024
LOG2E = 1.4426950408889634


def _cparams(sem):
    return pltpu.CompilerParams(dimension_semantics=sem, vmem_limit_bytes=VMEM_LIMIT)


def _resident(shape):
    nd = len(shape)
    return pl.BlockSpec(shape, lambda *_: (0,) * nd, pipeline_mode=pl.Buffered(1))


def _silu(x):
    return x * jax.nn.sigmoid(x)


def _softplus(x):
    return jnp.maximum(x, 0.0) + jnp.log1p(jnp.exp(-jnp.abs(x)))


def _gelu_tanh(x):
    return 0.5 * x * (1.0 + jnp.tanh(math.sqrt(2.0 / math.pi) * (x + 0.044715 * (x * x * x))))


NCHUNK = D_MODEL // LANES


def _load_token_rows(ref, n, first=0):
    return jnp.concatenate([ref[pl.ds(first * NCHUNK + s, n, stride=NCHUNK), :] for s in range(NCHUNK)], axis=1)


def _store_token_rows(ref, x):
    n = x.shape[0]
    for s in range(NCHUNK):
        ref[pl.ds(s, n, stride=NCHUNK), :] = x[:, s * LANES:(s + 1) * LANES]


def _layer_norm(x, g, b):
    mu = jnp.mean(x, axis=-1, keepdims=True)
    xc = x - mu
    var = jnp.mean(xc * xc, axis=-1, keepdims=True)
    return xc * lax.rsqrt(var + LN_EPS) * g + b


def _mod_kernel(c_ref, w_ref, b_ref, o_ref):
    s = _silu(c_ref[...])
    o_ref[0] = jnp.dot(s, w_ref[0], preferred_element_type=F32, precision=HIGHEST) + b_ref[0]


def _modulation(cc, w_mod, b_mod):
    depth = w_mod.shape[0]
    nblk = 6
    return pl.pallas_call(
        _mod_kernel,
        grid=(depth, nblk),
        in_specs=[
            pl.BlockSpec((16, D_MODEL), lambda l, j: (0, 0)),
            pl.BlockSpec((1, D_MODEL, D_MODEL), lambda l, j: (l, 0, j)),
            pl.BlockSpec((1, 1, D_MODEL), lambda l, j: (l, 0, j)),
        ],
        out_specs=pl.BlockSpec((1, 16, D_MODEL), lambda l, j: (l, 0, j)),
        out_shape=jax.ShapeDtypeStruct((depth, 16, nblk * D_MODEL), F32),
        compiler_params=_cparams(("arbitrary", "arbitrary")),
    )(cc, w_mod, b_mod.reshape(depth, 1, nblk * D_MODEL))


def _mod_spec(nb):
    return pl.BlockSpec((1, 6, D_MODEL), lambda b, i: (jnp.where(i == 0, nb, b), 0, 0))


def _inproj_kernel(x_ref, m_ref, cos_ref, sin_ref, wqk_ref, wvt_ref, ones_ref, wlxg_ref, wz_ref,
                   wxbc_ref, wdt_ref, wdtt_ref, wg_ref,
                   qk_ref, vt_ref, lxg_ref, z_ref, xbc_ref, dt_ref, dtt_ref, g_ref):
    x = x_ref[0]
    h = (x * (1.0 + m_ref[0, 1:2, :]) + m_ref[0, 0:1, :]).astype(BF16)
    nt = (((1,), (1,)), ((), ()))

    qk = jnp.dot(h, wqk_ref[...], preferred_element_type=F32)
    cos = cos_ref[...]
    sin = sin_ref[...]
    lane = lax.broadcasted_iota(jnp.int32, cos.shape, 1)
    first_half = (lane % ATT_DH) < (ATT_DH // 2)
    for j in range(2 * ATT_HEADS):
        blk = qk[:, j * LANES:(j + 1) * LANES]
        partner = jnp.where(first_half, pltpu.roll(blk, LANES - ATT_DH // 2, 1),
                            pltpu.roll(blk, ATT_DH // 2, 1))
        r = blk * cos + partner * sin
        if j < ATT_HEADS:
            r = r * (ATT_DH ** -0.5 * LOG2E)
        qk_ref[0, :, j * LANES:(j + 1) * LANES] = r.astype(BF16)

    vt = lax.dot_general(wvt_ref[...], h, nt, preferred_element_type=F32) + ones_ref[...]
    vt_ref[0] = vt.astype(BF16)
    lxg_ref[0] = jnp.dot(h, wlxg_ref[...], preferred_element_type=F32).astype(BF16)
    z_ref[0] = jnp.dot(h, wz_ref[...], preferred_element_type=F32).astype(BF16)
    xbc_ref[0] = jnp.dot(h, wxbc_ref[...], preferred_element_type=F32).astype(BF16)
    dt_ref[0] = jnp.dot(h, wdt_ref[...], preferred_element_type=F32)
    dtt_ref[0] = lax.dot_general(wdtt_ref[...], h, nt, preferred_element_type=F32)
    g_ref[0] = jnp.dot(h, wg_ref[...], preferred_element_type=F32).astype(BF16)


def _inproj(xc, mods, cos_t, sin_t, w):
    nb, s, _ = xc.shape
    nt = s // ROW_TILE
    row = lambda width: pl.BlockSpec((1, ROW_TILE, width), lambda b, i: (b, i, 0))
    col = lambda rows: pl.BlockSpec((1, rows, ROW_TILE), lambda b, i: (b, 0, i))
    vt_rows = ATT_HEADS * VT_ROWS
    outs = [
        (jax.ShapeDtypeStruct((nb, s, 2 * BRANCH_W), BF16), row(2 * BRANCH_W)),
        (jax.ShapeDtypeStruct((nb, vt_rows, s), BF16), col(vt_rows)),
        (jax.ShapeDtypeStruct((nb, s, 2 * BRANCH_W), BF16), row(2 * BRANCH_W)),
        (jax.ShapeDtypeStruct((nb, s, BRANCH_W), BF16), row(BRANCH_W)),
        (jax.ShapeDtypeStruct((nb, s, 2 * BRANCH_W), BF16), row(2 * BRANCH_W)),
        (jax.ShapeDtypeStruct((nb, s, LANES), F32), row(LANES)),
        (jax.ShapeDtypeStruct((nb, 2 * SSD_HEADS, s), F32), col(2 * SSD_HEADS)),
        (jax.ShapeDtypeStruct((nb, s, 3 * D_MODEL), BF16), row(3 * D_MODEL)),
    ]
    weights = [w["qk"], w["vt"], w["ones"], w["lxg"], w["z"], w["xbc"], w["dt"], w["dtt"], w["g"]]
    return pl.pallas_call(
        _inproj_kernel,
        grid=(nb, nt),
        in_specs=[row(D_MODEL), _mod_spec(nb),
                  pl.BlockSpec((ROW_TILE, LANES), lambda b, i: (i, 0)),
                  pl.BlockSpec((ROW_TILE, LANES), lambda b, i: (i, 0))]
                 + [_resident(a.shape) for a in weights],
        out_specs=[o[1] for o in outs],
        out_shape=[o[0] for o in outs],
        compiler_params=_cparams(("arbitrary", "arbitrary")),
    )(xc, mods, cos_t, sin_t, *weights)


def _attn_kernel(lq_ref, lk_ref, g_ref, q_ref, qn_ref, k_ref, vt_ref, o_ref, sa_ref, sb_ref, m_ref, *, lam_init):
    i = pl.program_id(2)
    tq = q_ref.shape[1]
    nck = k_ref.shape[1] // KEY_CHUNK
    prod = lq_ref[...] * lk_ref[...]
    d0 = jnp.sum(prod[0:1, :], axis=1, keepdims=True)
    d1 = jnp.sum(prod[1:2, :], axis=1, keepdims=True)
    lam = jnp.exp(d0) - jnp.exp(d1) + lam_init
    gcol = g_ref[0] * (1.0 - lam_init)

    def stack_maps(q):
        lane = lax.broadcasted_iota(jnp.int32, q.shape, 1)
        zero = jnp.zeros_like(q)
        return jnp.concatenate([jnp.where(lane < ATT_DH, q, zero), jnp.where(lane >= ATT_DH, q, zero)], axis=0)

    def scores(c, q2, s_ref, m):
        rows = slice(c * KEY_CHUNK, (c + 1) * KEY_CHUNK)
        st = lax.dot_general(k_ref[0, rows, :], q2, (((1,), (1,)), ((), ())),
                             preferred_element_type=F32)
        s_ref[rows, :] = st
        mc = jnp.max(st, axis=0, keepdims=True)
        return mc if m is None else jnp.maximum(m, mc)

    def weigh(c, s_ref, m, acc):
        rows = slice(c * KEY_CHUNK, (c + 1) * KEY_CHUNK)
        e = jnp.exp2((s_ref[rows, :] - m).astype(BF16))
        pv = jnp.dot(vt_ref[0, :, rows], e, preferred_element_type=F32)
        return pv if acc is None else acc + pv

    def finish(acc):
        r = 1.0 / acc[ATT_DV:ATT_DV + 1, :]
        o = acc[:ATT_DV, :tq] * r[:, :tq] - lam * (acc[:ATT_DV, tq:] * r[:, tq:])
        ms = jnp.mean(o * o, axis=0, keepdims=True)
        o_ref[0] = (o * lax.rsqrt(ms + RMS_EPS) * gcol).T.astype(BF16)

    @pl.when(i == 0)
    def _():
        q2 = stack_maps(q_ref[0])
        finish(weigh(0, sa_ref, scores(0, q2, sa_ref, None), None))
        q2n = stack_maps(qn_ref[0])
        mn = None
        for c in range(nck):
            mn = scores(c, q2n, sb_ref, mn)
        m_ref[1:2, :] = mn

    def step(cur_ref, nxt_ref, cur_slot, nxt_slot):
        q2n = stack_maps(qn_ref[0])
        m_cur = m_ref[cur_slot:cur_slot + 1, :]
        mn = acc = None
        for c in range(nck):
            mn = scores(c, q2n, nxt_ref, mn)
            acc = weigh(c, cur_ref, m_cur, acc)
        finish(acc)
        m_ref[nxt_slot:nxt_slot + 1, :] = mn

    @pl.when(i % 2 == 1)
    def _():
        step(sb_ref, sa_ref, 1, 0)

    @pl.when(jnp.logical_and(i % 2 == 0, i > 0))
    def _():
        step(sa_ref, sb_ref, 0, 1)


def _attention(qk, vt, lam_q, lam_k, attn_g, lam_init):
    nb, s, _ = qk.shape
    nq = s // ROW_TILE
    return pl.pallas_call(
        functools.partial(_attn_kernel, lam_init=lam_init),
        grid=(nb, ATT_HEADS, nq),
        in_specs=[
            pl.BlockSpec((2, ATT_DH), lambda b, h, i: (0, 0)),
            pl.BlockSpec((2, ATT_DH), lambda b, h, i: (0, 0)),
            pl.BlockSpec((1, ATT_DV, 1), lambda b, h, i: (h, 0, 0)),
            pl.BlockSpec((1, ROW_TILE, LANES), lambda b, h, i: (b, i, h)),
            pl.BlockSpec((1, ROW_TILE, LANES), lambda b, h, i: (b, jnp.minimum(i + 1, nq - 1), h)),
            pl.BlockSpec((1, s, LANES), lambda b, h, i: (b, 0, ATT_HEADS + h)),
            pl.BlockSpec((1, VT_ROWS, s), lambda b, h, i: (b, h, 0)),
        ],
        out_specs=pl.BlockSpec((1, ROW_TILE, LANES), lambda b, h, i: (b, i, h)),
        out_shape=jax.ShapeDtypeStruct((nb, s, BRANCH_W), BF16),
        scratch_shapes=[pltpu.VMEM((s, 2 * ROW_TILE), F32), pltpu.VMEM((s, 2 * ROW_TILE), F32),
                        pltpu.VMEM((SUBLANES, 2 * ROW_TILE), F32)],
        compiler_params=_cparams(("arbitrary", "arbitrary", "arbitrary")),
    )(lam_q, lam_k, attn_g.reshape(ATT_HEADS, ATT_DV, 1), qk, qk, qk, vt)


def _scan_chunk(j, nc, reverse):
    if not reverse:
        return j
    return jnp.where(j == 0, 0, nc - j)


def _conv4(x_ref, xp_ref, xn_ref, c, nc, w_ref, b_ref):
    x = x_ref[0].astype(F32)
    t = x.shape[0]
    row = lax.broadcasted_iota(jnp.int32, x.shape, 0)
    prev_ok = (c >= 2).astype(F32)
    next_ok = jnp.logical_and(c >= 1, c < nc - 1).astype(F32)
    prev = xp_ref[0].astype(F32)
    p1 = prev[HALO - 1:HALO, :] * prev_ok
    p2 = prev[HALO - 2:HALO - 1, :] * prev_ok
    n0 = xn_ref[0].astype(F32)[0:1, :] * next_ok
    xm1 = jnp.where(row == 0, p1, pltpu.roll(x, 1, 0))
    xm2 = jnp.where(row == 0, p2, jnp.where(row == 1, p1, pltpu.roll(x, 2, 0)))
    xp1 = jnp.where(row == t - 1, n0, pltpu.roll(x, t - 1, 0))
    return w_ref[0:1, :] * xm2 + w_ref[1:2, :] * xm1 + w_ref[2:3, :] * x + w_ref[3:4, :] * xp1 + b_ref[...]


def _tile_specs(width, nc, reverse, blk=0):
    per = ROW_TILE // HALO
    last = nc * per - 1
    ch = lambda j: _scan_chunk(j, nc, reverse)
    cur = pl.BlockSpec((1, ROW_TILE, width), lambda b, j: (b, ch(j), blk))
    prev = pl.BlockSpec((1, HALO, width), lambda b, j: (b, jnp.maximum(ch(j) * per - 1, 0), blk))
    nxt = pl.BlockSpec((1, HALO, width), lambda b, j: (b, jnp.minimum((ch(j) + 1) * per, last), blk))
    return cur, prev, nxt


def _rglru_kernel(*refs, reverse, nc):
    if reverse:
        (x_ref, xp_ref, xn_ref, cw_ref, cb_ref, wg_ref, ba_ref, bi_ref, lam_ref,
         yf_ref, lg_ref, o_ref, h_ref) = refs
    else:
        (x_ref, xp_ref, xn_ref, cw_ref, cb_ref, wg_ref, ba_ref, bi_ref, lam_ref, o_ref, h_ref) = refs
    j = pl.program_id(1)
    c = _scan_chunk(j, nc, reverse)

    @pl.when(j == 0)
    def _():
        h_ref[...] = jnp.zeros_like(h_ref)

    u = _conv4(x_ref, xp_ref, xn_ref, c, nc, cw_ref, cb_ref)
    t = u.shape[0]
    pre = jnp.dot(u.astype(BF16), wg_ref[...], preferred_element_type=F32)
    r = jax.nn.sigmoid(pre[:, :BRANCH_W] + ba_ref[...])
    gi = jax.nn.sigmoid(pre[:, BRANCH_W:] + bi_ref[...])
    log_a = (-LRU_C) * r * _softplus(-lam_ref[...])
    a = jnp.exp(log_a)
    var = 1.0 - jnp.exp(2.0 * log_a)
    bv = jnp.where(var > 0.0, var * lax.rsqrt(var), 0.0) * (gi * u)

    in_group = lax.broadcasted_iota(jnp.int32, u.shape, 0) % SUBLANES
    d = 1
    while d < SUBLANES:
        if reverse:
            keep = in_group < SUBLANES - d
            sh = t - d
        else:
            keep = in_group >= d
            sh = d
        a_s = jnp.where(keep, pltpu.roll(a, sh, 0), 1.0)
        b_s = jnp.where(keep, pltpu.roll(bv, sh, 0), 0.0)
        bv = a * b_s + bv
        a = a * a_s
        d *= 2
    carry = jnp.broadcast_to(h_ref[...], (SUBLANES, BRANCH_W))
    ngroup = t // SUBLANES
    pieces = [None] * ngroup
    for gidx in (range(ngroup - 1, -1, -1) if reverse else range(ngroup)):
        rows = slice(gidx * SUBLANES, (gidx + 1) * SUBLANES)
        pieces[gidx] = a[rows, :] * carry + bv[rows, :]
        edge = pieces[gidx][0:1, :] if reverse else pieces[gidx][SUBLANES - 1:SUBLANES, :]
        carry = jnp.broadcast_to(edge, (SUBLANES, BRANCH_W))
    h_ref[...] = carry[0:1, :]
    hs = jnp.concatenate(pieces, axis=0)
    if reverse:
        o_ref[0] = ((yf_ref[0] + hs) * _gelu_tanh(lg_ref[0].astype(F32))).astype(BF16)
    else:
        o_ref[0] = hs


def _rglru(lxg, conv_w, conv_b, w_gate, b_a, b_i, lam, reverse, yf=None):
    nb, s, _ = lxg.shape
    nc = s // ROW_TILE
    cur, prev, nxt = _tile_specs(BRANCH_W, nc, reverse)
    params = [conv_w, conv_b, w_gate, b_a, b_i, lam]
    in_specs = [cur, prev, nxt] + [_resident(p.shape) for p in params]
    args = [lxg, lxg, lxg] + params
    if reverse:
        in_specs += [cur, _tile_specs(BRANCH_W, nc, reverse, blk=1)[0]]
        args += [yf, lxg]
    return pl.pallas_call(
        functools.partial(_rglru_kernel, reverse=reverse, nc=nc),
        grid=(nb, nc),
        in_specs=in_specs,
        out_specs=cur,
        out_shape=jax.ShapeDtypeStruct((nb, s, BRANCH_W), BF16 if reverse else F32),
        scratch_shapes=[pltpu.VMEM((1, BRANCH_W), F32)],
        compiler_params=_cparams(("arbitrary", "arbitrary")),
    )(*args)


def _ssd_kernel(*refs, reverse, nc, direction):
    if reverse:
        (x_ref, xp_ref, xn_ref, dt_ref, dtt_ref, cw_ref, cb_ref, dtb_ref, dtbc_ref, alog_ref, alogc_ref,
         exp_ref, yf_ref, z_ref, skip_ref, g_ref, o_ref, st_ref) = refs
    else:
        (x_ref, xp_ref, xn_ref, dt_ref, dtt_ref, cw_ref, cb_ref, dtb_ref, dtbc_ref, alog_ref, alogc_ref,
         exp_ref, o_ref, st_ref) = refs
    j = pl.program_id(1)
    c = _scan_chunk(j, nc, reverse)

    @pl.when(j == 0)
    def _():
        st_ref[...] = jnp.zeros_like(st_ref)

    u = _silu(_conv4(x_ref, xp_ref, xn_ref, c, nc, cw_ref, cb_ref))
    t = u.shape[0]
    xs = u[:, :BRANCH_W]
    gw = SSD_HPG * SSD_HEADDIM

    dt_c = _softplus(dt_ref[0] + dtb_ref[...])
    a_c = dt_c * (-jnp.exp(alog_ref[...]))
    a_r = _softplus(dtt_ref[0] + dtbc_ref[...]) * (-jnp.exp(alogc_ref[...]))
    ri = lax.broadcasted_iota(jnp.int32, (t, t), 0)
    ci = lax.broadcasted_iota(jnp.int32, (t, t), 1)
    lower = (ci <= ri).astype(F32)
    upper = (ci >= ri).astype(F32)
    cs_c = jnp.dot(upper if reverse else lower, a_c, preferred_element_type=F32, precision=HIGHEST)
    cs_r = jnp.dot(a_r, lower if reverse else upper, preferred_element_type=F32, precision=HIGHEST)
    tot = cs_c[0:1, :] if reverse else cs_c[t - 1:t, :]
    keep = (ci >= ri) if reverse else (ci <= ri)

    expand = exp_ref[...]
    xdt = xs * jnp.dot(dt_c.astype(BF16), expand, preferred_element_type=F32)
    e_in = jnp.dot(jnp.exp(cs_c).astype(BF16), expand, preferred_element_type=F32)
    e_out = jnp.dot(jnp.exp(tot - cs_c).astype(BF16), expand, preferred_element_type=F32)
    e_tot = jnp.dot(jnp.broadcast_to(jnp.exp(tot), (SUBLANES, LANES)), expand.astype(F32),
                    preferred_element_type=F32, precision=HIGHEST)[0:1, :]
    xdt_b = xdt.astype(BF16)
    xdec_b = (xdt * e_out).astype(BF16)
    lane_head = lax.broadcasted_iota(jnp.int32, (t, gw), 1) // SSD_HEADDIM
    zero_b = jnp.zeros((t, gw), BF16)

    ys = []
    for g in range(SSD_GROUPS):
        bm = u[:, BRANCH_W + g * SSD_STATE:BRANCH_W + (g + 1) * SSD_STATE].astype(BF16)
        cm = u[:, BRANCH_W + (SSD_GROUPS + g) * SSD_STATE:BRANCH_W + (SSD_GROUPS + g + 1) * SSD_STATE].astype(BF16)
        cb = lax.dot_general(cm, bm, (((1,), (1,)), ((), ())), preferred_element_type=F32)
        st_g = st_ref[:, g * gw:(g + 1) * gw]
        y = jnp.dot(cm, st_g.astype(BF16), preferred_element_type=F32) * e_in[:, g * gw:(g + 1) * gw]
        xg = xdt_b[:, g * gw:(g + 1) * gw]
        for rr in range(SSD_HPG):
            col = direction * SSD_HEADS + g * SSD_HPG + rr
            decay = jnp.where(keep, jnp.exp(cs_c[:, col:col + 1] - cs_r[col:col + 1, :]), 0.0)
            gm = (cb * decay).astype(BF16)
            y = y + jnp.dot(gm, jnp.where(lane_head == rr, xg, zero_b), preferred_element_type=F32)
        ys.append(y)
        upd = lax.dot_general(bm, xdec_b[:, g * gw:(g + 1) * gw], (((0,), (0,)), ((), ())),
                              preferred_element_type=F32)
        st_ref[:, g * gw:(g + 1) * gw] = st_g * e_tot[:, g * gw:(g + 1) * gw] + upd
    y = jnp.concatenate(ys, axis=1)

    if reverse:
        y = yf_ref[0] + y + skip_ref[...] * xs
        tz = y * _silu(z_ref[0].astype(F32))
        outs = []
        for g in range(SSD_GROUPS):
            tg = tz[:, g * gw:(g + 1) * gw]
            outs.append(tg * lax.rsqrt(jnp.mean(tg * tg, axis=1, keepdims=True) + RMS_EPS))
        o_ref[0] = (jnp.concatenate(outs, axis=1) * g_ref[...]).astype(BF16)
    else:
        o_ref[0] = y


def _ssd(xbc, dt, dtt, p, direction, yf=None, z=None):
    reverse = direction == 1
    nb, s, _ = xbc.shape
    nc = s // ROW_TILE
    cur, prev, nxt = _tile_specs(2 * BRANCH_W, nc, reverse)
    ch = lambda j: _scan_chunk(j, nc, reverse)
    params = [p["conv_w"], p["conv_b"], p["dtb_row"], p["dtb_col"], p["alog_row"], p["alog_col"], p["expand"]]
    in_specs = [cur, prev, nxt,
                pl.BlockSpec((1, ROW_TILE, LANES), lambda b, j: (b, ch(j), 0)),
                pl.BlockSpec((1, 2 * SSD_HEADS, ROW_TILE), lambda b, j: (b, 0, ch(j)))]
    in_specs += [_resident(a.shape) for a in params]
    args = [xbc, xbc, xbc, dt, dtt] + params
    half = pl.BlockSpec((1, ROW_TILE, BRANCH_W), lambda b, j: (b, ch(j), 0))
    if reverse:
        in_specs += [half, half, _resident(p["skip"].shape), _resident(p["norm_g"].shape)]
        args += [yf, z, p["skip"], p["norm_g"]]
    return pl.pallas_call(
        functools.partial(_ssd_kernel, reverse=reverse, nc=nc, direction=direction),
        grid=(nb, nc),
        in_specs=in_specs,
        out_specs=half,
        out_shape=jax.ShapeDtypeStruct((nb, s, BRANCH_W), BF16 if reverse else F32),
        scratch_shapes=[pltpu.VMEM((SSD_STATE, BRANCH_W), F32)],
        compiler_params=_cparams(("arbitrary", "arbitrary")),
    )(*args)


def _merge_kernel(ya_ref, yb_ref, yc_ref, gt_ref, x_ref, m_ref, wb_ref, wo_ref, g_ref, b_ref,
                  x1_ref, h2_ref, *, alpha):
    acc = None
    for n, y_ref in enumerate((ya_ref, yb_ref, yc_ref)):
        proj = jnp.dot(y_ref[0], wb_ref[n], preferred_element_type=F32)
        gate = jax.nn.sigmoid(gt_ref[0, :, n * D_MODEL:(n + 1) * D_MODEL].astype(F32))
        acc = gate * proj if acc is None else acc + gate * proj
    mix = jnp.dot(acc.astype(BF16), wo_ref[...], preferred_element_type=F32)
    x1 = _layer_norm(alpha * x_ref[0] + m_ref[0, 2:3, :] * mix, g_ref[...], b_ref[...])
    x1_ref[0] = x1
    _store_token_rows(h2_ref.at[0], x1 * (1.0 + m_ref[0, 4:5, :]) + m_ref[0, 3:4, :])


def _zero_tile_kernel(h_ref, o_ref):
    del h_ref
    o_ref[...] = jnp.zeros_like(o_ref)


def _merge(ya, yb, yc, gates, xc, mods, w_branch, w_out, ln_g, ln_b, alpha):
    nb, s, _ = xc.shape
    nt = s // ROW_TILE
    row = lambda width: pl.BlockSpec((1, ROW_TILE, width), lambda b, i: (b, i, 0))
    h2_shape = jax.ShapeDtypeStruct((nb, (s + MOE_PAD) * NCHUNK, LANES), F32)
    x1, h2 = pl.pallas_call(
        functools.partial(_merge_kernel, alpha=alpha),
        grid=(nb, nt),
        in_specs=[row(BRANCH_W), row(BRANCH_W), row(BRANCH_W), row(3 * D_MODEL), row(D_MODEL), _mod_spec(nb),
                  _resident(w_branch.shape), _resident(w_out.shape), _resident(ln_g.shape), _resident(ln_b.shape)],
        out_specs=[row(D_MODEL), pl.BlockSpec((1, ROW_TILE * NCHUNK, LANES), lambda b, i: (b, i, 0))],
        out_shape=[jax.ShapeDtypeStruct((nb, s, D_MODEL), F32), h2_shape],
        compiler_params=_cparams(("arbitrary", "arbitrary")),
    )(ya, yb, yc, gates, xc, mods, w_branch, w_out, ln_g, ln_b)
    h2 = pl.pallas_call(
        _zero_tile_kernel,
        grid=(nb,),
        in_specs=[pl.BlockSpec(memory_space=pl.ANY)],
        out_specs=pl.BlockSpec((1, NCHUNK, LANES), lambda b: (b, s, 0)),
        out_shape=h2_shape,
        input_output_aliases={0: 0},
        compiler_params=_cparams(("arbitrary",)),
    )(h2)
    return x1, h2


def _first_index(hit, ridx, n):
    return jnp.min(jnp.where(hit, ridx, n), axis=0, keepdims=True)


def _route_kernel(h_ref, wr_ref, bias_ref, cnt_ref, rank_ref, wgt_ref, base_ref):
    tm = wgt_ref.shape[2]

    @pl.when(pl.program_id(1) == 0)
    def _():
        base_ref[...] = jnp.zeros_like(base_ref)

    logits = lax.dot_general(wr_ref[...], _load_token_rows(h_ref.at[0], tm), (((1,), (1,)), ((), ())),
                             preferred_element_type=F32, precision=HIGHEST)
    scores = jax.nn.sigmoid(logits)
    sel = scores + bias_ref[...]
    neg = -jnp.inf

    r8 = lax.broadcasted_iota(jnp.int32, (GROUP_SIZE, tm), 0)
    grp = jnp.zeros((N_GROUPS, tm), F32)
    for g in range(N_GROUPS):
        blk = sel[g * GROUP_SIZE:(g + 1) * GROUP_SIZE, :]
        m1 = jnp.max(blk, axis=0, keepdims=True)
        first = _first_index(blk == m1, r8, GROUP_SIZE)
        m2 = jnp.max(jnp.where(r8 == first, neg, blk), axis=0, keepdims=True)
        grp = jnp.where(r8 == g, m1 + m2, grp)
    gsel = jnp.zeros((N_GROUPS, tm), jnp.int32)
    for _ in range(TOPK_GROUPS):
        m = jnp.max(grp, axis=0, keepdims=True)
        hit = r8 == _first_index(grp == m, r8, N_GROUPS)
        gsel = jnp.where(hit, 1, gsel)
        grp = jnp.where(hit, neg, grp)
    cand = jnp.concatenate(
        [jnp.where(gsel[g:g + 1, :] > 0, sel[g * GROUP_SIZE:(g + 1) * GROUP_SIZE, :], neg) for g in range(N_GROUPS)],
        axis=0)

    re = lax.broadcasted_iota(jnp.int32, (N_EXPERTS, tm), 0)
    chosen = jnp.zeros((N_EXPERTS, tm), jnp.int32)
    for _ in range(TOP_K):
        m = jnp.max(cand, axis=0, keepdims=True)
        hit = re == _first_index(cand == m, re, N_EXPERTS)
        chosen = jnp.where(hit, 1, chosen)
        cand = jnp.where(hit, neg, cand)
    picked = jnp.where(chosen > 0, scores, 0.0)
    wgt = picked / jnp.sum(picked, axis=0, keepdims=True) * ROUTED_SCALE

    li = lax.broadcasted_iota(jnp.int32, (LANES, LANES), 0)
    lj = lax.broadcasted_iota(jnp.int32, (LANES, LANES), 1)
    strict = (li < lj).astype(BF16)
    chosen_b = chosen.astype(F32).astype(BF16)
    base = base_ref[...]
    pos = []
    for kb in range(tm // LANES):
        cblk = chosen_b[:, kb * LANES:(kb + 1) * LANES]
        pos.append(jnp.dot(cblk, strict, preferred_element_type=F32) + base)
        base = base + jnp.sum(cblk.astype(F32), axis=1, keepdims=True)
    base_ref[...] = base
    cnt_ref[0] = base.astype(jnp.int32)
    rank_ref[0] = jnp.where(chosen > 0, jnp.concatenate(pos, axis=1).astype(jnp.int32), -1)
    wgt_ref[0] = wgt


def _compact_kernel(cnt_ref, rank_ref, wgt_ref, idx_ref, wl_ref):
    rank = rank_ref[0]
    ne, tm = rank.shape
    lane = lax.broadcasted_iota(jnp.int32, (ne, tm), 1)
    live = jnp.where(rank >= 0, 1, 0)
    dist = lane - rank
    val = lane
    wv = wgt_ref[0]
    step = 1
    while step < tm:
        move = jnp.where(jnp.logical_and(live > 0, (dist & step) != 0), 1, 0)
        sh = tm - step
        take = pltpu.roll(move, sh, 1) > 0
        val = jnp.where(take, pltpu.roll(val, sh, 1), val)
        wv = jnp.where(take, pltpu.roll(wv, sh, 1), wv)
        dist = jnp.where(take, pltpu.roll(dist, sh, 1), dist)
        live = jnp.where(take, 1, jnp.where(move > 0, 0, live))
        step *= 2
    valid = lane < cnt_ref[0]
    spare = jnp.full((ne, MOE_BLOCK), tm * NCHUNK, jnp.int32)
    idx_ref[0] = jnp.concatenate([jnp.where(valid, val * NCHUNK, tm * NCHUNK), spare], axis=1)
    wl_ref[0] = jnp.where(valid, wv, 0.0)


def _router(h2, s, w_router_t, bias_col):
    nb = h2.shape[0]
    dense = pl.BlockSpec((1, N_EXPERTS, ROW_TILE), lambda b, i: (b, 0, i))
    counts, rank, wgt = pl.pallas_call(
        _route_kernel,
        grid=(nb, s // ROW_TILE),
        in_specs=[pl.BlockSpec((1, ROW_TILE * NCHUNK, LANES), lambda b, i: (b, i, 0)),
                  _resident(w_router_t.shape), _resident(bias_col.shape)],
        out_specs=[pl.BlockSpec((1, N_EXPERTS, 1), lambda b, i: (b, 0, 0)), dense, dense],
        out_shape=[jax.ShapeDtypeStruct((nb, N_EXPERTS, 1), jnp.int32),
                   jax.ShapeDtypeStruct((nb, N_EXPERTS, s), jnp.int32),
                   jax.ShapeDtypeStruct((nb, N_EXPERTS, s), F32)],
        scratch_shapes=[pltpu.VMEM((N_EXPERTS, 1), F32)],
        compiler_params=_cparams(("arbitrary", "arbitrary")),
    )(h2, w_router_t, bias_col)
    rows = lambda width: pl.BlockSpec((1, SUBLANES, width), lambda b, g: (b, g, 0))
    idx, wl = pl.pallas_call(
        _compact_kernel,
        grid=(nb, N_EXPERTS // SUBLANES),
        in_specs=[rows(1), rows(s), rows(s)],
        out_specs=[rows(s + MOE_BLOCK), rows(s)],
        out_shape=[jax.ShapeDtypeStruct((nb, N_EXPERTS, s + MOE_BLOCK), jnp.int32),
                   jax.ShapeDtypeStruct((nb, N_EXPERTS, s), F32)],
        compiler_params=_cparams(("arbitrary", "arbitrary")),
    )(counts, rank, wgt)
    return counts, idx, wl


SCATTER_GROUP = 16
EXPERTS_PER_STEP = 4


def _experts_kernel(cnt_ref, idx_ref, wl_ref, h_ref, wu_ref, wd_ref, o_ref, xa_ref, xb_ref, ya_ref, yb_ref):
    t = pl.program_id(0)
    eg = pl.program_id(1)
    h_rows = h_ref.at[0]
    acc_ref = o_ref.at[0]

    @pl.when(eg == 0)
    def _():
        o_ref[...] = jnp.zeros_like(o_ref)

    starts = [0]
    for k in range(EXPERTS_PER_STEP):
        count = cnt_ref[(t * (N_EXPERTS // EXPERTS_PER_STEP) + eg) * EXPERTS_PER_STEP + k]
        starts.append(starts[-1] + (count + MOE_BLOCK - 1) // MOE_BLOCK)
    nblk = starts[-1]
    ri = lax.broadcasted_iota(jnp.int32, (MOE_BLOCK, MOE_BLOCK), 0)
    ci = lax.broadcasted_iota(jnp.int32, (MOE_BLOCK, MOE_BLOCK), 1)

    def locate(blk):
        slot = 0
        for k in range(1, EXPERTS_PER_STEP):
            slot = slot + (blk >= starts[k]).astype(jnp.int32)
        first = starts[0]
        for k in range(1, EXPERTS_PER_STEP):
            first = jnp.where(slot == k, starts[k], first)
        return slot, blk - first

    def gather(blk, x_ref):
        slot, local = locate(blk)
        ids = idx_ref.at[slot, 0, pl.ds(local * MOE_BLOCK, MOE_BLOCK)]
        for r in range(MOE_BLOCK):
            x_ref[pl.ds(r * NCHUNK, NCHUNK), :] = h_rows[pl.ds(pl.multiple_of(ids[r], NCHUNK), NCHUNK), :]

    def ffn(blk, x_ref, y_ref):
        slot, local = locate(blk)
        x = _load_token_rows(x_ref, MOE_BLOCK)
        gu = jnp.dot(x.astype(BF16), wu_ref[slot], preferred_element_type=F32)
        act = _silu(gu[:, :EXPERT_F]) * gu[:, EXPERT_F:]
        y = jnp.dot(act.astype(BF16), wd_ref[slot], preferred_element_type=F32)
        w_row = wl_ref[slot, pl.ds(local, 1), :]
        w_col = jnp.sum(jnp.where(ri == ci, w_row, 0.0), axis=1, keepdims=True)
        _store_token_rows(y_ref, y * w_col)

    def scatter(blk, y_ref):
        slot, local = locate(blk)
        ids = idx_ref.at[slot, 0, pl.ds(local * MOE_BLOCK, MOE_BLOCK)]
        for g0 in range(0, MOE_BLOCK, SCATTER_GROUP):
            rows = range(g0, g0 + SCATTER_GROUP)
            dst = [pl.multiple_of(ids[r], NCHUNK) for r in rows]
            new = [acc_ref[pl.ds(d, NCHUNK), :] + y_ref[pl.ds(r * NCHUNK, NCHUNK), :] for d, r in zip(dst, rows)]
            for d, v in zip(dst, new):
                acc_ref[pl.ds(d, NCHUNK), :] = v

    @pl.when(nblk > 0)
    def _():
        gather(0, xa_ref)

    def pair(j, carry):
        b0 = 2 * j
        ffn(b0, xa_ref, ya_ref)
        gather(b0 + 1, xb_ref)
        ffn(b0 + 1, xb_ref, yb_ref)
        scatter(b0, ya_ref)
        gather(b0 + 2, xa_ref)
        scatter(b0 + 1, yb_ref)
        return carry

    lax.fori_loop(0, nblk // 2, pair, 0)

    @pl.when(nblk % 2 == 1)
    def _():
        ffn(nblk - 1, xa_ref, ya_ref)
        scatter(nblk - 1, ya_ref)


def _experts(h2, counts, idx, wl, w_up, w_down):
    nb, rows, _ = h2.shape
    nlist = idx.shape[-1]
    tile = pl.BlockSpec((1, rows, LANES), lambda t, e, c: (t, 0, 0), pipeline_mode=pl.Buffered(1))
    buf = pltpu.VMEM((MOE_BLOCK * NCHUNK, LANES), F32)
    eps = EXPERTS_PER_STEP
    groups = N_EXPERTS // eps
    grid_spec = pltpu.PrefetchScalarGridSpec(
        num_scalar_prefetch=1,
        grid=(nb, groups),
        in_specs=[pl.BlockSpec((eps, 1, nlist), lambda t, e, c: (t * groups + e, 0, 0), memory_space=pltpu.SMEM),
                  pl.BlockSpec((eps,) + wl.shape[1:], lambda t, e, c: (t * groups + e, 0, 0)),
                  tile,
                  pl.BlockSpec((eps, D_MODEL, 2 * EXPERT_F), lambda t, e, c: (e, 0, 0)),
                  pl.BlockSpec((eps, EXPERT_F, D_MODEL), lambda t, e, c: (e, 0, 0))],
        out_specs=tile,
        scratch_shapes=[buf, buf, buf, buf],
    )
    return pl.pallas_call(
        _experts_kernel,
        grid_spec=grid_spec,
        out_shape=jax.ShapeDtypeStruct(h2.shape, F32),
        compiler_params=_cparams(("arbitrary", "arbitrary")),
    )(counts, idx, wl, h2, w_up, w_down)


def _ffn_out_kernel(x1_ref, h2_ref, fr_ref, m_ref, wu_ref, wd_ref, g_ref, b_ref, o_ref, *, alpha):
    gu = jnp.dot(_load_token_rows(h2_ref.at[0], ROW_TILE).astype(BF16), wu_ref[...], preferred_element_type=F32)
    act = _silu(gu[:, :EXPERT_F]) * gu[:, EXPERT_F:]
    f = (jnp.dot(act.astype(BF16), wd_ref[...], preferred_element_type=F32)
         + _load_token_rows(fr_ref.at[0], ROW_TILE))
    o_ref[0] = _layer_norm(alpha * x1_ref[0] + m_ref[0, 5:6, :] * f, g_ref[...], b_ref[...])


def _ffn_out(x1, h2, fr, mods, ws_up, ws_down, ln_g, ln_b, alpha, latent_only):
    nb, s, _ = x1.shape
    nt = s // ROW_TILE
    skip = CTX_LEN // ROW_TILE if latent_only else 0
    row = pl.BlockSpec((1, ROW_TILE, D_MODEL), lambda b, i: (b, i + skip, 0))
    mod = pl.BlockSpec((1, 6, D_MODEL), lambda b, i: (jnp.where(i + skip == 0, nb, b), 0, 0))
    chunked = pl.BlockSpec((1, ROW_TILE * NCHUNK, LANES), lambda b, i: (b, i + skip, 0))
    return pl.pallas_call(
        functools.partial(_ffn_out_kernel, alpha=alpha),
        grid=(nb, nt - skip),
        in_specs=[row, chunked, chunked, mod, _resident(ws_up.shape), _resident(ws_down.shape),
                  _resident(ln_g.shape), _resident(ln_b.shape)],
        out_specs=pl.BlockSpec((1, ROW_TILE, D_MODEL), lambda b, i: (b, i, 0)),
        out_shape=jax.ShapeDtypeStruct((nb, s - skip * ROW_TILE, D_MODEL), F32),
        compiler_params=_cparams(("arbitrary", "arbitrary")),
    )(x1, h2, fr, mods, ws_up, ws_down, ln_g, ln_b)


def _rope_tables(n_lat):
    t = jnp.arange(n_lat)
    rowp = (t // GRID_W).astype(F32)
    colp = (t % GRID_W).astype(F32)
    n_freq = ATT_DH // 4
    inv = ROPE_BASE ** (-jnp.arange(n_freq, dtype=F32) / n_freq)
    ang = jnp.concatenate([rowp[:, None] * inv, colp[:, None] * inv], axis=-1)
    lane = jnp.arange(LANES)
    cos = jnp.cos(ang)[:, lane % (ATT_DH // 2)]
    sign = jnp.where((lane % ATT_DH) < ATT_DH // 2, -1.0, 1.0).astype(F32)
    sin = jnp.sin(ang)[:, lane % (ATT_DH // 2)] * sign
    cos = jnp.concatenate([jnp.ones((CTX_LEN, LANES), F32), cos], axis=0)
    sin = jnp.concatenate([jnp.zeros((CTX_LEN, LANES), F32), sin], axis=0)
    return cos, sin


def _block_diag(w):
    n, k, _ = w.shape
    eye = jnp.eye(n, dtype=w.dtype)
    return (eye[:, None, :, None] * w[:, :, None, :]).reshape(n * k, n * k)


def _inproj_weights(w):
    o_v = 2 * BRANCH_W
    o_lx = o_v + ATT_HEADS * ATT_DV
    o_z = o_lx + 2 * BRANCH_W
    o_xbc = o_z + BRANCH_W
    o_dt = o_xbc + 2 * BRANCH_W
    o_g = o_dt + 2 * SSD_HEADS
    wv = w[:, o_v:o_lx].T.reshape(ATT_HEADS, ATT_DV, D_MODEL)
    wv = jnp.pad(wv, ((0, 0), (0, VT_ROWS - ATT_DV), (0, 0))).reshape(ATT_HEADS * VT_ROWS, D_MODEL)
    ones = jnp.zeros((ATT_HEADS, VT_ROWS, 1), F32).at[:, ATT_DV, 0].set(1.0).reshape(ATT_HEADS * VT_ROWS, 1)
    wdt = w[:, o_dt:o_g]
    return {
        "qk": w[:, :o_v].astype(BF16),
        "vt": wv.astype(BF16),
        "ones": ones,
        "lxg": w[:, o_lx:o_z].astype(BF16),
        "z": w[:, o_z:o_xbc].astype(BF16),
        "xbc": w[:, o_xbc:o_dt].astype(BF16),
        "dt": jnp.pad(wdt, ((0, 0), (0, LANES - 2 * SSD_HEADS))).astype(BF16),
        "dtt": wdt.T.astype(BF16),
        "g": w[:, o_g:].astype(BF16),
    }


def _pad_row(v):
    return jnp.pad(v.reshape(1, -1), ((0, 0), (0, LANES - v.size)))


def kernel(x, c, ctx, c_ctx, w_mod, b_mod, w_in, lam_q, lam_k, attn_norm_g, lru_conv_w, lru_conv_b, lru_wa, lru_ba, lru_wi, lru_bi, lru_lambda, ssd_conv_w, ssd_conv_b, ssd_dt_bias, ssd_a_log, ssd_d, ssd_norm_g, w_branch, w_out, ln1_g, ln1_b, w_router, router_bias, w_up, w_down, ws_up, ws_down, ln2_g, ln2_b):
    nb, n_lat, _ = x.shape
    depth = w_mod.shape[0]
    assert ctx.shape[1] == CTX_LEN and n_lat % ROW_TILE == 0 and nb + 1 <= 16
    s = CTX_LEN + n_lat
    alpha = (2 * depth) ** 0.25

    xc = jnp.concatenate([ctx, x], axis=1)
    cc = jnp.zeros((16, D_MODEL), F32).at[:nb].set(c).at[nb].set(c_ctx)
    mods_all = _modulation(cc, w_mod, b_mod).reshape(depth, 16, 6, D_MODEL)
    cos_t, sin_t = _rope_tables(n_lat)
    head_of_channel = jnp.arange(BRANCH_W) // SSD_HEADDIM

    for l in range(depth):
        last = l == depth - 1
        lam_init = 0.8 - 0.6 * math.exp(-0.3 * l)
        mods = mods_all[l]
        qk, vt, lxg, z, xbc, dt, dtt, gates = _inproj(xc, mods, cos_t, sin_t, _inproj_weights(w_in[l]))

        ya = _attention(qk, vt, lam_q[l], lam_k[l], attn_norm_g[l], lam_init)

        yb = None
        for d in range(2):
            w_gate = jnp.concatenate([_block_diag(lru_wa[l, d]), _block_diag(lru_wi[l, d])], axis=1).astype(BF16)
            yb = _rglru(lxg, lru_conv_w[l], lru_conv_b[l].reshape(1, -1), w_gate,
                        lru_ba[l, d].reshape(1, -1), lru_bi[l, d].reshape(1, -1), lru_lambda[l, d].reshape(1, -1),
                        reverse=(d == 1), yf=yb)

        ssd_p = {
            "conv_w": ssd_conv_w[l], "conv_b": ssd_conv_b[l].reshape(1, -1),
            "dtb_row": _pad_row(ssd_dt_bias[l]), "dtb_col": ssd_dt_bias[l].reshape(-1, 1),
            "alog_row": _pad_row(ssd_a_log[l]), "alog_col": ssd_a_log[l].reshape(-1, 1),
            "skip": jnp.repeat(ssd_d[l], SSD_HEADDIM).reshape(1, -1), "norm_g": ssd_norm_g[l].reshape(1, -1),
        }
        yc = None
        for d in range(2):
            ssd_p["expand"] = (jnp.arange(LANES)[:, None] == d * SSD_HEADS + head_of_channel[None, :]).astype(BF16)
            yc = _ssd(xbc, dt, dtt, ssd_p, d, yf=yc, z=z)

        x1, h2 = _merge(ya, yb, yc, gates, xc, mods, w_branch[l].astype(BF16), w_out[l].astype(BF16),
                        ln1_g[l].reshape(1, -1), ln1_b[l].reshape(1, -1), alpha)

        counts, idx, wl = _router(h2, s, w_router[l].T, router_bias[l].reshape(-1, 1))
        fr = _experts(h2, counts.reshape(nb * N_EXPERTS), idx.reshape(nb * N_EXPERTS, 1, s + MOE_BLOCK),
                      wl.reshape(nb * N_EXPERTS, s // MOE_BLOCK, MOE_BLOCK),
                      w_up[l].astype(BF16), w_down[l].astype(BF16))
        xc = _ffn_out(x1, h2, fr, mods, ws_up[l].astype(BF16), ws_down[l].astype(BF16),
                      ln2_g[l].reshape(1, -1), ln2_b[l].reshape(1, -1), alpha, latent_only=last)
    return xc
```

```python
import functools
import math

import jax
import jax.numpy as jnp
from jax import lax
from jax.experimental import pallas as pl
from jax.experimental.pallas import tpu as pltpu

F32 = jnp.float32
BF16 = jnp.bfloat16
HIGHEST = lax.Precision.HIGHEST

D_MODEL = 1024
GRID_W = 64
CTX_LEN = 256
BRANCH_W = 512
ATT_HEADS = 4
ATT_DH = 64
ATT_DV = 128
ROPE_BASE = 10000.0
LRU_C = 8.0
SSD_HEADS = 8
SSD_HEADDIM = 64
SSD_HPG = 4
SSD_GROUPS = 2
SSD_STATE = 128
N_EXPERTS = 64
N_GROUPS = 8
GROUP_SIZE = N_EXPERTS // N_GROUPS
TOP_K = 8
TOPK_GROUPS = 4
EXPERT_F = 256
ROUTED_SCALE = 2.5
LN_EPS = 1e-5
RMS_EPS = 1e-6

ROW_TILE = 256
HALO = 16
VT_ROWS = 144
KEY_CHUNK = 256
SCORE_CHUNKS = 2
MOE_BLOCK = 128
MOE_PAD = ROW_TILE
SUBLANES = 8
LANES = 128
VMEM_LIMIT = 56 * 1024 * 1024
LOG2E = 1.4426950408889634


def _cparams(sem):
    return pltpu.CompilerParams(dimension_semantics=sem, vmem_limit_bytes=VMEM_LIMIT)


def _resident(shape):
    nd = len(shape)
    return pl.BlockSpec(shape, lambda *_: (0,) * nd, pipeline_mode=pl.Buffered(1))


def _silu(x):
    return x * jax.nn.sigmoid(x)


def _softplus(x):
    return jnp.maximum(x, 0.0) + jnp.log1p(jnp.exp(-jnp.abs(x)))


def _gelu_tanh(x):
    return 0.5 * x * (1.0 + jnp.tanh(math.sqrt(2.0 / math.pi) * (x + 0.044715 * (x * x * x))))


NCHUNK = D_MODEL // LANES


def _load_token_rows(ref, n, first=0):
    return jnp.concatenate([ref[pl.ds(first * NCHUNK + s, n, stride=NCHUNK), :] for s in range(NCHUNK)], axis=1)


def _store_token_rows(ref, x):
    n = x.shape[0]
    for s in range(NCHUNK):
        ref[pl.ds(s, n, stride=NCHUNK), :] = x[:, s * LANES:(s + 1) * LANES]


def _layer_norm(x, g, b):
    mu = jnp.mean(x, axis=-1, keepdims=True)
    xc = x - mu
    var = jnp.mean(xc * xc, axis=-1, keepdims=True)
    return xc * lax.rsqrt(var + LN_EPS) * g + b


def _mod_kernel(c_ref, w_ref, b_ref, o_ref):
    s = _silu(c_ref[...])
    o_ref[0] = jnp.dot(s, w_ref[0], preferred_element_type=F32, precision=HIGHEST) + b_ref[0]


def _modulation(cc, w_mod, b_mod):
    depth = w_mod.shape[0]
    nblk = 6
    return pl.pallas_call(
        _mod_kernel,
        grid=(depth, nblk),
        in_specs=[
            pl.BlockSpec((16, D_MODEL), lambda l, j: (0, 0)),
            pl.BlockSpec((1, D_MODEL, D_MODEL), lambda l, j: (l, 0, j)),
            pl.BlockSpec((1, 1, D_MODEL), lambda l, j: (l, 0, j)),
        ],
        out_specs=pl.BlockSpec((1, 16, D_MODEL), lambda l, j: (l, 0, j)),
        out_shape=jax.ShapeDtypeStruct((depth, 16, nblk * D_MODEL), F32),
        compiler_params=_cparams(("arbitrary", "arbitrary")),
    )(cc, w_mod, b_mod.reshape(depth, 1, nblk * D_MODEL))


def _mod_spec(nb):
    return pl.BlockSpec((1, 6, D_MODEL), lambda b, i: (jnp.where(i == 0, nb, b), 0, 0))


def _inproj_kernel(x_ref, m_ref, cos_ref, sin_ref, wqk_ref, wvt_ref, ones_ref, wlxg_ref, wz_ref,
                   wxbc_ref, wdt_ref, wdtt_ref, wg_ref,
                   qk_ref, vt_ref, lxg_ref, z_ref, xbc_ref, dt_ref, dtt_ref, g_ref):
    x = x_ref[0]
    h = (x * (1.0 + m_ref[0, 1:2, :]) + m_ref[0, 0:1, :]).astype(BF16)
    nt = (((1,), (1,)), ((), ()))

    qk = jnp.dot(h, wqk_ref[...], preferred_element_type=F32)
    cos = cos_ref[...]
    sin = sin_ref[...]
    lane = lax.broadcasted_iota(jnp.int32, cos.shape, 1)
    first_half = (lane % ATT_DH) < (ATT_DH // 2)
    for j in range(2 * ATT_HEADS):
        blk = qk[:, j * LANES:(j + 1) * LANES]
        partner = jnp.where(first_half, pltpu.roll(blk, LANES - ATT_DH // 2, 1),
                            pltpu.roll(blk, ATT_DH // 2, 1))
        r = blk * cos + partner * sin
        if j < ATT_HEADS:
            r = r * (ATT_DH ** -0.5 * LOG2E)
        qk_ref[0, :, j * LANES:(j + 1) * LANES] = r.astype(BF16)

    vt = lax.dot_general(wvt_ref[...], h, nt, preferred_element_type=F32) + ones_ref[...]
    vt_ref[0] = vt.astype(BF16)
    lxg_ref[0] = jnp.dot(h, wlxg_ref[...], preferred_element_type=F32).astype(BF16)
    z_ref[0] = jnp.dot(h, wz_ref[...], preferred_element_type=F32).astype(BF16)
    xbc_ref[0] = jnp.dot(h, wxbc_ref[...], preferred_element_type=F32).astype(BF16)
    dt_ref[0] = jnp.dot(h, wdt_ref[...], preferred_element_type=F32)
    dtt_ref[0] = lax.dot_general(wdtt_ref[...], h, nt, preferred_element_type=F32)
    g_ref[0] = jnp.dot(h, wg_ref[...], preferred_element_type=F32).astype(BF16)


def _inproj(xc, mods, cos_t, sin_t, w):
    nb, s, _ = xc.shape
    nt = s // ROW_TILE
    row = lambda width: pl.BlockSpec((1, ROW_TILE, width), lambda b, i: (b, i, 0))
    col = lambda rows: pl.BlockSpec((1, rows, ROW_TILE), lambda b, i: (b, 0, i))
    vt_rows = ATT_HEADS * VT_ROWS
    outs = [
        (jax.ShapeDtypeStruct((nb, s, 2 * BRANCH_W), BF16), row(2 * BRANCH_W)),
        (jax.ShapeDtypeStruct((nb, vt_rows, s), BF16), col(vt_rows)),
        (jax.ShapeDtypeStruct((nb, s, 2 * BRANCH_W), BF16), row(2 * BRANCH_W)),
        (jax.ShapeDtypeStruct((nb, s, BRANCH_W), BF16), row(BRANCH_W)),
        (jax.ShapeDtypeStruct((nb, s, 2 * BRANCH_W), BF16), row(2 * BRANCH_W)),
        (jax.ShapeDtypeStruct((nb, s, LANES), F32), row(LANES)),
        (jax.ShapeDtypeStruct((nb, 2 * SSD_HEADS, s), F32), col(2 * SSD_HEADS)),
        (jax.ShapeDtypeStruct((nb, s, 3 * D_MODEL), BF16), row(3 * D_MODEL)),
    ]
    weights = [w["qk"], w["vt"], w["ones"], w["lxg"], w["z"], w["xbc"], w["dt"], w["dtt"], w["g"]]
    return pl.pallas_call(
        _inproj_kernel,
        grid=(nb, nt),
        in_specs=[row(D_MODEL), _mod_spec(nb),
                  pl.BlockSpec((ROW_TILE, LANES), lambda b, i: (i, 0)),
                  pl.BlockSpec((ROW_TILE, LANES), lambda b, i: (i, 0))]
                 + [_resident(a.shape) for a in weights],
        out_specs=[o[1] for o in outs],
        out_shape=[o[0] for o in outs],
        compiler_params=_cparams(("arbitrary", "arbitrary")),
    )(xc, mods, cos_t, sin_t, *weights)


def _attn_kernel(lq_ref, lk_ref, g_ref, q_ref, qn_ref, k_ref, vt_ref, o_ref, sa_ref, sb_ref, m_ref, *, lam_init):
    i = pl.program_id(2)
    tq = q_ref.shape[1]
    nck = k_ref.shape[1] // KEY_CHUNK
    prod = lq_ref[...] * lk_ref[...]
    d0 = jnp.sum(prod[0:1, :], axis=1, keepdims=True)
    d1 = jnp.sum(prod[1:2, :], axis=1, keepdims=True)
    lam = jnp.exp(d0) - jnp.exp(d1) + lam_init
    gcol = g_ref[0] * (1.0 - lam_init)

    def stack_maps(q):
        lane = lax.broadcasted_iota(jnp.int32, q.shape, 1)
        zero = jnp.zeros_like(q)
        return jnp.concatenate([jnp.where(lane < ATT_DH, q, zero), jnp.where(lane >= ATT_DH, q, zero)], axis=0)

    def scores(c, q2, s_ref, m, n=1):
        rows = slice(c * KEY_CHUNK, (c + n) * KEY_CHUNK)
        st = lax.dot_general(k_ref[0, rows, :], q2, (((1,), (1,)), ((), ())),
                             preferred_element_type=F32)
        s_ref[rows, :] = st
        mc = jnp.max(st, axis=0, keepdims=True)
        return mc if m is None else jnp.maximum(m, mc)

    def weigh(c, s_ref, m, acc):
        rows = slice(c * KEY_CHUNK, (c + 1) * KEY_CHUNK)
        e = jnp.exp2((s_ref[rows, :] - m).astype(BF16))
        pv = jnp.dot(vt_ref[0, :, rows], e, preferred_element_type=F32)
        return pv if acc is None else acc + pv

    def finish(acc):
        r = 1.0 / acc[ATT_DV:ATT_DV + 1, :]
        o = acc[:ATT_DV, :tq] * r[:, :tq] - lam * (acc[:ATT_DV, tq:] * r[:, tq:])
        ms = jnp.mean(o * o, axis=0, keepdims=True)
        o_ref[0] = (o * lax.rsqrt(ms + RMS_EPS) * gcol).T.astype(BF16)

    @pl.when(i == 0)
    def _():
        q2 = stack_maps(q_ref[0])
        finish(weigh(0, sa_ref, scores(0, q2, sa_ref, None), None))
        q2n = stack_maps(qn_ref[0])
        mn = None
        for c in range(0, nck, SCORE_CHUNKS):
            mn = scores(c, q2n, sb_ref, mn, min(SCORE_CHUNKS, nck - c))
        m_ref[1:2, :] = mn

    def step(cur_ref, nxt_ref, cur_slot, nxt_slot):
        q2n = stack_maps(qn_ref[0])
        m_cur = m_ref[cur_slot:cur_slot + 1, :]
        mn = acc = None
        for c in range(nck):
            if c % SCORE_CHUNKS == 0:
                mn = scores(c, q2n, nxt_ref, mn, min(SCORE_CHUNKS, nck - c))
            acc = weigh(c, cur_ref, m_cur, acc)
        finish(acc)
        m_ref[nxt_slot:nxt_slot + 1, :] = mn

    @pl.when(i % 2 == 1)
    def _():
        step(sb_ref, sa_ref, 1, 0)

    @pl.when(jnp.logical_and(i % 2 == 0, i > 0))
    def _():
        step(sa_ref, sb_ref, 0, 1)


def _attention(qk, vt, lam_q, lam_k, attn_g, lam_init):
    nb, s, _ = qk.shape
    nq = s // ROW_TILE
    return pl.pallas_call(
        functools.partial(_attn_kernel, lam_init=lam_init),
        grid=(nb, ATT_HEADS, nq),
        in_specs=[
            pl.BlockSpec((2, ATT_DH), lambda b, h, i: (0, 0)),
            pl.BlockSpec((2, ATT_DH), lambda b, h, i: (0, 0)),
            pl.BlockSpec((1, ATT_DV, 1), lambda b, h, i: (h, 0, 0)),
            pl.BlockSpec((1, ROW_TILE, LANES), lambda b, h, i: (b, i, h)),
            pl.BlockSpec((1, ROW_TILE, LANES), lambda b, h, i: (b, jnp.minimum(i + 1, nq - 1), h)),
            pl.BlockSpec((1, s, LANES), lambda b, h, i: (b, 0, ATT_HEADS + h)),
            pl.BlockSpec((1, VT_ROWS, s), lambda b, h, i: (b, h, 0)),
        ],
        out_specs=pl.BlockSpec((1, ROW_TILE, LANES), lambda b, h, i: (b, i, h)),
        out_shape=jax.ShapeDtypeStruct((nb, s, BRANCH_W), BF16),
        scratch_shapes=[pltpu.VMEM((s, 2 * ROW_TILE), F32), pltpu.VMEM((s, 2 * ROW_TILE), F32),
                        pltpu.VMEM((SUBLANES, 2 * ROW_TILE), F32)],
        compiler_params=_cparams(("arbitrary", "arbitrary", "arbitrary")),
    )(lam_q, lam_k, attn_g.reshape(ATT_HEADS, ATT_DV, 1), qk, qk, qk, vt)


def _scan_chunk(j, nc, reverse):
    if not reverse:
        return j
    return jnp.where(j == 0, 0, nc - j)


def _conv4(x_ref, xp_ref, xn_ref, c, nc, w_ref, b_ref):
    x = x_ref[0].astype(F32)
    t = x.shape[0]
    row = lax.broadcasted_iota(jnp.int32, (SUBLANES, x.shape[1]), 0)
    prev_ok = (c >= 2).astype(F32)
    next_ok = jnp.logical_and(c >= 1, c < nc - 1).astype(F32)
    prev = xp_ref[0].astype(F32)
    p1 = prev[HALO - 1:HALO, :] * prev_ok
    p2 = prev[HALO - 2:HALO - 1, :] * prev_ok
    n0 = xn_ref[0].astype(F32)[0:1, :] * next_ok

    def patch_head(a, fix):
        return jnp.concatenate([fix(a[:SUBLANES, :]), a[SUBLANES:, :]], axis=0)

    def patch_tail(a, fix):
        return jnp.concatenate([a[:t - SUBLANES, :], fix(a[t - SUBLANES:, :])], axis=0)

    xm1 = patch_head(pltpu.roll(x, 1, 0), lambda a: jnp.where(row == 0, p1, a))
    xm2 = patch_head(pltpu.roll(x, 2, 0), lambda a: jnp.where(row == 0, p2, jnp.where(row == 1, p1, a)))
    xp1 = patch_tail(pltpu.roll(x, t - 1, 0), lambda a: jnp.where(row == SUBLANES - 1, n0, a))
    return w_ref[0:1, :] * xm2 + w_ref[1:2, :] * xm1 + w_ref[2:3, :] * x + w_ref[3:4, :] * xp1 + b_ref[...]


def _tile_specs(width, nc, reverse, blk=0):
    per = ROW_TILE // HALO
    last = nc * per - 1
    ch = lambda j: _scan_chunk(j, nc, reverse)
    cur = pl.BlockSpec((1, ROW_TILE, width), lambda b, j: (b, ch(j), blk))
    prev = pl.BlockSpec((1, HALO, width), lambda b, j: (b, jnp.maximum(ch(j) * per - 1, 0), blk))
    nxt = pl.BlockSpec((1, HALO, width), lambda b, j: (b, jnp.minimum((ch(j) + 1) * per, last), blk))
    return cur, prev, nxt


def _rglru_kernel(*refs, reverse, nc):
    if reverse:
        (x_ref, xp_ref, xn_ref, cw_ref, cb_ref, wg_ref, ba_ref, bi_ref, lam_ref,
         yf_ref, lg_ref, o_ref, h_ref) = refs
    else:
        (x_ref, xp_ref, xn_ref, cw_ref, cb_ref, wg_ref, ba_ref, bi_ref, lam_ref, o_ref, h_ref) = refs
    j = pl.program_id(1)
    c = _scan_chunk(j, nc, reverse)

    @pl.when(j == 0)
    def _():
        h_ref[...] = jnp.zeros_like(h_ref)

    u = _conv4(x_ref, xp_ref, xn_ref, c, nc, cw_ref, cb_ref)
    t = u.shape[0]
    pre = jnp.dot(u.astype(BF16), wg_ref[...], preferred_element_type=F32)
    r = jax.nn.sigmoid(pre[:, :BRANCH_W] + ba_ref[...])
    gi = jax.nn.sigmoid(pre[:, BRANCH_W:] + bi_ref[...])
    log_a = (-LRU_C) * r * _softplus(-lam_ref[...])
    a = jnp.exp(log_a)
    var = 1.0 - jnp.exp(2.0 * log_a)
    bv = jnp.where(var > 0.0, var * lax.rsqrt(var), 0.0) * (gi * u)

    in_group = lax.broadcasted_iota(jnp.int32, u.shape, 0) % SUBLANES
    d = 1
    while d < SUBLANES:
        if reverse:
            keep = in_group < SUBLANES - d
            sh = t - d
        else:
            keep = in_group >= d
            sh = d
        a_s = jnp.where(keep, pltpu.roll(a, sh, 0), 1.0)
        b_s = jnp.where(keep, pltpu.roll(bv, sh, 0), 0.0)
        bv = a * b_s + bv
        a = a * a_s
        d *= 2
    carry = jnp.broadcast_to(h_ref[...], (SUBLANES, BRANCH_W))
    ngroup = t // SUBLANES
    pieces = [None] * ngroup
    for gidx in (range(ngroup - 1, -1, -1) if reverse else range(ngroup)):
        rows = slice(gidx * SUBLANES, (gidx + 1) * SUBLANES)
        pieces[gidx] = a[rows, :] * carry + bv[rows, :]
        edge = pieces[gidx][0:1, :] if reverse else pieces[gidx][SUBLANES - 1:SUBLANES, :]
        carry = jnp.broadcast_to(edge, (SUBLANES, BRANCH_W))
    h_ref[...] = carry[0:1, :]
    hs = jnp.concatenate(pieces, axis=0)
    if reverse:
        o_ref[0] = ((yf_ref[0] + hs) * _gelu_tanh(lg_ref[0].astype(F32))).astype(BF16)
    else:
        o_ref[0] = hs


def _rglru(lxg, conv_w, conv_b, w_gate, b_a, b_i, lam, reverse, yf=None):
    nb, s, _ = lxg.shape
    nc = s // ROW_TILE
    cur, prev, nxt = _tile_specs(BRANCH_W, nc, reverse)
    params = [conv_w, conv_b, w_gate, b_a, b_i, lam]
    in_specs = [cur, prev, nxt] + [_resident(p.shape) for p in params]
    args = [lxg, lxg, lxg] + params
    if reverse:
        in_specs += [cur, _tile_specs(BRANCH_W, nc, reverse, blk=1)[0]]
        args += [yf, lxg]
    return pl.pallas_call(
        functools.partial(_rglru_kernel, reverse=reverse, nc=nc),
        grid=(nb, nc),
        in_specs=in_specs,
        out_specs=cur,
        out_shape=jax.ShapeDtypeStruct((nb, s, BRANCH_W), BF16 if reverse else F32),
        scratch_shapes=[pltpu.VMEM((1, BRANCH_W), F32)],
        compiler_params=_cparams(("arbitrary", "arbitrary")),
    )(*args)


def _ssd_kernel(*refs, reverse, nc, direction):
    if reverse:
        (x_ref, xp_ref, xn_ref, dt_ref, dtt_ref, cw_ref, cb_ref, dtb_ref, dtbc_ref, alog_ref, alogc_ref,
         exp_ref, yf_ref, z_ref, skip_ref, g_ref, o_ref, st_ref) = refs
    else:
        (x_ref, xp_ref, xn_ref, dt_ref, dtt_ref, cw_ref, cb_ref, dtb_ref, dtbc_ref, alog_ref, alogc_ref,
         exp_ref, o_ref, st_ref) = refs
    j = pl.program_id(1)
    c = _scan_chunk(j, nc, reverse)

    @pl.when(j == 0)
    def _():
        st_ref[...] = jnp.zeros_like(st_ref)

    u = _silu(_conv4(x_ref, xp_ref, xn_ref, c, nc, cw_ref, cb_ref))
    t = u.shape[0]
    xs = u[:, :BRANCH_W]
    gw = SSD_HPG * SSD_HEADDIM

    dt_c = _softplus(dt_ref[0] + dtb_ref[...])
    a_c = dt_c * (-jnp.exp(alog_ref[...]))
    a_r = _softplus(dtt_ref[0] + dtbc_ref[...]) * (-jnp.exp(alogc_ref[...]))
    ri = lax.broadcasted_iota(jnp.int32, (t, t), 0)
    ci = lax.broadcasted_iota(jnp.int32, (t, t), 1)
    lower = (ci <= ri).astype(F32)
    upper = (ci >= ri).astype(F32)
    cs_c = jnp.dot(upper if reverse else lower, a_c, preferred_element_type=F32, precision=HIGHEST)
    cs_r = jnp.dot(a_r, lower if reverse else upper, preferred_element_type=F32, precision=HIGHEST)
    tot = cs_c[0:1, :] if reverse else cs_c[t - 1:t, :]
    keep = (ci >= ri) if reverse else (ci <= ri)

    expand = exp_ref[...]
    xdt = xs * jnp.dot(dt_c.astype(BF16), expand, preferred_element_type=F32)
    e_in = jnp.dot(jnp.exp(cs_c).astype(BF16), expand, preferred_element_type=F32)
    e_out = jnp.dot(jnp.exp(tot - cs_c).astype(BF16), expand, preferred_element_type=F32)
    e_tot = jnp.dot(jnp.broadcast_to(jnp.exp(tot), (SUBLANES, LANES)), expand.astype(F32),
                    preferred_element_type=F32, precision=HIGHEST)[0:1, :]
    xdt_b = xdt.astype(BF16)
    xdec_b = (xdt * e_out).astype(BF16)
    lane_head = lax.broadcasted_iota(jnp.int32, (t, gw), 1) // SSD_HEADDIM
    zero_b = jnp.zeros((t, gw), BF16)

    ys = []
    for g in range(SSD_GROUPS):
        bm = u[:, BRANCH_W + g * SSD_STATE:BRANCH_W + (g + 1) * SSD_STATE].astype(BF16)
        cm = u[:, BRANCH_W + (SSD_GROUPS + g) * SSD_STATE:BRANCH_W + (SSD_GROUPS + g + 1) * SSD_STATE].astype(BF16)
        cb = lax.dot_general(cm, bm, (((1,), (1,)), ((), ())), preferred_element_type=F32)
        st_g = st_ref[:, g * gw:(g + 1) * gw]
        y = jnp.dot(cm, st_g.astype(BF16), preferred_element_type=F32) * e_in[:, g * gw:(g + 1) * gw]
        xg = xdt_b[:, g * gw:(g + 1) * gw]
        for rr in range(SSD_HPG):
            col = direction * SSD_HEADS + g * SSD_HPG + rr
            decay = jnp.where(keep, jnp.exp(cs_c[:, col:col + 1] - cs_r[col:col + 1, :]), 0.0)
            gm = (cb * decay).astype(BF16)
            y = y + jnp.dot(gm, jnp.where(lane_head == rr, xg, zero_b), preferred_element_type=F32)
        ys.append(y)
        upd = lax.dot_general(bm, xdec_b[:, g * gw:(g + 1) * gw], (((0,), (0,)), ((), ())),
                              preferred_element_type=F32)
        st_ref[:, g * gw:(g + 1) * gw] = st_g * e_tot[:, g * gw:(g + 1) * gw] + upd
    y = jnp.concatenate(ys, axis=1)

    if reverse:
        y = yf_ref[0] + y + skip_ref[...] * xs
        tz = y * _silu(z_ref[0].astype(F32))
        outs = []
        for g in range(SSD_GROUPS):
            tg = tz[:, g * gw:(g + 1) * gw]
            outs.append(tg * lax.rsqrt(jnp.mean(tg * tg, axis=1, keepdims=True) + RMS_EPS))
        o_ref[0] = (jnp.concatenate(outs, axis=1) * g_ref[...]).astype(BF16)
    else:
        o_ref[0] = y


def _ssd(xbc, dt, dtt, p, direction, yf=None, z=None):
    reverse = direction == 1
    nb, s, _ = xbc.shape
    nc = s // ROW_TILE
    cur, prev, nxt = _tile_specs(2 * BRANCH_W, nc, reverse)
    ch = lambda j: _scan_chunk(j, nc, reverse)
    params = [p["conv_w"], p["conv_b"], p["dtb_row"], p["dtb_col"], p["alog_row"], p["alog_col"], p["expand"]]
    in_specs = [cur, prev, nxt,
                pl.BlockSpec((1, ROW_TILE, LANES), lambda b, j: (b, ch(j), 0)),
                pl.BlockSpec((1, 2 * SSD_HEADS, ROW_TILE), lambda b, j: (b, 0, ch(j)))]
    in_specs += [_resident(a.shape) for a in params]
    args = [xbc, xbc, xbc, dt, dtt] + params
    half = pl.BlockSpec((1, ROW_TILE, BRANCH_W), lambda b, j: (b, ch(j), 0))
    if reverse:
        in_specs += [half, half, _resident(p["skip"].shape), _resident(p["norm_g"].shape)]
        args += [yf, z, p["skip"], p["norm_g"]]
    return pl.pallas_call(
        functools.partial(_ssd_kernel, reverse=reverse, nc=nc, direction=direction),
        grid=(nb, nc),
        in_specs=in_specs,
        out_specs=half,
        out_shape=jax.ShapeDtypeStruct((nb, s, BRANCH_W), BF16 if reverse else F32),
        scratch_shapes=[pltpu.VMEM((SSD_STATE, BRANCH_W), F32)],
        compiler_params=_cparams(("arbitrary", "arbitrary")),
    )(*args)


def _merge_kernel(ya_ref, yb_ref, yc_ref, gt_ref, x_ref, m_ref, wb_ref, wo_ref, g_ref, b_ref,
                  x1_ref, h2_ref, *, alpha):
    acc = None
    for n, y_ref in enumerate((ya_ref, yb_ref, yc_ref)):
        proj = jnp.dot(y_ref[0], wb_ref[n], preferred_element_type=F32)
        gate = jax.nn.sigmoid(gt_ref[0, :, n * D_MODEL:(n + 1) * D_MODEL].astype(F32))
        acc = gate * proj if acc is None else acc + gate * proj
    mix = jnp.dot(acc.astype(BF16), wo_ref[...], preferred_element_type=F32)
    x1 = _layer_norm(alpha * x_ref[0] + m_ref[0, 2:3, :] * mix, g_ref[...], b_ref[...])
    x1_ref[0] = x1
    _store_token_rows(h2_ref.at[0], x1 * (1.0 + m_ref[0, 4:5, :]) + m_ref[0, 3:4, :])


def _zero_tile_kernel(h_ref, o_ref):
    del h_ref
    o_ref[...] = jnp.zeros_like(o_ref)


def _merge(ya, yb, yc, gates, xc, mods, w_branch, w_out, ln_g, ln_b, alpha):
    nb, s, _ = xc.shape
    nt = s // ROW_TILE
    row = lambda width: pl.BlockSpec((1, ROW_TILE, width), lambda b, i: (b, i, 0))
    h2_shape = jax.ShapeDtypeStruct((nb, (s + MOE_PAD) * NCHUNK, LANES), F32)
    x1, h2 = pl.pallas_call(
        functools.partial(_merge_kernel, alpha=alpha),
        grid=(nb, nt),
        in_specs=[row(BRANCH_W), row(BRANCH_W), row(BRANCH_W), row(3 * D_MODEL), row(D_MODEL), _mod_spec(nb),
                  _resident(w_branch.shape), _resident(w_out.shape), _resident(ln_g.shape), _resident(ln_b.shape)],
        out_specs=[row(D_MODEL), pl.BlockSpec((1, ROW_TILE * NCHUNK, LANES), lambda b, i: (b, i, 0))],
        out_shape=[jax.ShapeDtypeStruct((nb, s, D_MODEL), F32), h2_shape],
        compiler_params=_cparams(("arbitrary", "arbitrary")),
    )(ya, yb, yc, gates, xc, mods, w_branch, w_out, ln_g, ln_b)
    h2 = pl.pallas_call(
        _zero_tile_kernel,
        grid=(nb,),
        in_specs=[pl.BlockSpec(memory_space=pl.ANY)],
        out_specs=pl.BlockSpec((1, NCHUNK, LANES), lambda b: (b, s, 0)),
        out_shape=h2_shape,
        input_output_aliases={0: 0},
        compiler_params=_cparams(("arbitrary",)),
    )(h2)
    return x1, h2


def _first_index(hit, ridx, n):
    return jnp.min(jnp.where(hit, ridx, n), axis=0, keepdims=True)


def _route_kernel(h_ref, wr_ref, bias_ref, cnt_ref, rank_ref, wgt_ref, base_ref, *, latent_only):
    tm = wgt_ref.shape[2]

    @pl.when(pl.program_id(1) == 0)
    def _():
        base_ref[...] = jnp.zeros_like(base_ref)

    logits = lax.dot_general(wr_ref[...], _load_token_rows(h_ref.at[0], tm), (((1,), (1,)), ((), ())),
                             preferred_element_type=F32, precision=HIGHEST)
    scores = jax.nn.sigmoid(logits)
    sel = scores + bias_ref[...]
    neg = -jnp.inf

    r8 = lax.broadcasted_iota(jnp.int32, (GROUP_SIZE, tm), 0)
    grp = jnp.zeros((N_GROUPS, tm), F32)
    for g in range(N_GROUPS):
        blk = sel[g * GROUP_SIZE:(g + 1) * GROUP_SIZE, :]
        m1 = jnp.max(blk, axis=0, keepdims=True)
        first = _first_index(blk == m1, r8, GROUP_SIZE)
        m2 = jnp.max(jnp.where(r8 == first, neg, blk), axis=0, keepdims=True)
        grp = jnp.where(r8 == g, m1 + m2, grp)
    gsel = jnp.zeros((N_GROUPS, tm), jnp.int32)
    for _ in range(TOPK_GROUPS):
        m = jnp.max(grp, axis=0, keepdims=True)
        hit = r8 == _first_index(grp == m, r8, N_GROUPS)
        gsel = jnp.where(hit, 1, gsel)
        grp = jnp.where(hit, neg, grp)
    cand = jnp.concatenate(
        [jnp.where(gsel[g:g + 1, :] > 0, sel[g * GROUP_SIZE:(g + 1) * GROUP_SIZE, :], neg) for g in range(N_GROUPS)],
        axis=0)

    re = lax.broadcasted_iota(jnp.int32, (N_EXPERTS, tm), 0)
    chosen = jnp.zeros((N_EXPERTS, tm), jnp.int32)
    for _ in range(TOP_K):
        m = jnp.max(cand, axis=0, keepdims=True)
        hit = re == _first_index(cand == m, re, N_EXPERTS)
        chosen = jnp.where(hit, 1, chosen)
        cand = jnp.where(hit, neg, cand)
    picked = jnp.where(chosen > 0, scores, 0.0)
    wgt = picked / jnp.sum(picked, axis=0, keepdims=True) * ROUTED_SCALE
    if latent_only:
        chosen = jnp.where(pl.program_id(1) == 0, 0, chosen)

    li = lax.broadcasted_iota(jnp.int32, (LANES, LANES), 0)
    lj = lax.broadcasted_iota(jnp.int32, (LANES, LANES), 1)
    strict = (li < lj).astype(BF16)
    chosen_b = chosen.astype(F32).astype(BF16)
    base = base_ref[...]
    pos = []
    for kb in range(tm // LANES):
        cblk = chosen_b[:, kb * LANES:(kb + 1) * LANES]
        pos.append(jnp.dot(cblk, strict, preferred_element_type=F32) + base)
        base = base + jnp.sum(cblk.astype(F32), axis=1, keepdims=True)
    base_ref[...] = base
    cnt_ref[0] = base.astype(jnp.int32)
    rank_ref[0] = jnp.where(chosen > 0, jnp.concatenate(pos, axis=1).astype(jnp.int32), -1)
    wgt_ref[0] = wgt


def _compact_kernel(cnt_ref, rank_ref, wgt_ref, idx_ref, wl_ref):
    rank = rank_ref[0]
    ne, tm = rank.shape
    lane = lax.broadcasted_iota(jnp.int32, (ne, tm), 1)
    live = jnp.where(rank >= 0, 1, 0)
    dist = lane - rank
    val = lane
    wv = wgt_ref[0]
    step = 1
    while step < tm:
        move = jnp.where(jnp.logical_and(live > 0, (dist & step) != 0), 1, 0)
        sh = tm - step
        take = pltpu.roll(move, sh, 1) > 0
        val = jnp.where(take, pltpu.roll(val, sh, 1), val)
        wv = jnp.where(take, pltpu.roll(wv, sh, 1), wv)
        dist = jnp.where(take, pltpu.roll(dist, sh, 1), dist)
        live = jnp.where(take, 1, jnp.where(move > 0, 0, live))
        step *= 2
    valid = lane < cnt_ref[0]
    spare = jnp.full((ne, MOE_BLOCK), tm * NCHUNK, jnp.int32)
    idx_ref[0] = jnp.concatenate([jnp.where(valid, val * NCHUNK, tm * NCHUNK), spare], axis=1)
    wl_ref[0] = jnp.where(valid, wv, 0.0)


def _router(h2, s, w_router_t, bias_col, latent_only):
    nb = h2.shape[0]
    dense = pl.BlockSpec((1, N_EXPERTS, ROW_TILE), lambda b, i: (b, 0, i))
    counts, rank, wgt = pl.pallas_call(
        functools.partial(_route_kernel, latent_only=latent_only),
        grid=(nb, s // ROW_TILE),
        in_specs=[pl.BlockSpec((1, ROW_TILE * NCHUNK, LANES), lambda b, i: (b, i, 0)),
                  _resident(w_router_t.shape), _resident(bias_col.shape)],
        out_specs=[pl.BlockSpec((1, N_EXPERTS, 1), lambda b, i: (b, 0, 0)), dense, dense],
        out_shape=[jax.ShapeDtypeStruct((nb, N_EXPERTS, 1), jnp.int32),
                   jax.ShapeDtypeStruct((nb, N_EXPERTS, s), jnp.int32),
                   jax.ShapeDtypeStruct((nb, N_EXPERTS, s), F32)],
        scratch_shapes=[pltpu.VMEM((N_EXPERTS, 1), F32)],
        compiler_params=_cparams(("arbitrary", "arbitrary")),
    )(h2, w_router_t, bias_col)
    rows = lambda width: pl.BlockSpec((1, SUBLANES, width), lambda b, g: (b, g, 0))
    idx, wl = pl.pallas_call(
        _compact_kernel,
        grid=(nb, N_EXPERTS // SUBLANES),
        in_specs=[rows(1), rows(s), rows(s)],
        out_specs=[rows(s + MOE_BLOCK), rows(s)],
        out_shape=[jax.ShapeDtypeStruct((nb, N_EXPERTS, s + MOE_BLOCK), jnp.int32),
                   jax.ShapeDtypeStruct((nb, N_EXPERTS, s), F32)],
        compiler_params=_cparams(("arbitrary", "arbitrary")),
    )(counts, rank, wgt)
    return counts, idx, wl


SCATTER_GROUP = 16


def _experts_kernel(cnt_ref, idx_ref, wl_ref, h_ref, wu_ref, wd_ref, o_ref, xa_ref, xb_ref, ya_ref, yb_ref):
    t = pl.program_id(0)
    e = pl.program_id(1)
    h_rows = h_ref.at[0]
    acc_ref = o_ref.at[0]

    @pl.when(e == 0)
    def _():
        o_ref[...] = jnp.zeros_like(o_ref)

    count = cnt_ref[t * N_EXPERTS + e]
    nblk = (count + MOE_BLOCK - 1) // MOE_BLOCK
    ri = lax.broadcasted_iota(jnp.int32, (MOE_BLOCK, MOE_BLOCK), 0)
    ci = lax.broadcasted_iota(jnp.int32, (MOE_BLOCK, MOE_BLOCK), 1)

    def gather(blk, x_ref):
        ids = idx_ref.at[0, 0, pl.ds(blk * MOE_BLOCK, MOE_BLOCK)]
        for r in range(MOE_BLOCK):
            x_ref[pl.ds(r * NCHUNK, NCHUNK), :] = h_rows[pl.ds(pl.multiple_of(ids[r], NCHUNK), NCHUNK), :]

    def ffn(blk, x_ref, y_ref):
        x = _load_token_rows(x_ref, MOE_BLOCK)
        gu = jnp.dot(x.astype(BF16), wu_ref[0], preferred_element_type=F32)
        act = _silu(gu[:, :EXPERT_F]) * gu[:, EXPERT_F:]
        y = jnp.dot(act.astype(BF16), wd_ref[0], preferred_element_type=F32)
        w_row = wl_ref[0, pl.ds(blk, 1), :]
        w_col = jnp.sum(jnp.where(ri == ci, w_row, 0.0), axis=1, keepdims=True)
        _store_token_rows(y_ref, y * w_col)

    def scatter(blk, y_ref):
        ids = idx_ref.at[0, 0, pl.ds(blk * MOE_BLOCK, MOE_BLOCK)]
        for g0 in range(0, MOE_BLOCK, SCATTER_GROUP):
            rows = range(g0, g0 + SCATTER_GROUP)
            dst = [pl.multiple_of(ids[r], NCHUNK) for r in rows]
            new = [acc_ref[pl.ds(d, NCHUNK), :] + y_ref[pl.ds(r * NCHUNK, NCHUNK), :] for d, r in zip(dst, rows)]
            for d, v in zip(dst, new):
                acc_ref[pl.ds(d, NCHUNK), :] = v

    @pl.when(nblk > 0)
    def _():
        gather(0, xa_ref)

    def pair(j, carry):
        b0 = 2 * j
        ffn(b0, xa_ref, ya_ref)
        gather(b0 + 1, xb_ref)
        ffn(b0 + 1, xb_ref, yb_ref)
        scatter(b0, ya_ref)
        gather(b0 + 2, xa_ref)
        scatter(b0 + 1, yb_ref)
        return carry

    lax.fori_loop(0, nblk // 2, pair, 0)

    @pl.when(nblk % 2 == 1)
    def _():
        ffn(nblk - 1, xa_ref, ya_ref)
        scatter(nblk - 1, ya_ref)


def _experts(h2, counts, idx, wl, w_up, w_down):
    nb, rows, _ = h2.shape
    nlist = idx.shape[-1]
    tile = pl.BlockSpec((1, rows, LANES), lambda t, e, c: (t, 0, 0), pipeline_mode=pl.Buffered(1))
    buf = pltpu.VMEM((MOE_BLOCK * NCHUNK, LANES), F32)
    grid_spec = pltpu.PrefetchScalarGridSpec(
        num_scalar_prefetch=1,
        grid=(nb, N_EXPERTS),
        in_specs=[pl.BlockSpec((1, 1, nlist), lambda t, e, c: (t * N_EXPERTS + e, 0, 0), memory_space=pltpu.SMEM),
                  pl.BlockSpec((1,) + wl.shape[1:], lambda t, e, c: (t * N_EXPERTS + e, 0, 0)),
                  tile,
                  pl.BlockSpec((1, D_MODEL, 2 * EXPERT_F), lambda t, e, c: (e, 0, 0)),
                  pl.BlockSpec((1, EXPERT_F, D_MODEL), lambda t, e, c: (e, 0, 0))],
        out_specs=tile,
        scratch_shapes=[buf, buf, buf, buf],
    )
    return pl.pallas_call(
        _experts_kernel,
        grid_spec=grid_spec,
        out_shape=jax.ShapeDtypeStruct(h2.shape, F32),
        compiler_params=_cparams(("arbitrary", "arbitrary")),
    )(counts, idx, wl, h2, w_up, w_down)


def _ffn_out_kernel(x1_ref, h2_ref, fr_ref, m_ref, wu_ref, wd_ref, g_ref, b_ref, o_ref, *, alpha):
    gu = jnp.dot(_load_token_rows(h2_ref.at[0], ROW_TILE).astype(BF16), wu_ref[...], preferred_element_type=F32)
    act = _silu(gu[:, :EXPERT_F]) * gu[:, EXPERT_F:]
    f = (jnp.dot(act.astype(BF16), wd_ref[...], preferred_element_type=F32)
         + _load_token_rows(fr_ref.at[0], ROW_TILE))
    o_ref[0] = _layer_norm(alpha * x1_ref[0] + m_ref[0, 5:6, :] * f, g_ref[...], b_ref[...])


def _ffn_out(x1, h2, fr, mods, ws_up, ws_down, ln_g, ln_b, alpha, latent_only):
    nb, s, _ = x1.shape
    nt = s // ROW_TILE
    skip = CTX_LEN // ROW_TILE if latent_only else 0
    row = pl.BlockSpec((1, ROW_TILE, D_MODEL), lambda b, i: (b, i + skip, 0))
    mod = pl.BlockSpec((1, 6, D_MODEL), lambda b, i: (jnp.where(i + skip == 0, nb, b), 0, 0))
    chunked = pl.BlockSpec((1, ROW_TILE * NCHUNK, LANES), lambda b, i: (b, i + skip, 0))
    return pl.pallas_call(
        functools.partial(_ffn_out_kernel, alpha=alpha),
        grid=(nb, nt - skip),
        in_specs=[row, chunked, chunked, mod, _resident(ws_up.shape), _resident(ws_down.shape),
                  _resident(ln_g.shape), _resident(ln_b.shape)],
        out_specs=pl.BlockSpec((1, ROW_TILE, D_MODEL), lambda b, i: (b, i, 0)),
        out_shape=jax.ShapeDtypeStruct((nb, s - skip * ROW_TILE, D_MODEL), F32),
        compiler_params=_cparams(("arbitrary", "arbitrary")),
    )(x1, h2, fr, mods, ws_up, ws_down, ln_g, ln_b)


def _rope_tables(n_lat):
    t = jnp.arange(n_lat)
    rowp = (t // GRID_W).astype(F32)
    colp = (t % GRID_W).astype(F32)
    n_freq = ATT_DH // 4
    inv = ROPE_BASE ** (-jnp.arange(n_freq, dtype=F32) / n_freq)
    ang = jnp.concatenate([rowp[:, None] * inv, colp[:, None] * inv], axis=-1)
    lane = jnp.arange(LANES)
    cos = jnp.cos(ang)[:, lane % (ATT_DH // 2)]
    sign = jnp.where((lane % ATT_DH) < ATT_DH // 2, -1.0, 1.0).astype(F32)
    sin = jnp.sin(ang)[:, lane % (ATT_DH // 2)] * sign
    cos = jnp.concatenate([jnp.ones((CTX_LEN, LANES), F32), cos], axis=0)
    sin = jnp.concatenate([jnp.zeros((CTX_LEN, LANES), F32), sin], axis=0)
    return cos, sin


def _block_diag(w):
    n, k, _ = w.shape
    eye = jnp.eye(n, dtype=w.dtype)
    return (eye[:, None, :, None] * w[:, :, None, :]).reshape(n * k, n * k)


def _inproj_weights(w):
    o_v = 2 * BRANCH_W
    o_lx = o_v + ATT_HEADS * ATT_DV
    o_z = o_lx + 2 * BRANCH_W
    o_xbc = o_z + BRANCH_W
    o_dt = o_xbc + 2 * BRANCH_W
    o_g = o_dt + 2 * SSD_HEADS
    wv = w[:, o_v:o_lx].T.reshape(ATT_HEADS, ATT_DV, D_MODEL)
    wv = jnp.pad(wv, ((0, 0), (0, VT_ROWS - ATT_DV), (0, 0))).reshape(ATT_HEADS * VT_ROWS, D_MODEL)
    ones = jnp.zeros((ATT_HEADS, VT_ROWS, 1), F32).at[:, ATT_DV, 0].set(1.0).reshape(ATT_HEADS * VT_ROWS, 1)
    wdt = w[:, o_dt:o_g]
    return {
        "qk": w[:, :o_v].astype(BF16),
        "vt": wv.astype(BF16),
        "ones": ones,
        "lxg": w[:, o_lx:o_z].astype(BF16),
        "z": w[:, o_z:o_xbc].astype(BF16),
        "xbc": w[:, o_xbc:o_dt].astype(BF16),
        "dt": jnp.pad(wdt, ((0, 0), (0, LANES - 2 * SSD_HEADS))).astype(BF16),
        "dtt": wdt.T.astype(BF16),
        "g": w[:, o_g:].astype(BF16),
    }


def _pad_row(v):
    return jnp.pad(v.reshape(1, -1), ((0, 0), (0, LANES - v.size)))


def kernel(x, c, ctx, c_ctx, w_mod, b_mod, w_in, lam_q, lam_k, attn_norm_g, lru_conv_w, lru_conv_b, lru_wa, lru_ba, lru_wi, lru_bi, lru_lambda, ssd_conv_w, ssd_conv_b, ssd_dt_bias, ssd_a_log, ssd_d, ssd_norm_g, w_branch, w_out, ln1_g, ln1_b, w_router, router_bias, w_up, w_down, ws_up, ws_down, ln2_g, ln2_b):
    nb, n_lat, _ = x.shape
    depth = w_mod.shape[0]
    assert ctx.shape[1] == CTX_LEN and n_lat % ROW_TILE == 0 and nb + 1 <= 16
    s = CTX_LEN + n_lat
    alpha = (2 * depth) ** 0.25

    xc = jnp.concatenate([ctx, x], axis=1)
    cc = jnp.zeros((16, D_MODEL), F32).at[:nb].set(c).at[nb].set(c_ctx)
    mods_all = _modulation(cc, w_mod, b_mod).reshape(depth, 16, 6, D_MODEL)
    cos_t, sin_t = _rope_tables(n_lat)
    head_of_channel = jnp.arange(BRANCH_W) // SSD_HEADDIM

    for l in range(depth):
        last = l == depth - 1
        lam_init = 0.8 - 0.6 * math.exp(-0.3 * l)
        mods = mods_all[l]
        qk, vt, lxg, z, xbc, dt, dtt, gates = _inproj(xc, mods, cos_t, sin_t, _inproj_weights(w_in[l]))

        ya = _attention(qk, vt, lam_q[l], lam_k[l], attn_norm_g[l], lam_init)

        yb = None
        for d in range(2):
            w_gate = jnp.concatenate([_block_diag(lru_wa[l, d]), _block_diag(lru_wi[l, d])], axis=1).astype(BF16)
            yb = _rglru(lxg, lru_conv_w[l], lru_conv_b[l].reshape(1, -1), w_gate,
                        lru_ba[l, d].reshape(1, -1), lru_bi[l, d].reshape(1, -1), lru_lambda[l, d].reshape(1, -1),
                        reverse=(d == 1), yf=yb)

        ssd_p = {
            "conv_w": ssd_conv_w[l], "conv_b": ssd_conv_b[l].reshape(1, -1),
            "dtb_row": _pad_row(ssd_dt_bias[l]), "dtb_col": ssd_dt_bias[l].reshape(-1, 1),
            "alog_row": _pad_row(ssd_a_log[l]), "alog_col": ssd_a_log[l].reshape(-1, 1),
            "skip": jnp.repeat(ssd_d[l], SSD_HEADDIM).reshape(1, -1), "norm_g": ssd_norm_g[l].reshape(1, -1),
        }
        yc = None
        for d in range(2):
            ssd_p["expand"] = (jnp.arange(LANES)[:, None] == d * SSD_HEADS + head_of_channel[None, :]).astype(BF16)
            yc = _ssd(xbc, dt, dtt, ssd_p, d, yf=yc, z=z)

        x1, h2 = _merge(ya, yb, yc, gates, xc, mods, w_branch[l].astype(BF16), w_out[l].astype(BF16),
                        ln1_g[l].reshape(1, -1), ln1_b[l].reshape(1, -1), alpha)

        counts, idx, wl = _router(h2, s, w_router[l].T, router_bias[l].reshape(-1, 1), latent_only=last)
        fr = _experts(h2, counts.reshape(nb * N_EXPERTS), idx.reshape(nb * N_EXPERTS, 1, s + MOE_BLOCK),
                      wl.reshape(nb * N_EXPERTS, s // MOE_BLOCK, MOE_BLOCK),
                      w_up[l].astype(BF16), w_down[l].astype(BF16))
        xc = _ffn_out(x1, h2, fr, mods, ws_up[l].astype(BF16), ws_down[l].astype(BF16),
                      ln2_g[l].reshape(1, -1), ln2_b[l].reshape(1, -1), alpha, latent_only=last)
    return xc
```

```python
import functools
import math

import jax
import jax.numpy as jnp
from jax import lax
from jax.experimental import pallas as pl
from jax.experimental.pallas import tpu as pltpu

F32 = jnp.float32
BF16 = jnp.bfloat16
HIGHEST = lax.Precision.HIGHEST

D_MODEL = 1024
GRID_W = 64
CTX_LEN = 256
BRANCH_W = 512
ATT_HEADS = 4
ATT_DH = 64
ATT_DV = 128
ROPE_BASE = 10000.0
LRU_C = 8.0
SSD_HEADS = 8
SSD_HEADDIM = 64
SSD_HPG = 4
SSD_GROUPS = 2
SSD_STATE = 128
N_EXPERTS = 64
N_GROUPS = 8
GROUP_SIZE = N_EXPERTS // N_GROUPS
TOP_K = 8
TOPK_GROUPS = 4
EXPERT_F = 256
ROUTED_SCALE = 2.5
LN_EPS = 1e-5
RMS_EPS = 1e-6

ROW_TILE = 256
HALO = 16
VT_ROWS = 144
KEY_CHUNK = 256
SCORE_CHUNKS = 4
MOE_BLOCK = 128
MOE_PAD = ROW_TILE
MOE_SPARE = 2 * MOE_BLOCK
SUBLANES = 8
LANES = 128
VMEM_LIMIT = 56 * 1024 * 1024
LOG2E = 1.4426950408889634


def _cparams(sem):
    return pltpu.CompilerParams(dimension_semantics=sem, vmem_limit_bytes=VMEM_LIMIT)


def _resident(shape):
    nd = len(shape)
    return pl.BlockSpec(shape, lambda *_: (0,) * nd, pipeline_mode=pl.Buffered(1))


def _silu(x):
    return x * jax.nn.sigmoid(x)


def _softplus(x):
    return jnp.maximum(x, 0.0) + jnp.log1p(jnp.exp(-jnp.abs(x)))


def _gelu_tanh(x):
    return 0.5 * x * (1.0 + jnp.tanh(math.sqrt(2.0 / math.pi) * (x + 0.044715 * (x * x * x))))


NCHUNK = D_MODEL // LANES


def _load_token_rows(ref, n, first=0):
    return jnp.concatenate([ref[pl.ds(first * NCHUNK + s, n, stride=NCHUNK), :] for s in range(NCHUNK)], axis=1)


def _store_token_rows(ref, x):
    n = x.shape[0]
    for s in range(NCHUNK):
        ref[pl.ds(s, n, stride=NCHUNK), :] = x[:, s * LANES:(s + 1) * LANES]


def _layer_norm(x, g, b):
    mu = jnp.mean(x, axis=-1, keepdims=True)
    xc = x - mu
    var = jnp.mean(xc * xc, axis=-1, keepdims=True)
    return xc * lax.rsqrt(var + LN_EPS) * g + b


def _mod_kernel(c_ref, w_ref, b_ref, o_ref):
    s = _silu(c_ref[...])
    o_ref[0] = jnp.dot(s, w_ref[0], preferred_element_type=F32, precision=HIGHEST) + b_ref[0]


def _modulation(cc, w_mod, b_mod):
    depth = w_mod.shape[0]
    nblk = 6
    return pl.pallas_call(
        _mod_kernel,
        grid=(depth, nblk),
        in_specs=[
            pl.BlockSpec((16, D_MODEL), lambda l, j: (0, 0)),
            pl.BlockSpec((1, D_MODEL, D_MODEL), lambda l, j: (l, 0, j)),
            pl.BlockSpec((1, 1, D_MODEL), lambda l, j: (l, 0, j)),
        ],
        out_specs=pl.BlockSpec((1, 16, D_MODEL), lambda l, j: (l, 0, j)),
        out_shape=jax.ShapeDtypeStruct((depth, 16, nblk * D_MODEL), F32),
        compiler_params=_cparams(("arbitrary", "arbitrary")),
    )(cc, w_mod, b_mod.reshape(depth, 1, nblk * D_MODEL))


def _mod_spec(nb):
    return pl.BlockSpec((1, 6, D_MODEL), lambda b, i: (jnp.where(i == 0, nb, b), 0, 0))


def _inproj_kernel(x_ref, m_ref, cos_ref, sin_ref, wqk_ref, wvt_ref, ones_ref, wlxg_ref, wz_ref,
                   wxbc_ref, wdt_ref, wdtt_ref, wg_ref,
                   qk_ref, vt_ref, lxg_ref, z_ref, xbc_ref, dt_ref, dtt_ref, g_ref):
    x = x_ref[0]
    h = (x * (1.0 + m_ref[0, 1:2, :]) + m_ref[0, 0:1, :]).astype(BF16)
    nt = (((1,), (1,)), ((), ()))

    qk = jnp.dot(h, wqk_ref[...], preferred_element_type=F32)
    cos = cos_ref[...]
    sin = sin_ref[...]
    lane = lax.broadcasted_iota(jnp.int32, cos.shape, 1)
    first_half = (lane % ATT_DH) < (ATT_DH // 2)
    for j in range(2 * ATT_HEADS):
        blk = qk[:, j * LANES:(j + 1) * LANES]
        partner = jnp.where(first_half, pltpu.roll(blk, LANES - ATT_DH // 2, 1),
                            pltpu.roll(blk, ATT_DH // 2, 1))
        r = blk * cos + partner * sin
        if j < ATT_HEADS:
            r = r * (ATT_DH ** -0.5 * LOG2E)
        qk_ref[0, :, j * LANES:(j + 1) * LANES] = r.astype(BF16)

    vt = lax.dot_general(wvt_ref[...], h, nt, preferred_element_type=F32) + ones_ref[...]
    vt_ref[0] = vt.astype(BF16)
    lxg_ref[0] = jnp.dot(h, wlxg_ref[...], preferred_element_type=F32).astype(BF16)
    z_ref[0] = jnp.dot(h, wz_ref[...], preferred_element_type=F32).astype(BF16)
    xbc_ref[0] = jnp.dot(h, wxbc_ref[...], preferred_element_type=F32).astype(BF16)
    dt_ref[0] = jnp.dot(h, wdt_ref[...], preferred_element_type=F32)
    dtt_ref[0] = lax.dot_general(wdtt_ref[...], h, nt, preferred_element_type=F32)
    g_ref[0] = jnp.dot(h, wg_ref[...], preferred_element_type=F32).astype(BF16)


def _inproj(xc, mods, cos_t, sin_t, w):
    nb, s, _ = xc.shape
    nt = s // ROW_TILE
    row = lambda width: pl.BlockSpec((1, ROW_TILE, width), lambda b, i: (b, i, 0))
    col = lambda rows: pl.BlockSpec((1, rows, ROW_TILE), lambda b, i: (b, 0, i))
    vt_rows = ATT_HEADS * VT_ROWS
    outs = [
        (jax.ShapeDtypeStruct((nb, s, 2 * BRANCH_W), BF16), row(2 * BRANCH_W)),
        (jax.ShapeDtypeStruct((nb, vt_rows, s), BF16), col(vt_rows)),
        (jax.ShapeDtypeStruct((nb, s, 2 * BRANCH_W), BF16), row(2 * BRANCH_W)),
        (jax.ShapeDtypeStruct((nb, s, BRANCH_W), BF16), row(BRANCH_W)),
        (jax.ShapeDtypeStruct((nb, s, 2 * BRANCH_W), BF16), row(2 * BRANCH_W)),
        (jax.ShapeDtypeStruct((nb, s, LANES), F32), row(LANES)),
        (jax.ShapeDtypeStruct((nb, 2 * SSD_HEADS, s), F32), col(2 * SSD_HEADS)),
        (jax.ShapeDtypeStruct((nb, s, 3 * D_MODEL), BF16), row(3 * D_MODEL)),
    ]
    weights = [w["qk"], w["vt"], w["ones"], w["lxg"], w["z"], w["xbc"], w["dt"], w["dtt"], w["g"]]
    return pl.pallas_call(
        _inproj_kernel,
        grid=(nb, nt),
        in_specs=[row(D_MODEL), _mod_spec(nb),
                  pl.BlockSpec((ROW_TILE, LANES), lambda b, i: (i, 0)),
                  pl.BlockSpec((ROW_TILE, LANES), lambda b, i: (i, 0))]
                 + [_resident(a.shape) for a in weights],
        out_specs=[o[1] for o in outs],
        out_shape=[o[0] for o in outs],
        compiler_params=_cparams(("arbitrary", "arbitrary")),
    )(xc, mods, cos_t, sin_t, *weights)


def _attn_kernel(lq_ref, lk_ref, g_ref, q_ref, qn_ref, k_ref, vt_ref, o_ref, sa_ref, sb_ref, m_ref, *, lam_init):
    i = pl.program_id(2)
    tq = q_ref.shape[1]
    nck = k_ref.shape[1] // KEY_CHUNK
    prod = lq_ref[...] * lk_ref[...]
    d0 = jnp.sum(prod[0:1, :], axis=1, keepdims=True)
    d1 = jnp.sum(prod[1:2, :], axis=1, keepdims=True)
    lam = jnp.exp(d0) - jnp.exp(d1) + lam_init
    gcol = g_ref[0] * (1.0 - lam_init)

    def stack_maps(q):
        lane = lax.broadcasted_iota(jnp.int32, q.shape, 1)
        zero = jnp.zeros_like(q)
        return jnp.concatenate([jnp.where(lane < ATT_DH, q, zero), jnp.where(lane >= ATT_DH, q, zero)], axis=0)

    def scores(c, q2, s_ref, m, n=1):
        rows = slice(c * KEY_CHUNK, (c + n) * KEY_CHUNK)
        st = lax.dot_general(k_ref[0, rows, :], q2, (((1,), (1,)), ((), ())),
                             preferred_element_type=F32)
        s_ref[rows, :] = st
        mc = jnp.max(st, axis=0, keepdims=True)
        return mc if m is None else jnp.maximum(m, mc)

    def weigh(c, s_ref, m, acc):
        rows = slice(c * KEY_CHUNK, (c + 1) * KEY_CHUNK)
        e = jnp.exp2((s_ref[rows, :] - m).astype(BF16))
        pv = jnp.dot(vt_ref[0, :, rows], e, preferred_element_type=F32)
        return pv if acc is None else acc + pv

    def finish(acc):
        r = 1.0 / acc[ATT_DV:ATT_DV + 1, :]
        o = acc[:ATT_DV, :tq] * r[:, :tq] - lam * (acc[:ATT_DV, tq:] * r[:, tq:])
        ms = jnp.mean(o * o, axis=0, keepdims=True)
        o_ref[0] = (o * lax.rsqrt(ms + RMS_EPS) * gcol).T.astype(BF16)

    @pl.when(i == 0)
    def _():
        q2 = stack_maps(q_ref[0])
        finish(weigh(0, sa_ref, scores(0, q2, sa_ref, None), None))
        q2n = stack_maps(qn_ref[0])
        mn = None
        for c in range(0, nck, SCORE_CHUNKS):
            mn = scores(c, q2n, sb_ref, mn, min(SCORE_CHUNKS, nck - c))
        m_ref[1:2, :] = mn

    def step(cur_ref, nxt_ref, cur_slot, nxt_slot):
        q2n = stack_maps(qn_ref[0])
        m_cur = m_ref[cur_slot:cur_slot + 1, :]
        mn = acc = None
        for c in range(nck):
            if c % SCORE_CHUNKS == 0:
                mn = scores(c, q2n, nxt_ref, mn, min(SCORE_CHUNKS, nck - c))
            acc = weigh(c, cur_ref, m_cur, acc)
        finish(acc)
        m_ref[nxt_slot:nxt_slot + 1, :] = mn

    @pl.when(i % 2 == 1)
    def _():
        step(sb_ref, sa_ref, 1, 0)

    @pl.when(jnp.logical_and(i % 2 == 0, i > 0))
    def _():
        step(sa_ref, sb_ref, 0, 1)


def _attention(qk, vt, lam_q, lam_k, attn_g, lam_init):
    nb, s, _ = qk.shape
    nq = s // ROW_TILE
    return pl.pallas_call(
        functools.partial(_attn_kernel, lam_init=lam_init),
        grid=(nb, ATT_HEADS, nq),
        in_specs=[
            pl.BlockSpec((2, ATT_DH), lambda b, h, i: (0, 0)),
            pl.BlockSpec((2, ATT_DH), lambda b, h, i: (0, 0)),
            pl.BlockSpec((1, ATT_DV, 1), lambda b, h, i: (h, 0, 0)),
            pl.BlockSpec((1, ROW_TILE, LANES), lambda b, h, i: (b, i, h)),
            pl.BlockSpec((1, ROW_TILE, LANES), lambda b, h, i: (b, jnp.minimum(i + 1, nq - 1), h)),
            pl.BlockSpec((1, s, LANES), lambda b, h, i: (b, 0, ATT_HEADS + h)),
            pl.BlockSpec((1, VT_ROWS, s), lambda b, h, i: (b, h, 0)),
        ],
        out_specs=pl.BlockSpec((1, ROW_TILE, LANES), lambda b, h, i: (b, i, h)),
        out_shape=jax.ShapeDtypeStruct((nb, s, BRANCH_W), BF16),
        scratch_shapes=[pltpu.VMEM((s, 2 * ROW_TILE), F32), pltpu.VMEM((s, 2 * ROW_TILE), F32),
                        pltpu.VMEM((SUBLANES, 2 * ROW_TILE), F32)],
        compiler_params=_cparams(("arbitrary", "arbitrary", "arbitrary")),
    )(lam_q, lam_k, attn_g.reshape(ATT_HEADS, ATT_DV, 1), qk, qk, qk, vt)


def _scan_chunk(j, nc, reverse):
    if not reverse:
        return j
    return jnp.where(j == 0, 0, nc - j)


def _conv4(x_ref, xp_ref, xn_ref, c, nc, w_ref, b_ref):
    x = x_ref[0].astype(F32)
    t = x.shape[0]
    row = lax.broadcasted_iota(jnp.int32, (SUBLANES, x.shape[1]), 0)
    prev_ok = (c >= 2).astype(F32)
    next_ok = jnp.logical_and(c >= 1, c < nc - 1).astype(F32)
    prev = xp_ref[0].astype(F32)
    p1 = prev[HALO - 1:HALO, :] * prev_ok
    p2 = prev[HALO - 2:HALO - 1, :] * prev_ok
    n0 = xn_ref[0].astype(F32)[0:1, :] * next_ok

    def patch_head(a, fix):
        return jnp.concatenate([fix(a[:SUBLANES, :]), a[SUBLANES:, :]], axis=0)

    def patch_tail(a, fix):
        return jnp.concatenate([a[:t - SUBLANES, :], fix(a[t - SUBLANES:, :])], axis=0)

    xm1 = patch_head(pltpu.roll(x, 1, 0), lambda a: jnp.where(row == 0, p1, a))
    xm2 = patch_head(pltpu.roll(x, 2, 0), lambda a: jnp.where(row == 0, p2, jnp.where(row == 1, p1, a)))
    xp1 = patch_tail(pltpu.roll(x, t - 1, 0), lambda a: jnp.where(row == SUBLANES - 1, n0, a))
    return w_ref[0:1, :] * xm2 + w_ref[1:2, :] * xm1 + w_ref[2:3, :] * x + w_ref[3:4, :] * xp1 + b_ref[...]


def _tile_specs(width, nc, reverse, blk=0):
    per = ROW_TILE // HALO
    last = nc * per - 1
    ch = lambda j: _scan_chunk(j, nc, reverse)
    cur = pl.BlockSpec((1, ROW_TILE, width), lambda b, j: (b, ch(j), blk))
    prev = pl.BlockSpec((1, HALO, width), lambda b, j: (b, jnp.maximum(ch(j) * per - 1, 0), blk))
    nxt = pl.BlockSpec((1, HALO, width), lambda b, j: (b, jnp.minimum((ch(j) + 1) * per, last), blk))
    return cur, prev, nxt


def _rglru_kernel(*refs, reverse, nc):
    if reverse:
        (x_ref, xp_ref, xn_ref, cw_ref, cb_ref, wg_ref, ba_ref, bi_ref, lam_ref,
         yf_ref, lg_ref, o_ref, h_ref) = refs
    else:
        (x_ref, xp_ref, xn_ref, cw_ref, cb_ref, wg_ref, ba_ref, bi_ref, lam_ref, o_ref, h_ref) = refs
    j = pl.program_id(1)
    c = _scan_chunk(j, nc, reverse)

    @pl.when(j == 0)
    def _():
        h_ref[...] = jnp.zeros_like(h_ref)

    u = _conv4(x_ref, xp_ref, xn_ref, c, nc, cw_ref, cb_ref)
    t = u.shape[0]
    pre = jnp.dot(u.astype(BF16), wg_ref[...], preferred_element_type=F32)
    r = jax.nn.sigmoid(pre[:, :BRANCH_W] + ba_ref[...])
    gi = jax.nn.sigmoid(pre[:, BRANCH_W:] + bi_ref[...])
    log_a = (-LRU_C) * r * _softplus(-lam_ref[...])
    a = jnp.exp(log_a)
    var = 1.0 - jnp.exp(2.0 * log_a)
    bv = jnp.where(var > 0.0, var * lax.rsqrt(var), 0.0) * (gi * u)

    in_group = lax.broadcasted_iota(jnp.int32, u.shape, 0) % SUBLANES
    d = 1
    while d < SUBLANES:
        if reverse:
            keep = in_group < SUBLANES - d
            sh = t - d
        else:
            keep = in_group >= d
            sh = d
        a_s = jnp.where(keep, pltpu.roll(a, sh, 0), 1.0)
        b_s = jnp.where(keep, pltpu.roll(bv, sh, 0), 0.0)
        bv = a * b_s + bv
        a = a * a_s
        d *= 2
    carry = jnp.broadcast_to(h_ref[...], (SUBLANES, BRANCH_W))
    ngroup = t // SUBLANES
    pieces = [None] * ngroup
    for gidx in (range(ngroup - 1, -1, -1) if reverse else range(ngroup)):
        rows = slice(gidx * SUBLANES, (gidx + 1) * SUBLANES)
        pieces[gidx] = a[rows, :] * carry + bv[rows, :]
        edge = pieces[gidx][0:1, :] if reverse else pieces[gidx][SUBLANES - 1:SUBLANES, :]
        carry = jnp.broadcast_to(edge, (SUBLANES, BRANCH_W))
    h_ref[...] = carry[0:1, :]
    hs = jnp.concatenate(pieces, axis=0)
    if reverse:
        o_ref[0] = ((yf_ref[0] + hs) * _gelu_tanh(lg_ref[0].astype(F32))).astype(BF16)
    else:
        o_ref[0] = hs


def _rglru(lxg, conv_w, conv_b, w_gate, b_a, b_i, lam, reverse, yf=None):
    nb, s, _ = lxg.shape
    nc = s // ROW_TILE
    cur, prev, nxt = _tile_specs(BRANCH_W, nc, reverse)
    params = [conv_w, conv_b, w_gate, b_a, b_i, lam]
    in_specs = [cur, prev, nxt] + [_resident(p.shape) for p in params]
    args = [lxg, lxg, lxg] + params
    if reverse:
        in_specs += [cur, _tile_specs(BRANCH_W, nc, reverse, blk=1)[0]]
        args += [yf, lxg]
    return pl.pallas_call(
        functools.partial(_rglru_kernel, reverse=reverse, nc=nc),
        grid=(nb, nc),
        in_specs=in_specs,
        out_specs=cur,
        out_shape=jax.ShapeDtypeStruct((nb, s, BRANCH_W), BF16 if reverse else F32),
        scratch_shapes=[pltpu.VMEM((1, BRANCH_W), F32)],
        compiler_params=_cparams(("arbitrary", "arbitrary")),
    )(*args)


def _ssd_kernel(*refs, reverse, nc, direction):
    if reverse:
        (x_ref, xp_ref, xn_ref, dt_ref, dtt_ref, cw_ref, cb_ref, dtb_ref, dtbc_ref, alog_ref, alogc_ref,
         exp_ref, yf_ref, z_ref, skip_ref, g_ref, o_ref, st_ref) = refs
    else:
        (x_ref, xp_ref, xn_ref, dt_ref, dtt_ref, cw_ref, cb_ref, dtb_ref, dtbc_ref, alog_ref, alogc_ref,
         exp_ref, o_ref, st_ref) = refs
    j = pl.program_id(1)
    c = _scan_chunk(j, nc, reverse)

    @pl.when(j == 0)
    def _():
        st_ref[...] = jnp.zeros_like(st_ref)

    u = _silu(_conv4(x_ref, xp_ref, xn_ref, c, nc, cw_ref, cb_ref))
    t = u.shape[0]
    xs = u[:, :BRANCH_W]
    gw = SSD_HPG * SSD_HEADDIM

    dt_c = _softplus(dt_ref[0] + dtb_ref[...])
    a_c = dt_c * (-jnp.exp(alog_ref[...]))
    a_r = _softplus(dtt_ref[0] + dtbc_ref[...]) * (-jnp.exp(alogc_ref[...]))
    ri = lax.broadcasted_iota(jnp.int32, (t, t), 0)
    ci = lax.broadcasted_iota(jnp.int32, (t, t), 1)
    lower = (ci <= ri).astype(F32)
    upper = (ci >= ri).astype(F32)
    cs_c = jnp.dot(upper if reverse else lower, a_c, preferred_element_type=F32, precision=HIGHEST)
    cs_r = jnp.dot(a_r, lower if reverse else upper, preferred_element_type=F32, precision=HIGHEST)
    tot = cs_c[0:1, :] if reverse else cs_c[t - 1:t, :]
    keep = (ci >= ri) if reverse else (ci <= ri)

    expand = exp_ref[...]
    xdt = xs * jnp.dot(dt_c.astype(BF16), expand, preferred_element_type=F32)
    e_in = jnp.dot(jnp.exp(cs_c).astype(BF16), expand, preferred_element_type=F32)
    e_out = jnp.dot(jnp.exp(tot - cs_c).astype(BF16), expand, preferred_element_type=F32)
    e_tot = jnp.dot(jnp.broadcast_to(jnp.exp(tot), (SUBLANES, LANES)), expand.astype(F32),
                    preferred_element_type=F32, precision=HIGHEST)[0:1, :]
    xdt_b = xdt.astype(BF16)
    xdec_b = (xdt * e_out).astype(BF16)
    lane_head = lax.broadcasted_iota(jnp.int32, (t, gw), 1) // SSD_HEADDIM
    zero_b = jnp.zeros((t, gw), BF16)

    ys = []
    for g in range(SSD_GROUPS):
        bm = u[:, BRANCH_W + g * SSD_STATE:BRANCH_W + (g + 1) * SSD_STATE].astype(BF16)
        cm = u[:, BRANCH_W + (SSD_GROUPS + g) * SSD_STATE:BRANCH_W + (SSD_GROUPS + g + 1) * SSD_STATE].astype(BF16)
        cb = lax.dot_general(cm, bm, (((1,), (1,)), ((), ())), preferred_element_type=F32)
        st_g = st_ref[:, g * gw:(g + 1) * gw]
        y = jnp.dot(cm, st_g.astype(BF16), preferred_element_type=F32) * e_in[:, g * gw:(g + 1) * gw]
        xg = xdt_b[:, g * gw:(g + 1) * gw]
        for rr in range(SSD_HPG):
            col = direction * SSD_HEADS + g * SSD_HPG + rr
            decay = jnp.where(keep, jnp.exp(cs_c[:, col:col + 1] - cs_r[col:col + 1, :]), 0.0)
            gm = (cb * decay).astype(BF16)
            y = y + jnp.dot(gm, jnp.where(lane_head == rr, xg, zero_b), preferred_element_type=F32)
        ys.append(y)
        upd = lax.dot_general(bm, xdec_b[:, g * gw:(g + 1) * gw], (((0,), (0,)), ((), ())),
                              preferred_element_type=F32)
        st_ref[:, g * gw:(g + 1) * gw] = st_g * e_tot[:, g * gw:(g + 1) * gw] + upd
    y = jnp.concatenate(ys, axis=1)

    if reverse:
        y = yf_ref[0] + y + skip_ref[...] * xs
        tz = y * _silu(z_ref[0].astype(F32))
        outs = []
        for g in range(SSD_GROUPS):
            tg = tz[:, g * gw:(g + 1) * gw]
            outs.append(tg * lax.rsqrt(jnp.mean(tg * tg, axis=1, keepdims=True) + RMS_EPS))
        o_ref[0] = (jnp.concatenate(outs, axis=1) * g_ref[...]).astype(BF16)
    else:
        o_ref[0] = y


def _ssd(xbc, dt, dtt, p, direction, yf=None, z=None):
    reverse = direction == 1
    nb, s, _ = xbc.shape
    nc = s // ROW_TILE
    cur, prev, nxt = _tile_specs(2 * BRANCH_W, nc, reverse)
    ch = lambda j: _scan_chunk(j, nc, reverse)
    params = [p["conv_w"], p["conv_b"], p["dtb_row"], p["dtb_col"], p["alog_row"], p["alog_col"], p["expand"]]
    in_specs = [cur, prev, nxt,
                pl.BlockSpec((1, ROW_TILE, LANES), lambda b, j: (b, ch(j), 0)),
                pl.BlockSpec((1, 2 * SSD_HEADS, ROW_TILE), lambda b, j: (b, 0, ch(j)))]
    in_specs += [_resident(a.shape) for a in params]
    args = [xbc, xbc, xbc, dt, dtt] + params
    half = pl.BlockSpec((1, ROW_TILE, BRANCH_W), lambda b, j: (b, ch(j), 0))
    if reverse:
        in_specs += [half, half, _resident(p["skip"].shape), _resident(p["norm_g"].shape)]
        args += [yf, z, p["skip"], p["norm_g"]]
    return pl.pallas_call(
        functools.partial(_ssd_kernel, reverse=reverse, nc=nc, direction=direction),
        grid=(nb, nc),
        in_specs=in_specs,
        out_specs=half,
        out_shape=jax.ShapeDtypeStruct((nb, s, BRANCH_W), BF16 if reverse else F32),
        scratch_shapes=[pltpu.VMEM((SSD_STATE, BRANCH_W), F32)],
        compiler_params=_cparams(("arbitrary", "arbitrary")),
    )(*args)


def _merge_kernel(ya_ref, yb_ref, yc_ref, gt_ref, x_ref, m_ref, wb_ref, wo_ref, g_ref, b_ref,
                  x1_ref, h2_ref, *, alpha):
    acc = None
    for n, y_ref in enumerate((ya_ref, yb_ref, yc_ref)):
        proj = jnp.dot(y_ref[0], wb_ref[n], preferred_element_type=F32)
        gate = jax.nn.sigmoid(gt_ref[0, :, n * D_MODEL:(n + 1) * D_MODEL].astype(F32))
        acc = gate * proj if acc is None else acc + gate * proj
    mix = jnp.dot(acc.astype(BF16), wo_ref[...], preferred_element_type=F32)
    x1 = _layer_norm(alpha * x_ref[0] + m_ref[0, 2:3, :] * mix, g_ref[...], b_ref[...])
    x1_ref[0] = x1
    _store_token_rows(h2_ref.at[0], x1 * (1.0 + m_ref[0, 4:5, :]) + m_ref[0, 3:4, :])


def _zero_tile_kernel(h_ref, o_ref):
    del h_ref
    o_ref[...] = jnp.zeros_like(o_ref)


def _merge(ya, yb, yc, gates, xc, mods, w_branch, w_out, ln_g, ln_b, alpha):
    nb, s, _ = xc.shape
    nt = s // ROW_TILE
    row = lambda width: pl.BlockSpec((1, ROW_TILE, width), lambda b, i: (b, i, 0))
    h2_shape = jax.ShapeDtypeStruct((nb, (s + MOE_PAD) * NCHUNK, LANES), F32)
    x1, h2 = pl.pallas_call(
        functools.partial(_merge_kernel, alpha=alpha),
        grid=(nb, nt),
        in_specs=[row(BRANCH_W), row(BRANCH_W), row(BRANCH_W), row(3 * D_MODEL), row(D_MODEL), _mod_spec(nb),
                  _resident(w_branch.shape), _resident(w_out.shape), _resident(ln_g.shape), _resident(ln_b.shape)],
        out_specs=[row(D_MODEL), pl.BlockSpec((1, ROW_TILE * NCHUNK, LANES), lambda b, i: (b, i, 0))],
        out_shape=[jax.ShapeDtypeStruct((nb, s, D_MODEL), F32), h2_shape],
        compiler_params=_cparams(("arbitrary", "arbitrary")),
    )(ya, yb, yc, gates, xc, mods, w_branch, w_out, ln_g, ln_b)
    h2 = pl.pallas_call(
        _zero_tile_kernel,
        grid=(nb,),
        in_specs=[pl.BlockSpec(memory_space=pl.ANY)],
        out_specs=pl.BlockSpec((1, NCHUNK, LANES), lambda b: (b, s, 0)),
        out_shape=h2_shape,
        input_output_aliases={0: 0},
        compiler_params=_cparams(("arbitrary",)),
    )(h2)
    return x1, h2


def _first_index(hit, ridx, n):
    return jnp.min(jnp.where(hit, ridx, n), axis=0, keepdims=True)


def _route_kernel(h_ref, wr_ref, bias_ref, cnt_ref, rank_ref, wgt_ref, base_ref, *, latent_only):
    tm = wgt_ref.shape[2]

    @pl.when(pl.program_id(1) == 0)
    def _():
        base_ref[...] = jnp.zeros_like(base_ref)

    logits = lax.dot_general(wr_ref[...], _load_token_rows(h_ref.at[0], tm), (((1,), (1,)), ((), ())),
                             preferred_element_type=F32, precision=HIGHEST)
    scores = jax.nn.sigmoid(logits)
    sel = scores + bias_ref[...]
    neg = -jnp.inf

    r8 = lax.broadcasted_iota(jnp.int32, (GROUP_SIZE, tm), 0)
    grp = jnp.zeros((N_GROUPS, tm), F32)
    for g in range(N_GROUPS):
        blk = sel[g * GROUP_SIZE:(g + 1) * GROUP_SIZE, :]
        m1 = jnp.max(blk, axis=0, keepdims=True)
        first = _first_index(blk == m1, r8, GROUP_SIZE)
        m2 = jnp.max(jnp.where(r8 == first, neg, blk), axis=0, keepdims=True)
        grp = jnp.where(r8 == g, m1 + m2, grp)
    gsel = jnp.zeros((N_GROUPS, tm), jnp.int32)
    for _ in range(TOPK_GROUPS):
        m = jnp.max(grp, axis=0, keepdims=True)
        hit = r8 == _first_index(grp == m, r8, N_GROUPS)
        gsel = jnp.where(hit, 1, gsel)
        grp = jnp.where(hit, neg, grp)
    cand = jnp.concatenate(
        [jnp.where(gsel[g:g + 1, :] > 0, sel[g * GROUP_SIZE:(g + 1) * GROUP_SIZE, :], neg) for g in range(N_GROUPS)],
        axis=0)

    re = lax.broadcasted_iota(jnp.int32, (N_EXPERTS, tm), 0)
    chosen = jnp.zeros((N_EXPERTS, tm), jnp.int32)
    for _ in range(TOP_K):
        m = jnp.max(cand, axis=0, keepdims=True)
        hit = re == _first_index(cand == m, re, N_EXPERTS)
        chosen = jnp.where(hit, 1, chosen)
        cand = jnp.where(hit, neg, cand)
    picked = jnp.where(chosen > 0, scores, 0.0)
    wgt = picked / jnp.sum(picked, axis=0, keepdims=True) * ROUTED_SCALE
    if latent_only:
        chosen = jnp.where(pl.program_id(1) == 0, 0, chosen)

    li = lax.broadcasted_iota(jnp.int32, (LANES, LANES), 0)
    lj = lax.broadcasted_iota(jnp.int32, (LANES, LANES), 1)
    strict = (li < lj).astype(BF16)
    chosen_b = chosen.astype(F32).astype(BF16)
    base = base_ref[...]
    pos = []
    for kb in range(tm // LANES):
        cblk = chosen_b[:, kb * LANES:(kb + 1) * LANES]
        pos.append(jnp.dot(cblk, strict, preferred_element_type=F32) + base)
        base = base + jnp.sum(cblk.astype(F32), axis=1, keepdims=True)
    base_ref[...] = base
    cnt_ref[0] = base.astype(jnp.int32)
    rank_ref[0] = jnp.where(chosen > 0, jnp.concatenate(pos, axis=1).astype(jnp.int32), -1)
    wgt_ref[0] = wgt


def _compact_kernel(cnt_ref, rank_ref, wgt_ref, idx_ref, wl_ref):
    rank = rank_ref[0]
    ne, tm = rank.shape
    lane = lax.broadcasted_iota(jnp.int32, (ne, tm), 1)
    live = jnp.where(rank >= 0, 1, 0)
    dist = lane - rank
    val = lane
    wv = wgt_ref[0]
    step = 1
    while step < tm:
        move = jnp.where(jnp.logical_and(live > 0, (dist & step) != 0), 1, 0)
        sh = tm - step
        take = pltpu.roll(move, sh, 1) > 0
        val = jnp.where(take, pltpu.roll(val, sh, 1), val)
        wv = jnp.where(take, pltpu.roll(wv, sh, 1), wv)
        dist = jnp.where(take, pltpu.roll(dist, sh, 1), dist)
        live = jnp.where(take, 1, jnp.where(move > 0, 0, live))
        step *= 2
    valid = lane < cnt_ref[0]
    spare = jnp.full((ne, MOE_SPARE), tm * NCHUNK, jnp.int32)
    idx_ref[0] = jnp.concatenate([jnp.where(valid, val * NCHUNK, tm * NCHUNK), spare], axis=1)
    wl_ref[0] = jnp.where(valid, wv, 0.0)


def _router(h2, s, w_router_t, bias_col, latent_only):
    nb = h2.shape[0]
    dense = pl.BlockSpec((1, N_EXPERTS, ROW_TILE), lambda b, i: (b, 0, i))
    counts, rank, wgt = pl.pallas_call(
        functools.partial(_route_kernel, latent_only=latent_only),
        grid=(nb, s // ROW_TILE),
        in_specs=[pl.BlockSpec((1, ROW_TILE * NCHUNK, LANES), lambda b, i: (b, i, 0)),
                  _resident(w_router_t.shape), _resident(bias_col.shape)],
        out_specs=[pl.BlockSpec((1, N_EXPERTS, 1), lambda b, i: (b, 0, 0)), dense, dense],
        out_shape=[jax.ShapeDtypeStruct((nb, N_EXPERTS, 1), jnp.int32),
                   jax.ShapeDtypeStruct((nb, N_EXPERTS, s), jnp.int32),
                   jax.ShapeDtypeStruct((nb, N_EXPERTS, s), F32)],
        scratch_shapes=[pltpu.VMEM((N_EXPERTS, 1), F32)],
        compiler_params=_cparams(("arbitrary", "arbitrary")),
    )(h2, w_router_t, bias_col)
    rows = lambda width: pl.BlockSpec((1, SUBLANES, width), lambda b, g: (b, g, 0))
    idx, wl = pl.pallas_call(
        _compact_kernel,
        grid=(nb, N_EXPERTS // SUBLANES),
        in_specs=[rows(1), rows(s), rows(s)],
        out_specs=[rows(s + MOE_SPARE), rows(s)],
        out_shape=[jax.ShapeDtypeStruct((nb, N_EXPERTS, s + MOE_SPARE), jnp.int32),
                   jax.ShapeDtypeStruct((nb, N_EXPERTS, s), F32)],
        compiler_params=_cparams(("arbitrary", "arbitrary")),
    )(counts, rank, wgt)
    return counts, idx, wl


SCATTER_GROUP = 16


def _experts_kernel(cnt_ref, idx_ref, wl_ref, h_ref, wu_ref, wd_ref, o_ref, xa_ref, xb_ref, ya_ref, yb_ref):
    t = pl.program_id(0)
    e = pl.program_id(1)
    h_rows = h_ref.at[0]
    acc_ref = o_ref.at[0]

    @pl.when(e == 0)
    def _():
        o_ref[...] = jnp.zeros_like(o_ref)

    count = cnt_ref[t * N_EXPERTS + e]
    nblk = (count + MOE_BLOCK - 1) // MOE_BLOCK
    ri = lax.broadcasted_iota(jnp.int32, (MOE_BLOCK, MOE_BLOCK), 0)
    ci = lax.broadcasted_iota(jnp.int32, (MOE_BLOCK, MOE_BLOCK), 1)

    def gather(blk, x_ref):
        ids = idx_ref.at[0, 0, pl.ds(blk * MOE_BLOCK, MOE_BLOCK)]
        for r in range(MOE_BLOCK):
            x_ref[pl.ds(r * NCHUNK, NCHUNK), :] = h_rows[pl.ds(pl.multiple_of(ids[r], NCHUNK), NCHUNK), :]

    def swiglu(x, blks, y_refs):
        gu = jnp.dot(x.astype(BF16), wu_ref[0], preferred_element_type=F32)
        act = _silu(gu[:, :EXPERT_F]) * gu[:, EXPERT_F:]
        y = jnp.dot(act.astype(BF16), wd_ref[0], preferred_element_type=F32)
        for k, (blk, y_ref) in enumerate(zip(blks, y_refs)):
            w_row = wl_ref[0, pl.ds(blk, 1), :]
            w_col = jnp.sum(jnp.where(ri == ci, w_row, 0.0), axis=1, keepdims=True)
            _store_token_rows(y_ref, y[k * MOE_BLOCK:(k + 1) * MOE_BLOCK, :] * w_col)

    def scatter(blk, y_ref):
        ids = idx_ref.at[0, 0, pl.ds(blk * MOE_BLOCK, MOE_BLOCK)]
        for g0 in range(0, MOE_BLOCK, SCATTER_GROUP):
            rows = range(g0, g0 + SCATTER_GROUP)
            dst = [pl.multiple_of(ids[r], NCHUNK) for r in rows]
            new = [acc_ref[pl.ds(d, NCHUNK), :] + y_ref[pl.ds(r * NCHUNK, NCHUNK), :] for d, r in zip(dst, rows)]
            for d, v in zip(dst, new):
                acc_ref[pl.ds(d, NCHUNK), :] = v

    npair = nblk // 2

    @pl.when(nblk > 0)
    def _():
        gather(0, xa_ref)

    @pl.when(npair > 0)
    def _():
        gather(1, xb_ref)
        ya_ref[...] = jnp.zeros_like(ya_ref)
        yb_ref[...] = jnp.zeros_like(yb_ref)

    def pair(j, carry):
        b0 = 2 * j
        x = jnp.concatenate([_load_token_rows(xa_ref, MOE_BLOCK), _load_token_rows(xb_ref, MOE_BLOCK)], axis=0)
        scatter(jnp.maximum(b0 - 2, 0), ya_ref)
        scatter(jnp.maximum(b0 - 1, 0), yb_ref)
        gather(b0 + 2, xa_ref)
        gather(b0 + 3, xb_ref)
        swiglu(x, (b0, b0 + 1), (ya_ref, yb_ref))
        return carry

    lax.fori_loop(0, npair, pair, 0)

    @pl.when(npair > 0)
    def _():
        scatter(2 * npair - 2, ya_ref)
        scatter(2 * npair - 1, yb_ref)

    @pl.when(nblk % 2 == 1)
    def _():
        swiglu(_load_token_rows(xa_ref, MOE_BLOCK), (nblk - 1,), (ya_ref,))
        scatter(nblk - 1, ya_ref)


def _experts(h2, counts, idx, wl, w_up, w_down):
    nb, rows, _ = h2.shape
    nlist = idx.shape[-1]
    tile = pl.BlockSpec((1, rows, LANES), lambda t, e, c: (t, 0, 0), pipeline_mode=pl.Buffered(1))
    buf = pltpu.VMEM((MOE_BLOCK * NCHUNK, LANES), F32)
    grid_spec = pltpu.PrefetchScalarGridSpec(
        num_scalar_prefetch=1,
        grid=(nb, N_EXPERTS),
        in_specs=[pl.BlockSpec((1, 1, nlist), lambda t, e, c: (t * N_EXPERTS + e, 0, 0), memory_space=pltpu.SMEM),
                  pl.BlockSpec((1,) + wl.shape[1:], lambda t, e, c: (t * N_EXPERTS + e, 0, 0)),
                  tile,
                  pl.BlockSpec((1, D_MODEL, 2 * EXPERT_F), lambda t, e, c: (e, 0, 0)),
                  pl.BlockSpec((1, EXPERT_F, D_MODEL), lambda t, e, c: (e, 0, 0))],
        out_specs=tile,
        scratch_shapes=[buf, buf, buf, buf],
    )
    return pl.pallas_call(
        _experts_kernel,
        grid_spec=grid_spec,
        out_shape=jax.ShapeDtypeStruct(h2.shape, F32),
        compiler_params=_cparams(("arbitrary", "arbitrary")),
    )(counts, idx, wl, h2, w_up, w_down)


def _ffn_out_kernel(x1_ref, h2_ref, fr_ref, m_ref, wu_ref, wd_ref, g_ref, b_ref, o_ref, *, alpha):
    gu = jnp.dot(_load_token_rows(h2_ref.at[0], ROW_TILE).astype(BF16), wu_ref[...], preferred_element_type=F32)
    act = _silu(gu[:, :EXPERT_F]) * gu[:, EXPERT_F:]
    f = (jnp.dot(act.astype(BF16), wd_ref[...], preferred_element_type=F32)
         + _load_token_rows(fr_ref.at[0], ROW_TILE))
    o_ref[0] = _layer_norm(alpha * x1_ref[0] + m_ref[0, 5:6, :] * f, g_ref[...], b_ref[...])


def _ffn_out(x1, h2, fr, mods, ws_up, ws_down, ln_g, ln_b, alpha, latent_only):
    nb, s, _ = x1.shape
    nt = s // ROW_TILE
    skip = CTX_LEN // ROW_TILE if latent_only else 0
    row = pl.BlockSpec((1, ROW_TILE, D_MODEL), lambda b, i: (b, i + skip, 0))
    mod = pl.BlockSpec((1, 6, D_MODEL), lambda b, i: (jnp.where(i + skip == 0, nb, b), 0, 0))
    chunked = pl.BlockSpec((1, ROW_TILE * NCHUNK, LANES), lambda b, i: (b, i + skip, 0))
    return pl.pallas_call(
        functools.partial(_ffn_out_kernel, alpha=alpha),
        grid=(nb, nt - skip),
        in_specs=[row, chunked, chunked, mod, _resident(ws_up.shape), _resident(ws_down.shape),
                  _resident(ln_g.shape), _resident(ln_b.shape)],
        out_specs=pl.BlockSpec((1, ROW_TILE, D_MODEL), lambda b, i: (b, i, 0)),
        out_shape=jax.ShapeDtypeStruct((nb, s - skip * ROW_TILE, D_MODEL), F32),
        compiler_params=_cparams(("arbitrary", "arbitrary")),
    )(x1, h2, fr, mods, ws_up, ws_down, ln_g, ln_b)


def _rope_tables(n_lat):
    t = jnp.arange(n_lat)
    rowp = (t // GRID_W).astype(F32)
    colp = (t % GRID_W).astype(F32)
    n_freq = ATT_DH // 4
    inv = ROPE_BASE ** (-jnp.arange(n_freq, dtype=F32) / n_freq)
    ang = jnp.concatenate([rowp[:, None] * inv, colp[:, None] * inv], axis=-1)
    lane = jnp.arange(LANES)
    cos = jnp.cos(ang)[:, lane % (ATT_DH // 2)]
    sign = jnp.where((lane % ATT_DH) < ATT_DH // 2, -1.0, 1.0).astype(F32)
    sin = jnp.sin(ang)[:, lane % (ATT_DH // 2)] * sign
    cos = jnp.concatenate([jnp.ones((CTX_LEN, LANES), F32), cos], axis=0)
    sin = jnp.concatenate([jnp.zeros((CTX_LEN, LANES), F32), sin], axis=0)
    return cos, sin


def _block_diag(w):
    n, k, _ = w.shape
    eye = jnp.eye(n, dtype=w.dtype)
    return (eye[:, None, :, None] * w[:, :, None, :]).reshape(n * k, n * k)


def _inproj_weights(w):
    o_v = 2 * BRANCH_W
    o_lx = o_v + ATT_HEADS * ATT_DV
    o_z = o_lx + 2 * BRANCH_W
    o_xbc = o_z + BRANCH_W
    o_dt = o_xbc + 2 * BRANCH_W
    o_g = o_dt + 2 * SSD_HEADS
    wv = w[:, o_v:o_lx].T.reshape(ATT_HEADS, ATT_DV, D_MODEL)
    wv = jnp.pad(wv, ((0, 0), (0, VT_ROWS - ATT_DV), (0, 0))).reshape(ATT_HEADS * VT_ROWS, D_MODEL)
    ones = jnp.zeros((ATT_HEADS, VT_ROWS, 1), F32).at[:, ATT_DV, 0].set(1.0).reshape(ATT_HEADS * VT_ROWS, 1)
    wdt = w[:, o_dt:o_g]
    return {
        "qk": w[:, :o_v].astype(BF16),
        "vt": wv.astype(BF16),
        "ones": ones,
        "lxg": w[:, o_lx:o_z].astype(BF16),
        "z": w[:, o_z:o_xbc].astype(BF16),
        "xbc": w[:, o_xbc:o_dt].astype(BF16),
        "dt": jnp.pad(wdt, ((0, 0), (0, LANES - 2 * SSD_HEADS))).astype(BF16),
        "dtt": wdt.T.astype(BF16),
        "g": w[:, o_g:].astype(BF16),
    }


def _pad_row(v):
    return jnp.pad(v.reshape(1, -1), ((0, 0), (0, LANES - v.size)))


def kernel(x, c, ctx, c_ctx, w_mod, b_mod, w_in, lam_q, lam_k, attn_norm_g, lru_conv_w, lru_conv_b, lru_wa, lru_ba, lru_wi, lru_bi, lru_lambda, ssd_conv_w, ssd_conv_b, ssd_dt_bias, ssd_a_log, ssd_d, ssd_norm_g, w_branch, w_out, ln1_g, ln1_b, w_router, router_bias, w_up, w_down, ws_up, ws_down, ln2_g, ln2_b):
    nb, n_lat, _ = x.shape
    depth = w_mod.shape[0]
    assert ctx.shape[1] == CTX_LEN and n_lat % ROW_TILE == 0 and nb + 1 <= 16
    s = CTX_LEN + n_lat
    alpha = (2 * depth) ** 0.25

    xc = jnp.concatenate([ctx, x], axis=1)
    cc = jnp.zeros((16, D_MODEL), F32).at[:nb].set(c).at[nb].set(c_ctx)
    mods_all = _modulation(cc, w_mod, b_mod).reshape(depth, 16, 6, D_MODEL)
    cos_t, sin_t = _rope_tables(n_lat)
    head_of_channel = jnp.arange(BRANCH_W) // SSD_HEADDIM

    for l in range(depth):
        last = l == depth - 1
        lam_init = 0.8 - 0.6 * math.exp(-0.3 * l)
        mods = mods_all[l]
        qk, vt, lxg, z, xbc, dt, dtt, gates = _inproj(xc, mods, cos_t, sin_t, _inproj_weights(w_in[l]))

        ya = _attention(qk, vt, lam_q[l], lam_k[l], attn_norm_g[l], lam_init)

        yb = None
        for d in range(2):
            w_gate = jnp.concatenate([_block_diag(lru_wa[l, d]), _block_diag(lru_wi[l, d])], axis=1).astype(BF16)
            yb = _rglru(lxg, lru_conv_w[l], lru_conv_b[l].reshape(1, -1), w_gate,
                        lru_ba[l, d].reshape(1, -1), lru_bi[l, d].reshape(1, -1), lru_lambda[l, d].reshape(1, -1),
                        reverse=(d == 1), yf=yb)

        ssd_p = {
            "conv_w": ssd_conv_w[l], "conv_b": ssd_conv_b[l].reshape(1, -1),
            "dtb_row": _pad_row(ssd_dt_bias[l]), "dtb_col": ssd_dt_bias[l].reshape(-1, 1),
            "alog_row": _pad_row(ssd_a_log[l]), "alog_col": ssd_a_log[l].reshape(-1, 1),
            "skip": jnp.repeat(ssd_d[l], SSD_HEADDIM).reshape(1, -1), "norm_g": ssd_norm_g[l].reshape(1, -1),
        }
        yc = None
        for d in range(2):
            ssd_p["expand"] = (jnp.arange(LANES)[:, None] == d * SSD_HEADS + head_of_channel[None, :]).astype(BF16)
            yc = _ssd(xbc, dt, dtt, ssd_p, d, yf=yc, z=z)

        x1, h2 = _merge(ya, yb, yc, gates, xc, mods, w_branch[l].astype(BF16), w_out[l].astype(BF16),
                        ln1_g[l].reshape(1, -1), ln1_b[l].reshape(1, -1), alpha)

        counts, idx, wl = _router(h2, s, w_router[l].T, router_bias[l].reshape(-1, 1), latent_only=last)
        fr = _experts(h2, counts.reshape(nb * N_EXPERTS), idx.reshape(nb * N_EXPERTS, 1, s + MOE_SPARE),
                      wl.reshape(nb * N_EXPERTS, s // MOE_BLOCK, MOE_BLOCK),
                      w_up[l].astype(BF16), w_down[l].astype(BF16))
        xc = _ffn_out(x1, h2, fr, mods, ws_up[l].astype(BF16), ws_down[l].astype(BF16),
                      ln2_g[l].reshape(1, -1), ln2_b[l].reshape(1, -1), alpha, latent_only=last)
    return xc
```

```python
import functools
import math

import jax
import jax.numpy as jnp
from jax import lax
from jax.experimental import pallas as pl
from jax.experimental.pallas import tpu as pltpu

F32 = jnp.float32
BF16 = jnp.bfloat16
HIGHEST = lax.Precision.HIGHEST

D_MODEL = 1024
GRID_W = 64
CTX_LEN = 256
BRANCH_W = 512
ATT_HEADS = 4
ATT_DH = 64
ATT_DV = 128
ROPE_BASE = 10000.0
LRU_C = 8.0
SSD_HEADS = 8
SSD_HEADDIM = 64
SSD_HPG = 4
SSD_GROUPS = 2
SSD_STATE = 128
N_EXPERTS = 64
N_GROUPS = 8
GROUP_SIZE = N_EXPERTS // N_GROUPS
TOP_K = 8
TOPK_GROUPS = 4
EXPERT_F = 256
ROUTED_SCALE = 2.5
LN_EPS = 1e-5
RMS_EPS = 1e-6

ROW_TILE = 256
HALO = 16
VT_ROWS = 144
KEY_CHUNK = 256
SCORE_CHUNKS = 2
MOE_BLOCK = 128
MOE_PAD = ROW_TILE
MOE_SPARE = 2 * MOE_BLOCK
SUBLANES = 8
LANES = 128
VMEM_LIMIT = 56 * 1024 * 1024
LOG2E = 1.4426950408889634


def _cparams(sem):
    return pltpu.CompilerParams(dimension_semantics=sem, vmem_limit_bytes=VMEM_LIMIT)


def _resident(shape):
    nd = len(shape)
    return pl.BlockSpec(shape, lambda *_: (0,) * nd, pipeline_mode=pl.Buffered(1))


def _silu(x):
    return x * jax.nn.sigmoid(x)


def _softplus(x):
    return jnp.maximum(x, 0.0) + jnp.log1p(jnp.exp(-jnp.abs(x)))


def _gelu_tanh(x):
    return 0.5 * x * (1.0 + jnp.tanh(math.sqrt(2.0 / math.pi) * (x + 0.044715 * (x * x * x))))


NCHUNK = D_MODEL // LANES


def _load_token_rows(ref, n, first=0):
    return jnp.concatenate([ref[pl.ds(first * NCHUNK + s, n, stride=NCHUNK), :] for s in range(NCHUNK)], axis=1)


def _store_token_rows(ref, x):
    n = x.shape[0]
    for s in range(NCHUNK):
        ref[pl.ds(s, n, stride=NCHUNK), :] = x[:, s * LANES:(s + 1) * LANES]


def _layer_norm(x, g, b):
    mu = jnp.mean(x, axis=-1, keepdims=True)
    xc = x - mu
    var = jnp.mean(xc * xc, axis=-1, keepdims=True)
    return xc * lax.rsqrt(var + LN_EPS) * g + b


def _mod_kernel(c_ref, w_ref, b_ref, o_ref):
    s = _silu(c_ref[...])
    o_ref[0] = jnp.dot(s, w_ref[0], preferred_element_type=F32, precision=HIGHEST) + b_ref[0]


def _modulation(cc, w_mod, b_mod):
    depth = w_mod.shape[0]
    nblk = 6
    return pl.pallas_call(
        _mod_kernel,
        grid=(depth, nblk),
        in_specs=[
            pl.BlockSpec((16, D_MODEL), lambda l, j: (0, 0)),
            pl.BlockSpec((1, D_MODEL, D_MODEL), lambda l, j: (l, 0, j)),
            pl.BlockSpec((1, 1, D_MODEL), lambda l, j: (l, 0, j)),
        ],
        out_specs=pl.BlockSpec((1, 16, D_MODEL), lambda l, j: (l, 0, j)),
        out_shape=jax.ShapeDtypeStruct((depth, 16, nblk * D_MODEL), F32),
        compiler_params=_cparams(("arbitrary", "arbitrary")),
    )(cc, w_mod, b_mod.reshape(depth, 1, nblk * D_MODEL))


def _mod_spec(nb):
    return pl.BlockSpec((1, 6, D_MODEL), lambda b, i: (jnp.where(i == 0, nb, b), 0, 0))


def _inproj_kernel(x_ref, m_ref, cos_ref, sin_ref, wqk_ref, wvt_ref, ones_ref, wlxg_ref, wz_ref,
                   wxbc_ref, wdt_ref, wdtt_ref, wg_ref,
                   qk_ref, vt_ref, lxg_ref, z_ref, xbc_ref, dt_ref, dtt_ref, g_ref):
    x = x_ref[0]
    h = (x * (1.0 + m_ref[0, 1:2, :]) + m_ref[0, 0:1, :]).astype(BF16)
    nt = (((1,), (1,)), ((), ()))

    qk = jnp.dot(h, wqk_ref[...], preferred_element_type=F32)
    cos = cos_ref[...]
    sin = sin_ref[...]
    lane = lax.broadcasted_iota(jnp.int32, cos.shape, 1)
    first_half = (lane % ATT_DH) < (ATT_DH // 2)
    for j in range(2 * ATT_HEADS):
        blk = qk[:, j * LANES:(j + 1) * LANES]
        partner = jnp.where(first_half, pltpu.roll(blk, LANES - ATT_DH // 2, 1),
                            pltpu.roll(blk, ATT_DH // 2, 1))
        r = blk * cos + partner * sin
        if j < ATT_HEADS:
            r = r * (ATT_DH ** -0.5 * LOG2E)
        qk_ref[0, :, j * LANES:(j + 1) * LANES] = r.astype(BF16)

    vt = lax.dot_general(wvt_ref[...], h, nt, preferred_element_type=F32) + ones_ref[...]
    vt_ref[0] = vt.astype(BF16)
    lxg_ref[0] = jnp.dot(h, wlxg_ref[...], preferred_element_type=F32).astype(BF16)
    z_ref[0] = jnp.dot(h, wz_ref[...], preferred_element_type=F32).astype(BF16)
    xbc_ref[0] = jnp.dot(h, wxbc_ref[...], preferred_element_type=F32).astype(BF16)
    dt_ref[0] = jnp.dot(h, wdt_ref[...], preferred_element_type=F32)
    dtt_ref[0] = lax.dot_general(wdtt_ref[...], h, nt, preferred_element_type=F32)
    g_ref[0] = jnp.dot(h, wg_ref[...], preferred_element_type=F32).astype(BF16)


def _inproj(xc, mods, cos_t, sin_t, w):
    nb, s, _ = xc.shape
    nt = s // ROW_TILE
    row = lambda width: pl.BlockSpec((1, ROW_TILE, width), lambda b, i: (b, i, 0))
    col = lambda rows: pl.BlockSpec((1, rows, ROW_TILE), lambda b, i: (b, 0, i))
    vt_rows = ATT_HEADS * VT_ROWS
    outs = [
        (jax.ShapeDtypeStruct((nb, s, 2 * BRANCH_W), BF16), row(2 * BRANCH_W)),
        (jax.ShapeDtypeStruct((nb, vt_rows, s), BF16), col(vt_rows)),
        (jax.ShapeDtypeStruct((nb, s, 2 * BRANCH_W), BF16), row(2 * BRANCH_W)),
        (jax.ShapeDtypeStruct((nb, s, BRANCH_W), BF16), row(BRANCH_W)),
        (jax.ShapeDtypeStruct((nb, s, 2 * BRANCH_W), BF16), row(2 * BRANCH_W)),
        (jax.ShapeDtypeStruct((nb, s, LANES), F32), row(LANES)),
        (jax.ShapeDtypeStruct((nb, 2 * SSD_HEADS, s), F32), col(2 * SSD_HEADS)),
        (jax.ShapeDtypeStruct((nb, s, 3 * D_MODEL), BF16), row(3 * D_MODEL)),
    ]
    weights = [w["qk"], w["vt"], w["ones"], w["lxg"], w["z"], w["xbc"], w["dt"], w["dtt"], w["g"]]
    return pl.pallas_call(
        _inproj_kernel,
        grid=(nb, nt),
        in_specs=[row(D_MODEL), _mod_spec(nb),
                  pl.BlockSpec((ROW_TILE, LANES), lambda b, i: (i, 0)),
                  pl.BlockSpec((ROW_TILE, LANES), lambda b, i: (i, 0))]
                 + [_resident(a.shape) for a in weights],
        out_specs=[o[1] for o in outs],
        out_shape=[o[0] for o in outs],
        compiler_params=_cparams(("arbitrary", "arbitrary")),
    )(xc, mods, cos_t, sin_t, *weights)


def _attn_kernel(lq_ref, lk_ref, g_ref, q_ref, qn_ref, k_ref, vt_ref, o_ref, sa_ref, sb_ref, m_ref, *, lam_init):
    i = pl.program_id(2)
    tq = q_ref.shape[1]
    nck = k_ref.shape[1] // KEY_CHUNK
    prod = lq_ref[...] * lk_ref[...]
    d0 = jnp.sum(prod[0:1, :], axis=1, keepdims=True)
    d1 = jnp.sum(prod[1:2, :], axis=1, keepdims=True)
    lam = jnp.exp(d0) - jnp.exp(d1) + lam_init
    gcol = g_ref[0] * (1.0 - lam_init)

    def stack_maps(q):
        lane = lax.broadcasted_iota(jnp.int32, q.shape, 1)
        zero = jnp.zeros_like(q)
        return jnp.concatenate([jnp.where(lane < ATT_DH, q, zero), jnp.where(lane >= ATT_DH, q, zero)], axis=0)

    def scores(c, q2, s_ref, m, n=1):
        rows = slice(c * KEY_CHUNK, (c + n) * KEY_CHUNK)
        st = lax.dot_general(k_ref[0, rows, :], q2, (((1,), (1,)), ((), ())),
                             preferred_element_type=F32)
        s_ref[rows, :] = st
        mc = jnp.max(st, axis=0, keepdims=True)
        return mc if m is None else jnp.maximum(m, mc)

    def weigh(c, s_ref, m, acc):
        rows = slice(c * KEY_CHUNK, (c + 1) * KEY_CHUNK)
        e = jnp.exp2((s_ref[rows, :] - m).astype(BF16))
        pv = jnp.dot(vt_ref[0, :, rows], e, preferred_element_type=F32)
        return pv if acc is None else acc + pv

    def finish(acc):
        r = 1.0 / acc[ATT_DV:ATT_DV + 1, :]
        o = acc[:ATT_DV, :tq] * r[:, :tq] - lam * (acc[:ATT_DV, tq:] * r[:, tq:])
        ms = jnp.mean(o * o, axis=0, keepdims=True)
        o_ref[0] = (o * lax.rsqrt(ms + RMS_EPS) * gcol).T.astype(BF16)

    @pl.when(i == 0)
    def _():
        q2 = stack_maps(q_ref[0])
        finish(weigh(0, sa_ref, scores(0, q2, sa_ref, None), None))
        q2n = stack_maps(qn_ref[0])
        mn = None
        for c in range(0, nck, SCORE_CHUNKS):
            mn = scores(c, q2n, sb_ref, mn, min(SCORE_CHUNKS, nck - c))
        m_ref[1:2, :] = mn

    def step(cur_ref, nxt_ref, cur_slot, nxt_slot):
        q2n = stack_maps(qn_ref[0])
        m_cur = m_ref[cur_slot:cur_slot + 1, :]
        mn = acc = None
        for c in range(nck):
            if c % SCORE_CHUNKS == 0:
                mn = scores(c, q2n, nxt_ref, mn, min(SCORE_CHUNKS, nck - c))
            acc = weigh(c, cur_ref, m_cur, acc)
        finish(acc)
        m_ref[nxt_slot:nxt_slot + 1, :] = mn

    @pl.when(i % 2 == 1)
    def _():
        step(sb_ref, sa_ref, 1, 0)

    @pl.when(jnp.logical_and(i % 2 == 0, i > 0))
    def _():
        step(sa_ref, sb_ref, 0, 1)


def _attention(qk, vt, lam_q, lam_k, attn_g, lam_init):
    nb, s, _ = qk.shape
    nq = s // ROW_TILE
    return pl.pallas_call(
        functools.partial(_attn_kernel, lam_init=lam_init),
        grid=(nb, ATT_HEADS, nq),
        in_specs=[
            pl.BlockSpec((2, ATT_DH), lambda b, h, i: (0, 0)),
            pl.BlockSpec((2, ATT_DH), lambda b, h, i: (0, 0)),
            pl.BlockSpec((1, ATT_DV, 1), lambda b, h, i: (h, 0, 0)),
            pl.BlockSpec((1, ROW_TILE, LANES), lambda b, h, i: (b, i, h)),
            pl.BlockSpec((1, ROW_TILE, LANES), lambda b, h, i: (b, jnp.minimum(i + 1, nq - 1), h)),
            pl.BlockSpec((1, s, LANES), lambda b, h, i: (b, 0, ATT_HEADS + h)),
            pl.BlockSpec((1, VT_ROWS, s), lambda b, h, i: (b, h, 0)),
        ],
        out_specs=pl.BlockSpec((1, ROW_TILE, LANES), lambda b, h, i: (b, i, h)),
        out_shape=jax.ShapeDtypeStruct((nb, s, BRANCH_W), BF16),
        scratch_shapes=[pltpu.VMEM((s, 2 * ROW_TILE), F32), pltpu.VMEM((s, 2 * ROW_TILE), F32),
                        pltpu.VMEM((SUBLANES, 2 * ROW_TILE), F32)],
        compiler_params=_cparams(("arbitrary", "arbitrary", "arbitrary")),
    )(lam_q, lam_k, attn_g.reshape(ATT_HEADS, ATT_DV, 1), qk, qk, qk, vt)


def _scan_chunk(j, nc, reverse):
    if not reverse:
        return j
    return jnp.where(j == 0, 0, nc - j)


def _conv4(x_ref, xp_ref, xn_ref, c, nc, w_ref, b_ref):
    x = x_ref[0].astype(F32)
    t = x.shape[0]
    row = lax.broadcasted_iota(jnp.int32, (SUBLANES, x.shape[1]), 0)
    prev_ok = (c >= 2).astype(F32)
    next_ok = jnp.logical_and(c >= 1, c < nc - 1).astype(F32)
    prev = xp_ref[0].astype(F32)
    p1 = prev[HALO - 1:HALO, :] * prev_ok
    p2 = prev[HALO - 2:HALO - 1, :] * prev_ok
    n0 = xn_ref[0].astype(F32)[0:1, :] * next_ok

    def patch_head(a, fix):
        return jnp.concatenate([fix(a[:SUBLANES, :]), a[SUBLANES:, :]], axis=0)

    def patch_tail(a, fix):
        return jnp.concatenate([a[:t - SUBLANES, :], fix(a[t - SUBLANES:, :])], axis=0)

    xm1 = patch_head(pltpu.roll(x, 1, 0), lambda a: jnp.where(row == 0, p1, a))
    xm2 = patch_head(pltpu.roll(x, 2, 0), lambda a: jnp.where(row == 0, p2, jnp.where(row == 1, p1, a)))
    xp1 = patch_tail(pltpu.roll(x, t - 1, 0), lambda a: jnp.where(row == SUBLANES - 1, n0, a))
    return w_ref[0:1, :] * xm2 + w_ref[1:2, :] * xm1 + w_ref[2:3, :] * x + w_ref[3:4, :] * xp1 + b_ref[...]


def _tile_specs(width, nc, reverse, blk=0):
    per = ROW_TILE // HALO
    last = nc * per - 1
    ch = lambda j: _scan_chunk(j, nc, reverse)
    cur = pl.BlockSpec((1, ROW_TILE, width), lambda b, j: (b, ch(j), blk))
    prev = pl.BlockSpec((1, HALO, width), lambda b, j: (b, jnp.maximum(ch(j) * per - 1, 0), blk))
    nxt = pl.BlockSpec((1, HALO, width), lambda b, j: (b, jnp.minimum((ch(j) + 1) * per, last), blk))
    return cur, prev, nxt


def _rglru_kernel(*refs, reverse, nc):
    if reverse:
        (x_ref, xp_ref, xn_ref, cw_ref, cb_ref, wg_ref, ba_ref, bi_ref, lam_ref,
         yf_ref, lg_ref, o_ref, h_ref) = refs
    else:
        (x_ref, xp_ref, xn_ref, cw_ref, cb_ref, wg_ref, ba_ref, bi_ref, lam_ref, o_ref, h_ref) = refs
    j = pl.program_id(1)
    c = _scan_chunk(j, nc, reverse)

    @pl.when(j == 0)
    def _():
        h_ref[...] = jnp.zeros_like(h_ref)

    u = _conv4(x_ref, xp_ref, xn_ref, c, nc, cw_ref, cb_ref)
    t = u.shape[0]
    pre = jnp.dot(u.astype(BF16), wg_ref[...], preferred_element_type=F32)
    r = jax.nn.sigmoid(pre[:, :BRANCH_W] + ba_ref[...])
    gi = jax.nn.sigmoid(pre[:, BRANCH_W:] + bi_ref[...])
    log_a = (-LRU_C) * r * _softplus(-lam_ref[...])
    a = jnp.exp(log_a)
    var = 1.0 - jnp.exp(2.0 * log_a)
    bv = jnp.where(var > 0.0, var * lax.rsqrt(var), 0.0) * (gi * u)

    in_group = lax.broadcasted_iota(jnp.int32, u.shape, 0) % SUBLANES
    d = 1
    while d < SUBLANES:
        if reverse:
            keep = in_group < SUBLANES - d
            sh = t - d
        else:
            keep = in_group >= d
            sh = d
        a_s = jnp.where(keep, pltpu.roll(a, sh, 0), 1.0)
        b_s = jnp.where(keep, pltpu.roll(bv, sh, 0), 0.0)
        bv = a * b_s + bv
        a = a * a_s
        d *= 2
    carry = jnp.broadcast_to(h_ref[...], (SUBLANES, BRANCH_W))
    ngroup = t // SUBLANES
    pieces = [None] * ngroup
    for gidx in (range(ngroup - 1, -1, -1) if reverse else range(ngroup)):
        rows = slice(gidx * SUBLANES, (gidx + 1) * SUBLANES)
        pieces[gidx] = a[rows, :] * carry + bv[rows, :]
        edge = pieces[gidx][0:1, :] if reverse else pieces[gidx][SUBLANES - 1:SUBLANES, :]
        carry = jnp.broadcast_to(edge, (SUBLANES, BRANCH_W))
    h_ref[...] = carry[0:1, :]
    hs = jnp.concatenate(pieces, axis=0)
    if reverse:
        o_ref[0] = ((yf_ref[0] + hs) * _gelu_tanh(lg_ref[0].astype(F32))).astype(BF16)
    else:
        o_ref[0] = hs


def _rglru(lxg, conv_w, conv_b, w_gate, b_a, b_i, lam, reverse, yf=None):
    nb, s, _ = lxg.shape
    nc = s // ROW_TILE
    cur, prev, nxt = _tile_specs(BRANCH_W, nc, reverse)
    params = [conv_w, conv_b, w_gate, b_a, b_i, lam]
    in_specs = [cur, prev, nxt] + [_resident(p.shape) for p in params]
    args = [lxg, lxg, lxg] + params
    if reverse:
        in_specs += [cur, _tile_specs(BRANCH_W, nc, reverse, blk=1)[0]]
        args += [yf, lxg]
    return pl.pallas_call(
        functools.partial(_rglru_kernel, reverse=reverse, nc=nc),
        grid=(nb, nc),
        in_specs=in_specs,
        out_specs=cur,
        out_shape=jax.ShapeDtypeStruct((nb, s, BRANCH_W), BF16 if reverse else F32),
        scratch_shapes=[pltpu.VMEM((1, BRANCH_W), F32)],
        compiler_params=_cparams(("arbitrary", "arbitrary")),
    )(*args)


def _ssd_kernel(*refs, reverse, nc, direction):
    if reverse:
        (x_ref, xp_ref, xn_ref, dt_ref, dtt_ref, cw_ref, cb_ref, dtb_ref, dtbc_ref, alog_ref, alogc_ref,
         exp_ref, yf_ref, z_ref, skip_ref, g_ref, o_ref, st_ref) = refs
    else:
        (x_ref, xp_ref, xn_ref, dt_ref, dtt_ref, cw_ref, cb_ref, dtb_ref, dtbc_ref, alog_ref, alogc_ref,
         exp_ref, o_ref, st_ref) = refs
    j = pl.program_id(1)
    c = _scan_chunk(j, nc, reverse)

    @pl.when(j == 0)
    def _():
        st_ref[...] = jnp.zeros_like(st_ref)

    u = _silu(_conv4(x_ref, xp_ref, xn_ref, c, nc, cw_ref, cb_ref))
    t = u.shape[0]
    xs = u[:, :BRANCH_W]
    gw = SSD_HPG * SSD_HEADDIM

    dt_c = _softplus(dt_ref[0] + dtb_ref[...])
    a_c = dt_c * (-jnp.exp(alog_ref[...]))
    a_r = _softplus(dtt_ref[0] + dtbc_ref[...]) * (-jnp.exp(alogc_ref[...]))
    ri = lax.broadcasted_iota(jnp.int32, (t, t), 0)
    ci = lax.broadcasted_iota(jnp.int32, (t, t), 1)
    lower = (ci <= ri).astype(F32)
    upper = (ci >= ri).astype(F32)
    cs_c = jnp.dot(upper if reverse else lower, a_c, preferred_element_type=F32, precision=HIGHEST)
    cs_r = jnp.dot(a_r, lower if reverse else upper, preferred_element_type=F32, precision=HIGHEST)
    tot = cs_c[0:1, :] if reverse else cs_c[t - 1:t, :]
    keep = (ci >= ri) if reverse else (ci <= ri)

    expand = exp_ref[...]
    xdt = xs * jnp.dot(dt_c.astype(BF16), expand, preferred_element_type=F32)
    e_in = jnp.dot(jnp.exp(cs_c).astype(BF16), expand, preferred_element_type=F32)
    e_out = jnp.dot(jnp.exp(tot - cs_c).astype(BF16), expand, preferred_element_type=F32)
    e_tot = jnp.dot(jnp.broadcast_to(jnp.exp(tot), (SUBLANES, LANES)), expand.astype(F32),
                    preferred_element_type=F32, precision=HIGHEST)[0:1, :]
    xdt_b = xdt.astype(BF16)
    xdec_b = (xdt * e_out).astype(BF16)
    lane_head = lax.broadcasted_iota(jnp.int32, (t, gw), 1) // SSD_HEADDIM
    zero_b = jnp.zeros((t, gw), BF16)

    ys = []
    for g in range(SSD_GROUPS):
        bm = u[:, BRANCH_W + g * SSD_STATE:BRANCH_W + (g + 1) * SSD_STATE].astype(BF16)
        cm = u[:, BRANCH_W + (SSD_GROUPS + g) * SSD_STATE:BRANCH_W + (SSD_GROUPS + g + 1) * SSD_STATE].astype(BF16)
        cb = lax.dot_general(cm, bm, (((1,), (1,)), ((), ())), preferred_element_type=F32)
        st_g = st_ref[:, g * gw:(g + 1) * gw]
        y = jnp.dot(cm, st_g.astype(BF16), preferred_element_type=F32) * e_in[:, g * gw:(g + 1) * gw]
        xg = xdt_b[:, g * gw:(g + 1) * gw]
        for rr in range(SSD_HPG):
            col = direction * SSD_HEADS + g * SSD_HPG + rr
            decay = jnp.where(keep, jnp.exp(cs_c[:, col:col + 1] - cs_r[col:col + 1, :]), 0.0)
            gm = (cb * decay).astype(BF16)
            y = y + jnp.dot(gm, jnp.where(lane_head == rr, xg, zero_b), preferred_element_type=F32)
        ys.append(y)
        upd = lax.dot_general(bm, xdec_b[:, g * gw:(g + 1) * gw], (((0,), (0,)), ((), ())),
                              preferred_element_type=F32)
        st_ref[:, g * gw:(g + 1) * gw] = st_g * e_tot[:, g * gw:(g + 1) * gw] + upd
    y = jnp.concatenate(ys, axis=1)

    if reverse:
        y = yf_ref[0] + y + skip_ref[...] * xs
        tz = y * _silu(z_ref[0].astype(F32))
        outs = []
        for g in range(SSD_GROUPS):
            tg = tz[:, g * gw:(g + 1) * gw]
            outs.append(tg * lax.rsqrt(jnp.mean(tg * tg, axis=1, keepdims=True) + RMS_EPS))
        o_ref[0] = (jnp.concatenate(outs, axis=1) * g_ref[...]).astype(BF16)
    else:
        o_ref[0] = y


def _ssd(xbc, dt, dtt, p, direction, yf=None, z=None):
    reverse = direction == 1
    nb, s, _ = xbc.shape
    nc = s // ROW_TILE
    cur, prev, nxt = _tile_specs(2 * BRANCH_W, nc, reverse)
    ch = lambda j: _scan_chunk(j, nc, reverse)
    params = [p["conv_w"], p["conv_b"], p["dtb_row"], p["dtb_col"], p["alog_row"], p["alog_col"], p["expand"]]
    in_specs = [cur, prev, nxt,
                pl.BlockSpec((1, ROW_TILE, LANES), lambda b, j: (b, ch(j), 0)),
                pl.BlockSpec((1, 2 * SSD_HEADS, ROW_TILE), lambda b, j: (b, 0, ch(j)))]
    in_specs += [_resident(a.shape) for a in params]
    args = [xbc, xbc, xbc, dt, dtt] + params
    half = pl.BlockSpec((1, ROW_TILE, BRANCH_W), lambda b, j: (b, ch(j), 0))
    if reverse:
        in_specs += [half, half, _resident(p["skip"].shape), _resident(p["norm_g"].shape)]
        args += [yf, z, p["skip"], p["norm_g"]]
    return pl.pallas_call(
        functools.partial(_ssd_kernel, reverse=reverse, nc=nc, direction=direction),
        grid=(nb, nc),
        in_specs=in_specs,
        out_specs=half,
        out_shape=jax.ShapeDtypeStruct((nb, s, BRANCH_W), BF16 if reverse else F32),
        scratch_shapes=[pltpu.VMEM((SSD_STATE, BRANCH_W), F32)],
        compiler_params=_cparams(("arbitrary", "arbitrary")),
    )(*args)


def _merge_kernel(ya_ref, yb_ref, yc_ref, gt_ref, x_ref, m_ref, wb_ref, wo_ref, g_ref, b_ref,
                  x1_ref, h2_ref, *, alpha):
    acc = None
    for n, y_ref in enumerate((ya_ref, yb_ref, yc_ref)):
        proj = jnp.dot(y_ref[0], wb_ref[n], preferred_element_type=F32)
        gate = jax.nn.sigmoid(gt_ref[0, :, n * D_MODEL:(n + 1) * D_MODEL].astype(F32))
        acc = gate * proj if acc is None else acc + gate * proj
    mix = jnp.dot(acc.astype(BF16), wo_ref[...], preferred_element_type=F32)
    x1 = _layer_norm(alpha * x_ref[0] + m_ref[0, 2:3, :] * mix, g_ref[...], b_ref[...])
    x1_ref[0] = x1
    _store_token_rows(h2_ref.at[0], x1 * (1.0 + m_ref[0, 4:5, :]) + m_ref[0, 3:4, :])


def _zero_tile_kernel(h_ref, o_ref):
    del h_ref
    o_ref[...] = jnp.zeros_like(o_ref)


def _merge(ya, yb, yc, gates, xc, mods, w_branch, w_out, ln_g, ln_b, alpha):
    nb, s, _ = xc.shape
    nt = s // ROW_TILE
    row = lambda width: pl.BlockSpec((1, ROW_TILE, width), lambda b, i: (b, i, 0))
    h2_shape = jax.ShapeDtypeStruct((nb, (s + MOE_PAD) * NCHUNK, LANES), F32)
    x1, h2 = pl.pallas_call(
        functools.partial(_merge_kernel, alpha=alpha),
        grid=(nb, nt),
        in_specs=[row(BRANCH_W), row(BRANCH_W), row(BRANCH_W), row(3 * D_MODEL), row(D_MODEL), _mod_spec(nb),
                  _resident(w_branch.shape), _resident(w_out.shape), _resident(ln_g.shape), _resident(ln_b.shape)],
        out_specs=[row(D_MODEL), pl.BlockSpec((1, ROW_TILE * NCHUNK, LANES), lambda b, i: (b, i, 0))],
        out_shape=[jax.ShapeDtypeStruct((nb, s, D_MODEL), F32), h2_shape],
        compiler_params=_cparams(("arbitrary", "arbitrary")),
    )(ya, yb, yc, gates, xc, mods, w_branch, w_out, ln_g, ln_b)
    h2 = pl.pallas_call(
        _zero_tile_kernel,
        grid=(nb,),
        in_specs=[pl.BlockSpec(memory_space=pl.ANY)],
        out_specs=pl.BlockSpec((1, NCHUNK, LANES), lambda b: (b, s, 0)),
        out_shape=h2_shape,
        input_output_aliases={0: 0},
        compiler_params=_cparams(("arbitrary",)),
    )(h2)
    return x1, h2


def _first_index(hit, ridx, n):
    return jnp.min(jnp.where(hit, ridx, n), axis=0, keepdims=True)


def _route_kernel(h_ref, wr_ref, bias_ref, cnt_ref, rank_ref, wgt_ref, base_ref, *, latent_only):
    tm = wgt_ref.shape[2]

    @pl.when(pl.program_id(1) == 0)
    def _():
        base_ref[...] = jnp.zeros_like(base_ref)

    logits = lax.dot_general(wr_ref[...], _load_token_rows(h_ref.at[0], tm), (((1,), (1,)), ((), ())),
                             preferred_element_type=F32, precision=HIGHEST)
    scores = jax.nn.sigmoid(logits)
    sel = scores + bias_ref[...]
    neg = -jnp.inf

    r8 = lax.broadcasted_iota(jnp.int32, (GROUP_SIZE, tm), 0)
    grp = jnp.zeros((N_GROUPS, tm), F32)
    for g in range(N_GROUPS):
        blk = sel[g * GROUP_SIZE:(g + 1) * GROUP_SIZE, :]
        m1 = jnp.max(blk, axis=0, keepdims=True)
        first = _first_index(blk == m1, r8, GROUP_SIZE)
        m2 = jnp.max(jnp.where(r8 == first, neg, blk), axis=0, keepdims=True)
        grp = jnp.where(r8 == g, m1 + m2, grp)
    gsel = jnp.zeros((N_GROUPS, tm), jnp.int32)
    for _ in range(TOPK_GROUPS):
        m = jnp.max(grp, axis=0, keepdims=True)
        hit = r8 == _first_index(grp == m, r8, N_GROUPS)
        gsel = jnp.where(hit, 1, gsel)
        grp = jnp.where(hit, neg, grp)
    cand = jnp.concatenate(
        [jnp.where(gsel[g:g + 1, :] > 0, sel[g * GROUP_SIZE:(g + 1) * GROUP_SIZE, :], neg) for g in range(N_GROUPS)],
        axis=0)

    re = lax.broadcasted_iota(jnp.int32, (N_EXPERTS, tm), 0)
    chosen = jnp.zeros((N_EXPERTS, tm), jnp.int32)
    for _ in range(TOP_K):
        m = jnp.max(cand, axis=0, keepdims=True)
        hit = re == _first_index(cand == m, re, N_EXPERTS)
        chosen = jnp.where(hit, 1, chosen)
        cand = jnp.where(hit, neg, cand)
    picked = jnp.where(chosen > 0, scores, 0.0)
    wgt = picked / jnp.sum(picked, axis=0, keepdims=True) * ROUTED_SCALE
    if latent_only:
        chosen = jnp.where(pl.program_id(1) == 0, 0, chosen)

    li = lax.broadcasted_iota(jnp.int32, (LANES, LANES), 0)
    lj = lax.broadcasted_iota(jnp.int32, (LANES, LANES), 1)
    strict = (li < lj).astype(BF16)
    chosen_b = chosen.astype(F32).astype(BF16)
    base = base_ref[...]
    pos = []
    for kb in range(tm // LANES):
        cblk = chosen_b[:, kb * LANES:(kb + 1) * LANES]
        pos.append(jnp.dot(cblk, strict, preferred_element_type=F32) + base)
        base = base + jnp.sum(cblk.astype(F32), axis=1, keepdims=True)
    base_ref[...] = base
    cnt_ref[0] = base.astype(jnp.int32)
    rank_ref[0] = jnp.where(chosen > 0, jnp.concatenate(pos, axis=1).astype(jnp.int32), -1)
    wgt_ref[0] = wgt


def _compact_kernel(cnt_ref, rank_ref, wgt_ref, idx_ref, wl_ref):
    rank = rank_ref[0]
    ne, tm = rank.shape
    lane = lax.broadcasted_iota(jnp.int32, (ne, tm), 1)
    live = jnp.where(rank >= 0, 1, 0)
    dist = lane - rank
    val = lane
    wv = wgt_ref[0]
    step = 1
    while step < tm:
        move = jnp.where(jnp.logical_and(live > 0, (dist & step) != 0), 1, 0)
        sh = tm - step
        take = pltpu.roll(move, sh, 1) > 0
        val = jnp.where(take, pltpu.roll(val, sh, 1), val)
        wv = jnp.where(take, pltpu.roll(wv, sh, 1), wv)
        dist = jnp.where(take, pltpu.roll(dist, sh, 1), dist)
        live = jnp.where(take, 1, jnp.where(move > 0, 0, live))
        step *= 2
    valid = lane < cnt_ref[0]
    spare = jnp.full((ne, MOE_SPARE), tm * NCHUNK, jnp.int32)
    idx_ref[0] = jnp.concatenate([jnp.where(valid, val * NCHUNK, tm * NCHUNK), spare], axis=1)
    wl_ref[0] = jnp.where(valid, wv, 0.0)


def _router(h2, s, w_router_t, bias_col, latent_only):
    nb = h2.shape[0]
    dense = pl.BlockSpec((1, N_EXPERTS, ROW_TILE), lambda b, i: (b, 0, i))
    counts, rank, wgt = pl.pallas_call(
        functools.partial(_route_kernel, latent_only=latent_only),
        grid=(nb, s // ROW_TILE),
        in_specs=[pl.BlockSpec((1, ROW_TILE * NCHUNK, LANES), lambda b, i: (b, i, 0)),
                  _resident(w_router_t.shape), _resident(bias_col.shape)],
        out_specs=[pl.BlockSpec((1, N_EXPERTS, 1), lambda b, i: (b, 0, 0)), dense, dense],
        out_shape=[jax.ShapeDtypeStruct((nb, N_EXPERTS, 1), jnp.int32),
                   jax.ShapeDtypeStruct((nb, N_EXPERTS, s), jnp.int32),
                   jax.ShapeDtypeStruct((nb, N_EXPERTS, s), F32)],
        scratch_shapes=[pltpu.VMEM((N_EXPERTS, 1), F32)],
        compiler_params=_cparams(("arbitrary", "arbitrary")),
    )(h2, w_router_t, bias_col)
    rows = lambda width: pl.BlockSpec((1, SUBLANES, width), lambda b, g: (b, g, 0))
    idx, wl = pl.pallas_call(
        _compact_kernel,
        grid=(nb, N_EXPERTS // SUBLANES),
        in_specs=[rows(1), rows(s), rows(s)],
        out_specs=[rows(s + MOE_SPARE), rows(s)],
        out_shape=[jax.ShapeDtypeStruct((nb, N_EXPERTS, s + MOE_SPARE), jnp.int32),
                   jax.ShapeDtypeStruct((nb, N_EXPERTS, s), F32)],
        compiler_params=_cparams(("arbitrary", "arbitrary")),
    )(counts, rank, wgt)
    return counts, idx, wl


SCATTER_GROUP = 16


def _experts_kernel(cnt_ref, idx_ref, wl_ref, h_ref, wu_ref, wd_ref, o_ref,
                    xa_ref, xb_ref, xc_ref, ya_ref, yb_ref, yc_ref):
    t = pl.program_id(0)
    e = pl.program_id(1)
    h_rows = h_ref.at[0]
    acc_ref = o_ref.at[0]

    @pl.when(e == 0)
    def _():
        o_ref[...] = jnp.zeros_like(o_ref)

    count = cnt_ref[t * N_EXPERTS + e]
    nblk = (count + MOE_BLOCK - 1) // MOE_BLOCK
    ri = lax.broadcasted_iota(jnp.int32, (MOE_BLOCK, MOE_BLOCK), 0)
    ci = lax.broadcasted_iota(jnp.int32, (MOE_BLOCK, MOE_BLOCK), 1)

    def gather(blk, x_ref):
        ids = idx_ref.at[0, 0, pl.ds(blk * MOE_BLOCK, MOE_BLOCK)]
        for r in range(MOE_BLOCK):
            x_ref[pl.ds(r * NCHUNK, NCHUNK), :] = h_rows[pl.ds(pl.multiple_of(ids[r], NCHUNK), NCHUNK), :]

    def swiglu(x, blks, y_refs):
        gu = jnp.dot(x.astype(BF16), wu_ref[0], preferred_element_type=F32)
        act = _silu(gu[:, :EXPERT_F]) * gu[:, EXPERT_F:]
        y = jnp.dot(act.astype(BF16), wd_ref[0], preferred_element_type=F32)
        for k, (blk, y_ref) in enumerate(zip(blks, y_refs)):
            w_row = wl_ref[0, pl.ds(blk, 1), :]
            w_col = jnp.sum(jnp.where(ri == ci, w_row, 0.0), axis=1, keepdims=True)
            _store_token_rows(y_ref, y[k * MOE_BLOCK:(k + 1) * MOE_BLOCK, :] * w_col)

    def scatter(blk, y_ref):
        ids = idx_ref.at[0, 0, pl.ds(blk * MOE_BLOCK, MOE_BLOCK)]
        for g0 in range(0, MOE_BLOCK, SCATTER_GROUP):
            rows = range(g0, g0 + SCATTER_GROUP)
            dst = [pl.multiple_of(ids[r], NCHUNK) for r in rows]
            new = [acc_ref[pl.ds(d, NCHUNK), :] + y_ref[pl.ds(r * NCHUNK, NCHUNK), :] for d, r in zip(dst, rows)]
            for d, v in zip(dst, new):
                acc_ref[pl.ds(d, NCHUNK), :] = v

    odd = nblk % 2 == 1
    nloop = jnp.where(jnp.logical_and(odd, nblk >= 3), nblk // 2 - 1, nblk // 2)

    @pl.when(nblk > 0)
    def _():
        gather(0, xa_ref)

    @pl.when(nblk >= 2)
    def _():
        gather(1, xb_ref)
        ya_ref[...] = jnp.zeros_like(ya_ref)
        yb_ref[...] = jnp.zeros_like(yb_ref)

    def load_rows(*x_refs):
        return jnp.concatenate([_load_token_rows(r, MOE_BLOCK) for r in x_refs], axis=0)

    def pair(j, carry):
        b0 = 2 * j
        x = load_rows(xa_ref, xb_ref)
        scatter(jnp.maximum(b0 - 2, 0), ya_ref)
        scatter(jnp.maximum(b0 - 1, 0), yb_ref)
        gather(b0 + 2, xa_ref)
        gather(b0 + 3, xb_ref)
        swiglu(x, (b0, b0 + 1), (ya_ref, yb_ref))
        return carry

    lax.fori_loop(0, nloop, pair, 0)
    b0 = 2 * nloop

    @pl.when(jnp.logical_and(odd, nblk >= 3))
    def _():
        gather(b0 + 2, xc_ref)
        x = load_rows(xa_ref, xb_ref, xc_ref)
        scatter(jnp.maximum(b0 - 2, 0), ya_ref)
        scatter(jnp.maximum(b0 - 1, 0), yb_ref)
        swiglu(x, (b0, b0 + 1, b0 + 2), (ya_ref, yb_ref, yc_ref))
        scatter(b0, ya_ref)
        scatter(b0 + 1, yb_ref)
        scatter(b0 + 2, yc_ref)

    @pl.when(jnp.logical_and(jnp.logical_not(odd), nblk > 0))
    def _():
        scatter(nblk - 2, ya_ref)
        scatter(nblk - 1, yb_ref)

    @pl.when(nblk == 1)
    def _():
        swiglu(load_rows(xa_ref), (0,), (ya_ref,))
        scatter(0, ya_ref)


def _experts(h2, counts, idx, wl, w_up, w_down):
    nb, rows, _ = h2.shape
    nlist = idx.shape[-1]
    tile = pl.BlockSpec((1, rows, LANES), lambda t, e, c: (t, 0, 0), pipeline_mode=pl.Buffered(1))
    buf = pltpu.VMEM((MOE_BLOCK * NCHUNK, LANES), F32)
    grid_spec = pltpu.PrefetchScalarGridSpec(
        num_scalar_prefetch=1,
        grid=(nb, N_EXPERTS),
        in_specs=[pl.BlockSpec((1, 1, nlist), lambda t, e, c: (t * N_EXPERTS + e, 0, 0), memory_space=pltpu.SMEM),
                  pl.BlockSpec((1,) + wl.shape[1:], lambda t, e, c: (t * N_EXPERTS + e, 0, 0)),
                  tile,
                  pl.BlockSpec((1, D_MODEL, 2 * EXPERT_F), lambda t, e, c: (e, 0, 0)),
                  pl.BlockSpec((1, EXPERT_F, D_MODEL), lambda t, e, c: (e, 0, 0))],
        out_specs=tile,
        scratch_shapes=[buf] * 6,
    )
    return pl.pallas_call(
        _experts_kernel,
        grid_spec=grid_spec,
        out_shape=jax.ShapeDtypeStruct(h2.shape, F32),
        compiler_params=_cparams(("arbitrary", "arbitrary")),
    )(counts, idx, wl, h2, w_up, w_down)


def _ffn_out_kernel(x1_ref, h2_ref, fr_ref, m_ref, wu_ref, wd_ref, g_ref, b_ref, o_ref, *, alpha):
    gu = jnp.dot(_load_token_rows(h2_ref.at[0], ROW_TILE).astype(BF16), wu_ref[...], preferred_element_type=F32)
    act = _silu(gu[:, :EXPERT_F]) * gu[:, EXPERT_F:]
    f = (jnp.dot(act.astype(BF16), wd_ref[...], preferred_element_type=F32)
         + _load_token_rows(fr_ref.at[0], ROW_TILE))
    o_ref[0] = _layer_norm(alpha * x1_ref[0] + m_ref[0, 5:6, :] * f, g_ref[...], b_ref[...])


def _ffn_out(x1, h2, fr, mods, ws_up, ws_down, ln_g, ln_b, alpha, latent_only):
    nb, s, _ = x1.shape
    nt = s // ROW_TILE
    skip = CTX_LEN // ROW_TILE if latent_only else 0
    row = pl.BlockSpec((1, ROW_TILE, D_MODEL), lambda b, i: (b, i + skip, 0))
    mod = pl.BlockSpec((1, 6, D_MODEL), lambda b, i: (jnp.where(i + skip == 0, nb, b), 0, 0))
    chunked = pl.BlockSpec((1, ROW_TILE * NCHUNK, LANES), lambda b, i: (b, i + skip, 0))
    return pl.pallas_call(
        functools.partial(_ffn_out_kernel, alpha=alpha),
        grid=(nb, nt - skip),
        in_specs=[row, chunked, chunked, mod, _resident(ws_up.shape), _resident(ws_down.shape),
                  _resident(ln_g.shape), _resident(ln_b.shape)],
        out_specs=pl.BlockSpec((1, ROW_TILE, D_MODEL), lambda b, i: (b, i, 0)),
        out_shape=jax.ShapeDtypeStruct((nb, s - skip * ROW_TILE, D_MODEL), F32),
        compiler_params=_cparams(("arbitrary", "arbitrary")),
    )(x1, h2, fr, mods, ws_up, ws_down, ln_g, ln_b)


def _rope_tables(n_lat):
    t = jnp.arange(n_lat)
    rowp = (t // GRID_W).astype(F32)
    colp = (t % GRID_W).astype(F32)
    n_freq = ATT_DH // 4
    inv = ROPE_BASE ** (-jnp.arange(n_freq, dtype=F32) / n_freq)
    ang = jnp.concatenate([rowp[:, None] * inv, colp[:, None] * inv], axis=-1)
    lane = jnp.arange(LANES)
    cos = jnp.cos(ang)[:, lane % (ATT_DH // 2)]
    sign = jnp.where((lane % ATT_DH) < ATT_DH // 2, -1.0, 1.0).astype(F32)
    sin = jnp.sin(ang)[:, lane % (ATT_DH // 2)] * sign
    cos = jnp.concatenate([jnp.ones((CTX_LEN, LANES), F32), cos], axis=0)
    sin = jnp.concatenate([jnp.zeros((CTX_LEN, LANES), F32), sin], axis=0)
    return cos, sin


def _block_diag(w):
    n, k, _ = w.shape
    eye = jnp.eye(n, dtype=w.dtype)
    return (eye[:, None, :, None] * w[:, :, None, :]).reshape(n * k, n * k)


def _inproj_weights(w):
    o_v = 2 * BRANCH_W
    o_lx = o_v + ATT_HEADS * ATT_DV
    o_z = o_lx + 2 * BRANCH_W
    o_xbc = o_z + BRANCH_W
    o_dt = o_xbc + 2 * BRANCH_W
    o_g = o_dt + 2 * SSD_HEADS
    wv = w[:, o_v:o_lx].T.reshape(ATT_HEADS, ATT_DV, D_MODEL)
    wv = jnp.pad(wv, ((0, 0), (0, VT_ROWS - ATT_DV), (0, 0))).reshape(ATT_HEADS * VT_ROWS, D_MODEL)
    ones = jnp.zeros((ATT_HEADS, VT_ROWS, 1), F32).at[:, ATT_DV, 0].set(1.0).reshape(ATT_HEADS * VT_ROWS, 1)
    wdt = w[:, o_dt:o_g]
    return {
        "qk": w[:, :o_v].astype(BF16),
        "vt": wv.astype(BF16),
        "ones": ones,
        "lxg": w[:, o_lx:o_z].astype(BF16),
        "z": w[:, o_z:o_xbc].astype(BF16),
        "xbc": w[:, o_xbc:o_dt].astype(BF16),
        "dt": jnp.pad(wdt, ((0, 0), (0, LANES - 2 * SSD_HEADS))).astype(BF16),
        "dtt": wdt.T.astype(BF16),
        "g": w[:, o_g:].astype(BF16),
    }


def _pad_row(v):
    return jnp.pad(v.reshape(1, -1), ((0, 0), (0, LANES - v.size)))


def kernel(x, c, ctx, c_ctx, w_mod, b_mod, w_in, lam_q, lam_k, attn_norm_g, lru_conv_w, lru_conv_b, lru_wa, lru_ba, lru_wi, lru_bi, lru_lambda, ssd_conv_w, ssd_conv_b, ssd_dt_bias, ssd_a_log, ssd_d, ssd_norm_g, w_branch, w_out, ln1_g, ln1_b, w_router, router_bias, w_up, w_down, ws_up, ws_down, ln2_g, ln2_b):
    nb, n_lat, _ = x.shape
    depth = w_mod.shape[0]
    assert ctx.shape[1] == CTX_LEN and n_lat % ROW_TILE == 0 and nb + 1 <= 16
    s = CTX_LEN + n_lat
    alpha = (2 * depth) ** 0.25

    xc = jnp.concatenate([ctx, x], axis=1)
    cc = jnp.zeros((16, D_MODEL), F32).at[:nb].set(c).at[nb].set(c_ctx)
    mods_all = _modulation(cc, w_mod, b_mod).reshape(depth, 16, 6, D_MODEL)
    cos_t, sin_t = _rope_tables(n_lat)
    head_of_channel = jnp.arange(BRANCH_W) // SSD_HEADDIM

    for l in range(depth):
        last = l == depth - 1
        lam_init = 0.8 - 0.6 * math.exp(-0.3 * l)
        mods = mods_all[l]
        qk, vt, lxg, z, xbc, dt, dtt, gates = _inproj(xc, mods, cos_t, sin_t, _inproj_weights(w_in[l]))

        ya = _attention(qk, vt, lam_q[l], lam_k[l], attn_norm_g[l], lam_init)

        yb = None
        for d in range(2):
            w_gate = jnp.concatenate([_block_diag(lru_wa[l, d]), _block_diag(lru_wi[l, d])], axis=1).astype(BF16)
            yb = _rglru(lxg, lru_conv_w[l], lru_conv_b[l].reshape(1, -1), w_gate,
                        lru_ba[l, d].reshape(1, -1), lru_bi[l, d].reshape(1, -1), lru_lambda[l, d].reshape(1, -1),
                        reverse=(d == 1), yf=yb)

        ssd_p = {
            "conv_w": ssd_conv_w[l], "conv_b": ssd_conv_b[l].reshape(1, -1),
            "dtb_row": _pad_row(ssd_dt_bias[l]), "dtb_col": ssd_dt_bias[l].reshape(-1, 1),
            "alog_row": _pad_row(ssd_a_log[l]), "alog_col": ssd_a_log[l].reshape(-1, 1),
            "skip": jnp.repeat(ssd_d[l], SSD_HEADDIM).reshape(1, -1), "norm_g": ssd_norm_g[l].reshape(1, -1),
        }
        yc = None
        for d in range(2):
            ssd_p["expand"] = (jnp.arange(LANES)[:, None] == d * SSD_HEADS + head_of_channel[None, :]).astype(BF16)
            yc = _ssd(xbc, dt, dtt, ssd_p, d, yf=yc, z=z)

        x1, h2 = _merge(ya, yb, yc, gates, xc, mods, w_branch[l].astype(BF16), w_out[l].astype(BF16),
                        ln1_g[l].reshape(1, -1), ln1_b[l].reshape(1, -1), alpha)

        counts, idx, wl = _router(h2, s, w_router[l].T, router_bias[l].reshape(-1, 1), latent_only=last)
        fr = _experts(h2, counts.reshape(nb * N_EXPERTS), idx.reshape(nb * N_EXPERTS, 1, s + MOE_SPARE),
                      wl.reshape(nb * N_EXPERTS, s // MOE_BLOCK, MOE_BLOCK),
                      w_up[l].astype(BF16), w_down[l].astype(BF16))
        xc = _ffn_out(x1, h2, fr, mods, ws_up[l].astype(BF16), ws_down[l].astype(BF16),
                      ln2_g[l].reshape(1, -1), ln2_b[l].reshape(1, -1), alpha, latent_only=last)
    return xc
```

```python
import functools
import math

import jax
import jax.numpy as jnp
from jax import lax
from jax.experimental import pallas as pl
from jax.experimental.pallas import tpu as pltpu

F32 = jnp.float32
BF16 = jnp.bfloat16
HIGHEST = lax.Precision.HIGHEST

D_MODEL = 1024
GRID_W = 64
CTX_LEN = 256
BRANCH_W = 512
ATT_HEADS = 4
ATT_DH = 64
ATT_DV = 128
ROPE_BASE = 10000.0
LRU_C = 8.0
SSD_HEADS = 8
SSD_HEADDIM = 64
SSD_HPG = 4
SSD_GROUPS = 2
SSD_STATE = 128
N_EXPERTS = 64
N_GROUPS = 8
GROUP_SIZE = N_EXPERTS // N_GROUPS
TOP_K = 8
TOPK_GROUPS = 4
EXPERT_F = 256
ROUTED_SCALE = 2.5
LN_EPS = 1e-5
RMS_EPS = 1e-6

ROW_TILE = 256
HALO = 16
VT_ROWS = 144
KEY_CHUNK = 256
SCORE_CHUNKS = 2
MOE_BLOCK = 128
MOE_PAD = ROW_TILE
MOE_SPARE = 2 * MOE_BLOCK
SUBLANES = 8
LANES = 128
VMEM_LIMIT = 56 * 1024 * 1024
LOG2E = 1.4426950408889634


def _cparams(sem):
    return pltpu.CompilerParams(dimension_semantics=sem, vmem_limit_bytes=VMEM_LIMIT)


def _resident(shape):
    nd = len(shape)
    return pl.BlockSpec(shape, lambda *_: (0,) * nd, pipeline_mode=pl.Buffered(1))


def _silu(x):
    return x * jax.nn.sigmoid(x)


def _softplus(x):
    return jnp.maximum(x, 0.0) + jnp.log1p(jnp.exp(-jnp.abs(x)))


def _gelu_tanh(x):
    return 0.5 * x * (1.0 + jnp.tanh(math.sqrt(2.0 / math.pi) * (x + 0.044715 * (x * x * x))))


NCHUNK = D_MODEL // LANES


def _load_token_rows(ref, n, first=0):
    return jnp.concatenate([ref[pl.ds(first * NCHUNK + s, n, stride=NCHUNK), :] for s in range(NCHUNK)], axis=1)


def _store_token_rows(ref, x):
    n = x.shape[0]
    for s in range(NCHUNK):
        ref[pl.ds(s, n, stride=NCHUNK), :] = x[:, s * LANES:(s + 1) * LANES]


def _layer_norm(x, g, b):
    mu = jnp.mean(x, axis=-1, keepdims=True)
    xc = x - mu
    var = jnp.mean(xc * xc, axis=-1, keepdims=True)
    return xc * lax.rsqrt(var + LN_EPS) * g + b


def _mod_kernel(c_ref, w_ref, b_ref, o_ref):
    s = _silu(c_ref[...])
    o_ref[0] = jnp.dot(s, w_ref[0], preferred_element_type=F32, precision=HIGHEST) + b_ref[0]


def _modulation(cc, w_mod, b_mod):
    depth = w_mod.shape[0]
    nblk = 6
    return pl.pallas_call(
        _mod_kernel,
        grid=(depth, nblk),
        in_specs=[
            pl.BlockSpec((16, D_MODEL), lambda l, j: (0, 0)),
            pl.BlockSpec((1, D_MODEL, D_MODEL), lambda l, j: (l, 0, j)),
            pl.BlockSpec((1, 1, D_MODEL), lambda l, j: (l, 0, j)),
        ],
        out_specs=pl.BlockSpec((1, 16, D_MODEL), lambda l, j: (l, 0, j)),
        out_shape=jax.ShapeDtypeStruct((depth, 16, nblk * D_MODEL), F32),
        compiler_params=_cparams(("arbitrary", "arbitrary")),
    )(cc, w_mod, b_mod.reshape(depth, 1, nblk * D_MODEL))


def _mod_spec(nb):
    return pl.BlockSpec((1, 6, D_MODEL), lambda b, i: (jnp.where(i == 0, nb, b), 0, 0))


def _inproj_kernel(x_ref, m_ref, cos_ref, sin_ref, wqk_ref, wvt_ref, ones_ref, wlxg_ref, wz_ref,
                   wxbc_ref, wdt_ref, wdtt_ref, wg_ref,
                   qk_ref, vt_ref, lxg_ref, z_ref, xbc_ref, dt_ref, dtt_ref, g_ref):
    x = x_ref[0]
    h = (x * (1.0 + m_ref[0, 1:2, :]) + m_ref[0, 0:1, :]).astype(BF16)
    nt = (((1,), (1,)), ((), ()))

    qk = jnp.dot(h, wqk_ref[...], preferred_element_type=F32)
    cos = cos_ref[...]
    sin = sin_ref[...]
    lane = lax.broadcasted_iota(jnp.int32, cos.shape, 1)
    first_half = (lane % ATT_DH) < (ATT_DH // 2)
    for j in range(2 * ATT_HEADS):
        blk = qk[:, j * LANES:(j + 1) * LANES]
        partner = jnp.where(first_half, pltpu.roll(blk, LANES - ATT_DH // 2, 1),
                            pltpu.roll(blk, ATT_DH // 2, 1))
        r = blk * cos + partner * sin
        if j < ATT_HEADS:
            r = r * (ATT_DH ** -0.5 * LOG2E)
        qk_ref[0, :, j * LANES:(j + 1) * LANES] = r.astype(BF16)

    vt = lax.dot_general(wvt_ref[...], h, nt, preferred_element_type=F32) + ones_ref[...]
    vt_ref[0] = vt.astype(BF16)
    lxg_ref[0] = jnp.dot(h, wlxg_ref[...], preferred_element_type=F32).astype(BF16)
    z_ref[0] = jnp.dot(h, wz_ref[...], preferred_element_type=F32).astype(BF16)
    xbc_ref[0] = jnp.dot(h, wxbc_ref[...], preferred_element_type=F32).astype(BF16)
    dt_ref[0] = jnp.dot(h, wdt_ref[...], preferred_element_type=F32)
    dtt_ref[0] = lax.dot_general(wdtt_ref[...], h, nt, preferred_element_type=F32)
    g_ref[0] = jnp.dot(h, wg_ref[...], preferred_element_type=F32).astype(BF16)


def _inproj(xc, mods, cos_t, sin_t, w):
    nb, s, _ = xc.shape
    nt = s // ROW_TILE
    row = lambda width: pl.BlockSpec((1, ROW_TILE, width), lambda b, i: (b, i, 0))
    col = lambda rows: pl.BlockSpec((1, rows, ROW_TILE), lambda b, i: (b, 0, i))
    vt_rows = ATT_HEADS * VT_ROWS
    outs = [
        (jax.ShapeDtypeStruct((nb, s, 2 * BRANCH_W), BF16), row(2 * BRANCH_W)),
        (jax.ShapeDtypeStruct((nb, vt_rows, s), BF16), col(vt_rows)),
        (jax.ShapeDtypeStruct((nb, s, 2 * BRANCH_W), BF16), row(2 * BRANCH_W)),
        (jax.ShapeDtypeStruct((nb, s, BRANCH_W), BF16), row(BRANCH_W)),
        (jax.ShapeDtypeStruct((nb, s, 2 * BRANCH_W), BF16), row(2 * BRANCH_W)),
        (jax.ShapeDtypeStruct((nb, s, LANES), F32), row(LANES)),
        (jax.ShapeDtypeStruct((nb, 2 * SSD_HEADS, s), F32), col(2 * SSD_HEADS)),
        (jax.ShapeDtypeStruct((nb, s, 3 * D_MODEL), BF16), row(3 * D_MODEL)),
    ]
    weights = [w["qk"], w["vt"], w["ones"], w["lxg"], w["z"], w["xbc"], w["dt"], w["dtt"], w["g"]]
    return pl.pallas_call(
        _inproj_kernel,
        grid=(nb, nt),
        in_specs=[row(D_MODEL), _mod_spec(nb),
                  pl.BlockSpec((ROW_TILE, LANES), lambda b, i: (i, 0)),
                  pl.BlockSpec((ROW_TILE, LANES), lambda b, i: (i, 0))]
                 + [_resident(a.shape) for a in weights],
        out_specs=[o[1] for o in outs],
        out_shape=[o[0] for o in outs],
        compiler_params=_cparams(("arbitrary", "arbitrary")),
    )(xc, mods, cos_t, sin_t, *weights)


def _attn_kernel(lq_ref, lk_ref, g_ref, q_ref, qn_ref, k_ref, vt_ref, o_ref, sa_ref, sb_ref, m_ref, *, lam_init):
    i = pl.program_id(2)
    tq = q_ref.shape[1]
    nck = k_ref.shape[1] // KEY_CHUNK
    prod = lq_ref[...] * lk_ref[...]
    d0 = jnp.sum(prod[0:1, :], axis=1, keepdims=True)
    d1 = jnp.sum(prod[1:2, :], axis=1, keepdims=True)
    lam = jnp.exp(d0) - jnp.exp(d1) + lam_init
    gcol = g_ref[0] * (1.0 - lam_init)

    def stack_maps(q):
        lane = lax.broadcasted_iota(jnp.int32, q.shape, 1)
        zero = jnp.zeros_like(q)
        return jnp.concatenate([jnp.where(lane < ATT_DH, q, zero), jnp.where(lane >= ATT_DH, q, zero)], axis=0)

    def scores(c, q2, s_ref, m, n=1):
        rows = slice(c * KEY_CHUNK, (c + n) * KEY_CHUNK)
        st = lax.dot_general(k_ref[0, rows, :], q2, (((1,), (1,)), ((), ())),
                             preferred_element_type=F32)
        s_ref[rows, :] = st
        mc = jnp.max(st, axis=0, keepdims=True)
        return mc if m is None else jnp.maximum(m, mc)

    def weigh(c, s_ref, m, acc):
        rows = slice(c * KEY_CHUNK, (c + 1) * KEY_CHUNK)
        e = jnp.exp2((s_ref[rows, :] - m).astype(BF16))
        pv = jnp.dot(vt_ref[0, :, rows], e, preferred_element_type=F32)
        return pv if acc is None else acc + pv

    def finish(acc):
        r = 1.0 / acc[ATT_DV:ATT_DV + 1, :]
        o = acc[:ATT_DV, :tq] * r[:, :tq] - lam * (acc[:ATT_DV, tq:] * r[:, tq:])
        ms = jnp.mean(o * o, axis=0, keepdims=True)
        o_ref[0] = (o * lax.rsqrt(ms + RMS_EPS) * gcol).T.astype(BF16)

    @pl.when(i == 0)
    def _():
        q2 = stack_maps(q_ref[0])
        finish(weigh(0, sa_ref, scores(0, q2, sa_ref, None), None))
        q2n = stack_maps(qn_ref[0])
        mn = None
        for c in range(0, nck, SCORE_CHUNKS):
            mn = scores(c, q2n, sb_ref, mn, min(SCORE_CHUNKS, nck - c))
        m_ref[1:2, :] = mn

    def step(cur_ref, nxt_ref, cur_slot, nxt_slot):
        q2n = stack_maps(qn_ref[0])
        m_cur = m_ref[cur_slot:cur_slot + 1, :]
        mn = acc = None
        for c in range(nck):
            if c % SCORE_CHUNKS == 0:
                mn = scores(c, q2n, nxt_ref, mn, min(SCORE_CHUNKS, nck - c))
            acc = weigh(c, cur_ref, m_cur, acc)
        finish(acc)
        m_ref[nxt_slot:nxt_slot + 1, :] = mn

    @pl.when(i % 2 == 1)
    def _():
        step(sb_ref, sa_ref, 1, 0)

    @pl.when(jnp.logical_and(i % 2 == 0, i > 0))
    def _():
        step(sa_ref, sb_ref, 0, 1)


def _attention(qk, vt, lam_q, lam_k, attn_g, lam_init):
    nb, s, _ = qk.shape
    nq = s // ROW_TILE
    return pl.pallas_call(
        functools.partial(_attn_kernel, lam_init=lam_init),
        grid=(nb, ATT_HEADS, nq),
        in_specs=[
            pl.BlockSpec((2, ATT_DH), lambda b, h, i: (0, 0)),
            pl.BlockSpec((2, ATT_DH), lambda b, h, i: (0, 0)),
            pl.BlockSpec((1, ATT_DV, 1), lambda b, h, i: (h, 0, 0)),
            pl.BlockSpec((1, ROW_TILE, LANES), lambda b, h, i: (b, i, h)),
            pl.BlockSpec((1, ROW_TILE, LANES), lambda b, h, i: (b, jnp.minimum(i + 1, nq - 1), h)),
            pl.BlockSpec((1, s, LANES), lambda b, h, i: (b, 0, ATT_HEADS + h)),
            pl.BlockSpec((1, VT_ROWS, s), lambda b, h, i: (b, h, 0)),
        ],
        out_specs=pl.BlockSpec((1, ROW_TILE, LANES), lambda b, h, i: (b, i, h)),
        out_shape=jax.ShapeDtypeStruct((nb, s, BRANCH_W), BF16),
        scratch_shapes=[pltpu.VMEM((s, 2 * ROW_TILE), F32), pltpu.VMEM((s, 2 * ROW_TILE), F32),
                        pltpu.VMEM((SUBLANES, 2 * ROW_TILE), F32)],
        compiler_params=_cparams(("arbitrary", "arbitrary", "arbitrary")),
    )(lam_q, lam_k, attn_g.reshape(ATT_HEADS, ATT_DV, 1), qk, qk, qk, vt)


def _scan_chunk(j, nc, reverse):
    if not reverse:
        return j
    return jnp.where(j == 0, 0, nc - j)


def _conv4(x_ref, xp_ref, xn_ref, c, nc, w_ref, b_ref):
    x = x_ref[0].astype(F32)
    t = x.shape[0]
    row = lax.broadcasted_iota(jnp.int32, (SUBLANES, x.shape[1]), 0)
    prev_ok = (c >= 2).astype(F32)
    next_ok = jnp.logical_and(c >= 1, c < nc - 1).astype(F32)
    prev = xp_ref[0].astype(F32)
    p1 = prev[HALO - 1:HALO, :] * prev_ok
    p2 = prev[HALO - 2:HALO - 1, :] * prev_ok
    n0 = xn_ref[0].astype(F32)[0:1, :] * next_ok

    def patch_head(a, fix):
        return jnp.concatenate([fix(a[:SUBLANES, :]), a[SUBLANES:, :]], axis=0)

    def patch_tail(a, fix):
        return jnp.concatenate([a[:t - SUBLANES, :], fix(a[t - SUBLANES:, :])], axis=0)

    xm1 = patch_head(pltpu.roll(x, 1, 0), lambda a: jnp.where(row == 0, p1, a))
    xm2 = patch_head(pltpu.roll(x, 2, 0), lambda a: jnp.where(row == 0, p2, jnp.where(row == 1, p1, a)))
    xp1 = patch_tail(pltpu.roll(x, t - 1, 0), lambda a: jnp.where(row == SUBLANES - 1, n0, a))
    return w_ref[0:1, :] * xm2 + w_ref[1:2, :] * xm1 + w_ref[2:3, :] * x + w_ref[3:4, :] * xp1 + b_ref[...]


def _tile_specs(width, nc, reverse, blk=0):
    per = ROW_TILE // HALO
    last = nc * per - 1
    ch = lambda j: _scan_chunk(j, nc, reverse)
    cur = pl.BlockSpec((1, ROW_TILE, width), lambda b, j: (b, ch(j), blk))
    prev = pl.BlockSpec((1, HALO, width), lambda b, j: (b, jnp.maximum(ch(j) * per - 1, 0), blk))
    nxt = pl.BlockSpec((1, HALO, width), lambda b, j: (b, jnp.minimum((ch(j) + 1) * per, last), blk))
    return cur, prev, nxt


def _rglru_kernel(*refs, reverse, nc):
    if reverse:
        (x_ref, xp_ref, xn_ref, cw_ref, cb_ref, wg_ref, ba_ref, bi_ref, lam_ref,
         yf_ref, lg_ref, o_ref, h_ref) = refs
    else:
        (x_ref, xp_ref, xn_ref, cw_ref, cb_ref, wg_ref, ba_ref, bi_ref, lam_ref, o_ref, h_ref) = refs
    j = pl.program_id(1)
    c = _scan_chunk(j, nc, reverse)

    @pl.when(j == 0)
    def _():
        h_ref[...] = jnp.zeros_like(h_ref)

    u = _conv4(x_ref, xp_ref, xn_ref, c, nc, cw_ref, cb_ref)
    t = u.shape[0]
    pre = jnp.dot(u.astype(BF16), wg_ref[...], preferred_element_type=F32)
    r = jax.nn.sigmoid(pre[:, :BRANCH_W] + ba_ref[...])
    gi = jax.nn.sigmoid(pre[:, BRANCH_W:] + bi_ref[...])
    log_a = (-LRU_C) * r * _softplus(-lam_ref[...])
    a = jnp.exp(log_a)
    var = 1.0 - jnp.exp(2.0 * log_a)
    bv = jnp.where(var > 0.0, var * lax.rsqrt(var), 0.0) * (gi * u)

    in_group = lax.broadcasted_iota(jnp.int32, u.shape, 0) % SUBLANES
    d = 1
    while d < SUBLANES:
        if reverse:
            keep = in_group < SUBLANES - d
            sh = t - d
        else:
            keep = in_group >= d
            sh = d
        a_s = jnp.where(keep, pltpu.roll(a, sh, 0), 1.0)
        b_s = jnp.where(keep, pltpu.roll(bv, sh, 0), 0.0)
        bv = a * b_s + bv
        a = a * a_s
        d *= 2
    carry = jnp.broadcast_to(h_ref[...], (SUBLANES, BRANCH_W))
    ngroup = t // SUBLANES
    pieces = [None] * ngroup
    for gidx in (range(ngroup - 1, -1, -1) if reverse else range(ngroup)):
        rows = slice(gidx * SUBLANES, (gidx + 1) * SUBLANES)
        pieces[gidx] = a[rows, :] * carry + bv[rows, :]
        edge = pieces[gidx][0:1, :] if reverse else pieces[gidx][SUBLANES - 1:SUBLANES, :]
        carry = jnp.broadcast_to(edge, (SUBLANES, BRANCH_W))
    h_ref[...] = carry[0:1, :]
    hs = jnp.concatenate(pieces, axis=0)
    if reverse:
        o_ref[0] = ((yf_ref[0] + hs) * _gelu_tanh(lg_ref[0].astype(F32))).astype(BF16)
    else:
        o_ref[0] = hs


def _rglru(lxg, conv_w, conv_b, w_gate, b_a, b_i, lam, reverse, yf=None):
    nb, s, _ = lxg.shape
    nc = s // ROW_TILE
    cur, prev, nxt = _tile_specs(BRANCH_W, nc, reverse)
    params = [conv_w, conv_b, w_gate, b_a, b_i, lam]
    in_specs = [cur, prev, nxt] + [_resident(p.shape) for p in params]
    args = [lxg, lxg, lxg] + params
    if reverse:
        in_specs += [cur, _tile_specs(BRANCH_W, nc, reverse, blk=1)[0]]
        args += [yf, lxg]
    return pl.pallas_call(
        functools.partial(_rglru_kernel, reverse=reverse, nc=nc),
        grid=(nb, nc),
        in_specs=in_specs,
        out_specs=cur,
        out_shape=jax.ShapeDtypeStruct((nb, s, BRANCH_W), BF16 if reverse else F32),
        scratch_shapes=[pltpu.VMEM((1, BRANCH_W), F32)],
        compiler_params=_cparams(("arbitrary", "arbitrary")),
    )(*args)


def _ssd_kernel(*refs, reverse, nc, direction):
    if reverse:
        (x_ref, xp_ref, xn_ref, dt_ref, dtt_ref, cw_ref, cb_ref, dtb_ref, dtbc_ref, alog_ref, alogc_ref,
         exp_ref, yf_ref, z_ref, skip_ref, g_ref, o_ref, st_ref) = refs
    else:
        (x_ref, xp_ref, xn_ref, dt_ref, dtt_ref, cw_ref, cb_ref, dtb_ref, dtbc_ref, alog_ref, alogc_ref,
         exp_ref, o_ref, st_ref) = refs
    j = pl.program_id(1)
    c = _scan_chunk(j, nc, reverse)

    @pl.when(j == 0)
    def _():
        st_ref[...] = jnp.zeros_like(st_ref)

    u = _silu(_conv4(x_ref, xp_ref, xn_ref, c, nc, cw_ref, cb_ref))
    t = u.shape[0]
    xs = u[:, :BRANCH_W]
    gw = SSD_HPG * SSD_HEADDIM

    dt_c = _softplus(dt_ref[0] + dtb_ref[...])
    a_c = dt_c * (-jnp.exp(alog_ref[...]))
    a_r = _softplus(dtt_ref[0] + dtbc_ref[...]) * (-jnp.exp(alogc_ref[...]))
    ri = lax.broadcasted_iota(jnp.int32, (t, t), 0)
    ci = lax.broadcasted_iota(jnp.int32, (t, t), 1)
    lower = (ci <= ri).astype(F32)
    upper = (ci >= ri).astype(F32)
    cs_c = jnp.dot(upper if reverse else lower, a_c, preferred_element_type=F32, precision=HIGHEST)
    cs_r = jnp.dot(a_r, lower if reverse else upper, preferred_element_type=F32, precision=HIGHEST)
    tot = cs_c[0:1, :] if reverse else cs_c[t - 1:t, :]
    keep = (ci >= ri) if reverse else (ci <= ri)

    expand = exp_ref[...]
    xdt = xs * jnp.dot(dt_c.astype(BF16), expand, preferred_element_type=F32)
    e_in = jnp.dot(jnp.exp(cs_c).astype(BF16), expand, preferred_element_type=F32)
    e_out = jnp.dot(jnp.exp(tot - cs_c).astype(BF16), expand, preferred_element_type=F32)
    e_tot = jnp.dot(jnp.broadcast_to(jnp.exp(tot), (SUBLANES, LANES)), expand.astype(F32),
                    preferred_element_type=F32, precision=HIGHEST)[0:1, :]
    xdt_b = xdt.astype(BF16)
    xdec_b = (xdt * e_out).astype(BF16)
    lane_head = lax.broadcasted_iota(jnp.int32, (t, gw), 1) // SSD_HEADDIM
    zero_b = jnp.zeros((t, gw), BF16)

    ys = []
    for g in range(SSD_GROUPS):
        bm = u[:, BRANCH_W + g * SSD_STATE:BRANCH_W + (g + 1) * SSD_STATE].astype(BF16)
        cm = u[:, BRANCH_W + (SSD_GROUPS + g) * SSD_STATE:BRANCH_W + (SSD_GROUPS + g + 1) * SSD_STATE].astype(BF16)
        cb = lax.dot_general(cm, bm, (((1,), (1,)), ((), ())), preferred_element_type=F32)
        st_g = st_ref[:, g * gw:(g + 1) * gw]
        y = jnp.dot(cm, st_g.astype(BF16), preferred_element_type=F32) * e_in[:, g * gw:(g + 1) * gw]
        xg = xdt_b[:, g * gw:(g + 1) * gw]
        for rr in range(SSD_HPG):
            col = direction * SSD_HEADS + g * SSD_HPG + rr
            decay = jnp.where(keep, jnp.exp(cs_c[:, col:col + 1] - cs_r[col:col + 1, :]), 0.0)
            gm = (cb * decay).astype(BF16)
            y = y + jnp.dot(gm, jnp.where(lane_head == rr, xg, zero_b), preferred_element_type=F32)
        ys.append(y)
        upd = lax.dot_general(bm, xdec_b[:, g * gw:(g + 1) * gw], (((0,), (0,)), ((), ())),
                              preferred_element_type=F32)
        st_ref[:, g * gw:(g + 1) * gw] = st_g * e_tot[:, g * gw:(g + 1) * gw] + upd
    y = jnp.concatenate(ys, axis=1)

    if reverse:
        y = yf_ref[0] + y + skip_ref[...] * xs
        tz = y * _silu(z_ref[0].astype(F32))
        outs = []
        for g in range(SSD_GROUPS):
            tg = tz[:, g * gw:(g + 1) * gw]
            outs.append(tg * lax.rsqrt(jnp.mean(tg * tg, axis=1, keepdims=True) + RMS_EPS))
        o_ref[0] = (jnp.concatenate(outs, axis=1) * g_ref[...]).astype(BF16)
    else:
        o_ref[0] = y


def _ssd(xbc, dt, dtt, p, direction, yf=None, z=None):
    reverse = direction == 1
    nb, s, _ = xbc.shape
    nc = s // ROW_TILE
    cur, prev, nxt = _tile_specs(2 * BRANCH_W, nc, reverse)
    ch = lambda j: _scan_chunk(j, nc, reverse)
    params = [p["conv_w"], p["conv_b"], p["dtb_row"], p["dtb_col"], p["alog_row"], p["alog_col"], p["expand"]]
    in_specs = [cur, prev, nxt,
                pl.BlockSpec((1, ROW_TILE, LANES), lambda b, j: (b, ch(j), 0)),
                pl.BlockSpec((1, 2 * SSD_HEADS, ROW_TILE), lambda b, j: (b, 0, ch(j)))]
    in_specs += [_resident(a.shape) for a in params]
    args = [xbc, xbc, xbc, dt, dtt] + params
    half = pl.BlockSpec((1, ROW_TILE, BRANCH_W), lambda b, j: (b, ch(j), 0))
    if reverse:
        in_specs += [half, half, _resident(p["skip"].shape), _resident(p["norm_g"].shape)]
        args += [yf, z, p["skip"], p["norm_g"]]
    return pl.pallas_call(
        functools.partial(_ssd_kernel, reverse=reverse, nc=nc, direction=direction),
        grid=(nb, nc),
        in_specs=in_specs,
        out_specs=half,
        out_shape=jax.ShapeDtypeStruct((nb, s, BRANCH_W), BF16 if reverse else F32),
        scratch_shapes=[pltpu.VMEM((SSD_STATE, BRANCH_W), F32)],
        compiler_params=_cparams(("arbitrary", "arbitrary")),
    )(*args)


def _merge_kernel(ya_ref, yb_ref, yc_ref, gt_ref, x_ref, m_ref, wb_ref, wo_ref, g_ref, b_ref,
                  x1_ref, h2_ref, *, alpha):
    acc = None
    for n, y_ref in enumerate((ya_ref, yb_ref, yc_ref)):
        proj = jnp.dot(y_ref[0], wb_ref[n], preferred_element_type=F32)
        gate = jax.nn.sigmoid(gt_ref[0, :, n * D_MODEL:(n + 1) * D_MODEL].astype(F32))
        acc = gate * proj if acc is None else acc + gate * proj
    mix = jnp.dot(acc.astype(BF16), wo_ref[...], preferred_element_type=F32)
    x1 = _layer_norm(alpha * x_ref[0] + m_ref[0, 2:3, :] * mix, g_ref[...], b_ref[...])
    x1_ref[0] = x1
    _store_token_rows(h2_ref.at[0], x1 * (1.0 + m_ref[0, 4:5, :]) + m_ref[0, 3:4, :])


def _zero_tile_kernel(h_ref, o_ref):
    del h_ref
    o_ref[...] = jnp.zeros_like(o_ref)


def _merge(ya, yb, yc, gates, xc, mods, w_branch, w_out, ln_g, ln_b, alpha):
    nb, s, _ = xc.shape
    nt = s // ROW_TILE
    row = lambda width: pl.BlockSpec((1, ROW_TILE, width), lambda b, i: (b, i, 0))
    h2_shape = jax.ShapeDtypeStruct((nb, (s + MOE_PAD) * NCHUNK, LANES), F32)
    x1, h2 = pl.pallas_call(
        functools.partial(_merge_kernel, alpha=alpha),
        grid=(nb, nt),
        in_specs=[row(BRANCH_W), row(BRANCH_W), row(BRANCH_W), row(3 * D_MODEL), row(D_MODEL), _mod_spec(nb),
                  _resident(w_branch.shape), _resident(w_out.shape), _resident(ln_g.shape), _resident(ln_b.shape)],
        out_specs=[row(D_MODEL), pl.BlockSpec((1, ROW_TILE * NCHUNK, LANES), lambda b, i: (b, i, 0))],
        out_shape=[jax.ShapeDtypeStruct((nb, s, D_MODEL), F32), h2_shape],
        compiler_params=_cparams(("arbitrary", "arbitrary")),
    )(ya, yb, yc, gates, xc, mods, w_branch, w_out, ln_g, ln_b)
    h2 = pl.pallas_call(
        _zero_tile_kernel,
        grid=(nb,),
        in_specs=[pl.BlockSpec(memory_space=pl.ANY)],
        out_specs=pl.BlockSpec((1, NCHUNK, LANES), lambda b: (b, s, 0)),
        out_shape=h2_shape,
        input_output_aliases={0: 0},
        compiler_params=_cparams(("arbitrary",)),
    )(h2)
    return x1, h2


def _first_index(hit, ridx, n):
    return jnp.min(jnp.where(hit, ridx, n), axis=0, keepdims=True)


def _route_kernel(h_ref, wr_ref, bias_ref, cnt_ref, rank_ref, wgt_ref, base_ref, *, latent_only):
    nrow, _, tile_tokens = wgt_ref.shape
    tm = nrow * tile_tokens

    @pl.when(pl.program_id(1) == 0)
    def _():
        base_ref[...] = jnp.zeros_like(base_ref)

    logits = jnp.concatenate(
        [lax.dot_general(wr_ref[...], _load_token_rows(h_ref.at[bb], tile_tokens), (((1,), (1,)), ((), ())),
                         preferred_element_type=F32, precision=HIGHEST) for bb in range(nrow)], axis=1)
    scores = jax.nn.sigmoid(logits)
    sel = scores + bias_ref[...]
    neg = -jnp.inf

    r8 = lax.broadcasted_iota(jnp.int32, (GROUP_SIZE, tm), 0)
    grp = jnp.zeros((N_GROUPS, tm), F32)
    for g in range(N_GROUPS):
        blk = sel[g * GROUP_SIZE:(g + 1) * GROUP_SIZE, :]
        m1 = jnp.max(blk, axis=0, keepdims=True)
        first = _first_index(blk == m1, r8, GROUP_SIZE)
        m2 = jnp.max(jnp.where(r8 == first, neg, blk), axis=0, keepdims=True)
        grp = jnp.where(r8 == g, m1 + m2, grp)
    gsel = jnp.zeros((N_GROUPS, tm), jnp.int32)
    for _ in range(TOPK_GROUPS):
        m = jnp.max(grp, axis=0, keepdims=True)
        hit = r8 == _first_index(grp == m, r8, N_GROUPS)
        gsel = jnp.where(hit, 1, gsel)
        grp = jnp.where(hit, neg, grp)
    cand = jnp.concatenate(
        [jnp.where(gsel[g:g + 1, :] > 0, sel[g * GROUP_SIZE:(g + 1) * GROUP_SIZE, :], neg) for g in range(N_GROUPS)],
        axis=0)

    re = lax.broadcasted_iota(jnp.int32, (N_EXPERTS, tm), 0)
    chosen = jnp.zeros((N_EXPERTS, tm), jnp.int32)
    for _ in range(TOP_K):
        m = jnp.max(cand, axis=0, keepdims=True)
        hit = re == _first_index(cand == m, re, N_EXPERTS)
        chosen = jnp.where(hit, 1, chosen)
        cand = jnp.where(hit, neg, cand)
    picked = jnp.where(chosen > 0, scores, 0.0)
    wgt = picked / jnp.sum(picked, axis=0, keepdims=True) * ROUTED_SCALE
    if latent_only:
        chosen = jnp.where(pl.program_id(1) == 0, 0, chosen)

    li = lax.broadcasted_iota(jnp.int32, (LANES, LANES), 0)
    lj = lax.broadcasted_iota(jnp.int32, (LANES, LANES), 1)
    strict = (li < lj).astype(BF16)
    chosen_b = chosen.astype(F32).astype(BF16)
    for bb in range(nrow):
        base = base_ref[bb]
        pos = []
        for kb in range(bb * tile_tokens // LANES, (bb + 1) * tile_tokens // LANES):
            cblk = chosen_b[:, kb * LANES:(kb + 1) * LANES]
            pos.append(jnp.dot(cblk, strict, preferred_element_type=F32) + base)
            base = base + jnp.sum(cblk.astype(F32), axis=1, keepdims=True)
        base_ref[bb] = base
        cols = slice(bb * tile_tokens, (bb + 1) * tile_tokens)
        cnt_ref[bb] = base.astype(jnp.int32)
        rank_ref[bb] = jnp.where(chosen[:, cols] > 0, jnp.concatenate(pos, axis=1).astype(jnp.int32), -1)
        wgt_ref[bb] = wgt[:, cols]


def _compact_kernel(cnt_ref, rank_ref, wgt_ref, idx_ref, wl_ref):
    rank = rank_ref[0]
    ne, tm = rank.shape
    lane = lax.broadcasted_iota(jnp.int32, (ne, tm), 1)
    dist = jnp.where(rank >= 0, lane - rank, 0)
    val = lane
    wv = wgt_ref[0]
    step = 1
    while step < tm:
        move = dist & step
        sh = tm - step
        take = pltpu.roll(move, sh, 1) != 0
        val = jnp.where(take, pltpu.roll(val, sh, 1), val)
        wv = jnp.where(take, pltpu.roll(wv, sh, 1), wv)
        dist = jnp.where(take, pltpu.roll(dist, sh, 1), jnp.where(move != 0, 0, dist))
        step *= 2
    valid = lane < cnt_ref[0]
    spare = jnp.full((ne, MOE_SPARE), tm * NCHUNK, jnp.int32)
    idx_ref[0] = jnp.concatenate([jnp.where(valid, val * NCHUNK, tm * NCHUNK), spare], axis=1)
    wl_ref[0] = jnp.where(valid, wv, 0.0)


def _router(h2, s, w_router_t, bias_col, latent_only):
    nb = h2.shape[0]
    nrow = 2 if nb % 2 == 0 else 1
    dense = pl.BlockSpec((nrow, N_EXPERTS, ROW_TILE), lambda b, i: (b, 0, i))
    counts, rank, wgt = pl.pallas_call(
        functools.partial(_route_kernel, latent_only=latent_only),
        grid=(nb // nrow, s // ROW_TILE),
        in_specs=[pl.BlockSpec((nrow, ROW_TILE * NCHUNK, LANES), lambda b, i: (b, i, 0)),
                  _resident(w_router_t.shape), _resident(bias_col.shape)],
        out_specs=[pl.BlockSpec((nrow, N_EXPERTS, 1), lambda b, i: (b, 0, 0)), dense, dense],
        out_shape=[jax.ShapeDtypeStruct((nb, N_EXPERTS, 1), jnp.int32),
                   jax.ShapeDtypeStruct((nb, N_EXPERTS, s), jnp.int32),
                   jax.ShapeDtypeStruct((nb, N_EXPERTS, s), F32)],
        scratch_shapes=[pltpu.VMEM((nrow, N_EXPERTS, 1), F32)],
        compiler_params=_cparams(("arbitrary", "arbitrary")),
    )(h2, w_router_t, bias_col)
    rows = lambda width: pl.BlockSpec((1, SUBLANES, width), lambda b, g: (b, g, 0))
    idx, wl = pl.pallas_call(
        _compact_kernel,
        grid=(nb, N_EXPERTS // SUBLANES),
        in_specs=[rows(1), rows(s), rows(s)],
        out_specs=[rows(s + MOE_SPARE), rows(s)],
        out_shape=[jax.ShapeDtypeStruct((nb, N_EXPERTS, s + MOE_SPARE), jnp.int32),
                   jax.ShapeDtypeStruct((nb, N_EXPERTS, s), F32)],
        compiler_params=_cparams(("arbitrary", "arbitrary")),
    )(counts, rank, wgt)
    return counts, idx, wl


SCATTER_GROUP = 16


def _experts_kernel(cnt_ref, idx_ref, wl_ref, h_ref, wu_ref, wd_ref, o_ref,
                    xa_ref, xb_ref, xc_ref, ya_ref, yb_ref, yc_ref):
    t = pl.program_id(0)
    e = pl.program_id(1)
    h_rows = h_ref.at[0]
    acc_ref = o_ref.at[0]

    @pl.when(e == 0)
    def _():
        o_ref[...] = jnp.zeros_like(o_ref)

    count = cnt_ref[t * N_EXPERTS + e]
    nblk = (count + MOE_BLOCK - 1) // MOE_BLOCK
    ri = lax.broadcasted_iota(jnp.int32, (MOE_BLOCK, MOE_BLOCK), 0)
    ci = lax.broadcasted_iota(jnp.int32, (MOE_BLOCK, MOE_BLOCK), 1)

    def gather(blk, x_ref):
        ids = idx_ref.at[0, 0, pl.ds(blk * MOE_BLOCK, MOE_BLOCK)]
        for r in range(MOE_BLOCK):
            x_ref[pl.ds(r * NCHUNK, NCHUNK), :] = h_rows[pl.ds(pl.multiple_of(ids[r], NCHUNK), NCHUNK), :]

    def swiglu(x, blks, y_refs):
        gu = jnp.dot(x.astype(BF16), wu_ref[0], preferred_element_type=F32)
        act = _silu(gu[:, :EXPERT_F]) * gu[:, EXPERT_F:]
        y = jnp.dot(act.astype(BF16), wd_ref[0], preferred_element_type=F32)
        for k, (blk, y_ref) in enumerate(zip(blks, y_refs)):
            w_row = wl_ref[0, pl.ds(blk, 1), :]
            w_col = jnp.sum(jnp.where(ri == ci, w_row, 0.0), axis=1, keepdims=True)
            _store_token_rows(y_ref, y[k * MOE_BLOCK:(k + 1) * MOE_BLOCK, :] * w_col)

    def scatter(blk, y_ref):
        ids = idx_ref.at[0, 0, pl.ds(blk * MOE_BLOCK, MOE_BLOCK)]
        for g0 in range(0, MOE_BLOCK, SCATTER_GROUP):
            rows = range(g0, g0 + SCATTER_GROUP)
            dst = [pl.multiple_of(ids[r], NCHUNK) for r in rows]
            new = [acc_ref[pl.ds(d, NCHUNK), :] + y_ref[pl.ds(r * NCHUNK, NCHUNK), :] for d, r in zip(dst, rows)]
            for d, v in zip(dst, new):
                acc_ref[pl.ds(d, NCHUNK), :] = v

    odd = nblk % 2 == 1
    nloop = jnp.where(jnp.logical_and(odd, nblk >= 3), nblk // 2 - 1, nblk // 2)

    @pl.when(nblk > 0)
    def _():
        gather(0, xa_ref)

    @pl.when(nblk >= 2)
    def _():
        gather(1, xb_ref)
        ya_ref[...] = jnp.zeros_like(ya_ref)
        yb_ref[...] = jnp.zeros_like(yb_ref)

    def load_rows(*x_refs):
        return jnp.concatenate([_load_token_rows(r, MOE_BLOCK) for r in x_refs], axis=0)

    def pair(j, carry):
        b0 = 2 * j
        x = load_rows(xa_ref, xb_ref)
        scatter(jnp.maximum(b0 - 2, 0), ya_ref)
        scatter(jnp.maximum(b0 - 1, 0), yb_ref)
        gather(b0 + 2, xa_ref)
        gather(b0 + 3, xb_ref)
        swiglu(x, (b0, b0 + 1), (ya_ref, yb_ref))
        return carry

    lax.fori_loop(0, nloop, pair, 0)
    b0 = 2 * nloop

    @pl.when(jnp.logical_and(odd, nblk >= 3))
    def _():
        gather(b0 + 2, xc_ref)
        x = load_rows(xa_ref, xb_ref, xc_ref)
        scatter(jnp.maximum(b0 - 2, 0), ya_ref)
        scatter(jnp.maximum(b0 - 1, 0), yb_ref)
        swiglu(x, (b0, b0 + 1, b0 + 2), (ya_ref, yb_ref, yc_ref))
        scatter(b0, ya_ref)
        scatter(b0 + 1, yb_ref)
        scatter(b0 + 2, yc_ref)

    @pl.when(jnp.logical_and(jnp.logical_not(odd), nblk > 0))
    def _():
        scatter(nblk - 2, ya_ref)
        scatter(nblk - 1, yb_ref)

    @pl.when(nblk == 1)
    def _():
        swiglu(load_rows(xa_ref), (0,), (ya_ref,))
        scatter(0, ya_ref)


def _experts(h2, counts, idx, wl, w_up, w_down):
    nb, rows, _ = h2.shape
    nlist = idx.shape[-1]
    tile = pl.BlockSpec((1, rows, LANES), lambda t, e, c: (t, 0, 0), pipeline_mode=pl.Buffered(1))
    buf = pltpu.VMEM((MOE_BLOCK * NCHUNK, LANES), F32)
    grid_spec = pltpu.PrefetchScalarGridSpec(
        num_scalar_prefetch=1,
        grid=(nb, N_EXPERTS),
        in_specs=[pl.BlockSpec((1, 1, nlist), lambda t, e, c: (t * N_EXPERTS + e, 0, 0), memory_space=pltpu.SMEM),
                  pl.BlockSpec((1,) + wl.shape[1:], lambda t, e, c: (t * N_EXPERTS + e, 0, 0)),
                  tile,
                  pl.BlockSpec((1, D_MODEL, 2 * EXPERT_F), lambda t, e, c: (e, 0, 0)),
                  pl.BlockSpec((1, EXPERT_F, D_MODEL), lambda t, e, c: (e, 0, 0))],
        out_specs=tile,
        scratch_shapes=[buf] * 6,
    )
    return pl.pallas_call(
        _experts_kernel,
        grid_spec=grid_spec,
        out_shape=jax.ShapeDtypeStruct(h2.shape, F32),
        compiler_params=_cparams(("arbitrary", "arbitrary")),
    )(counts, idx, wl, h2, w_up, w_down)


def _ffn_out_kernel(x1_ref, fr_ref, m_ref, wu_ref, wd_ref, g_ref, b_ref, o_ref, *, alpha):
    x1 = x1_ref[0]
    h2 = x1 * (1.0 + m_ref[0, 4:5, :]) + m_ref[0, 3:4, :]
    gu = jnp.dot(h2.astype(BF16), wu_ref[...], preferred_element_type=F32)
    act = _silu(gu[:, :EXPERT_F]) * gu[:, EXPERT_F:]
    f = (jnp.dot(act.astype(BF16), wd_ref[...], preferred_element_type=F32)
         + _load_token_rows(fr_ref.at[0], ROW_TILE))
    o_ref[0] = _layer_norm(alpha * x1 + m_ref[0, 5:6, :] * f, g_ref[...], b_ref[...])


def _ffn_out(x1, fr, mods, ws_up, ws_down, ln_g, ln_b, alpha, latent_only):
    nb, s, _ = x1.shape
    nt = s // ROW_TILE
    skip = CTX_LEN // ROW_TILE if latent_only else 0
    row = pl.BlockSpec((1, ROW_TILE, D_MODEL), lambda b, i: (b, i + skip, 0))
    mod = pl.BlockSpec((1, 6, D_MODEL), lambda b, i: (jnp.where(i + skip == 0, nb, b), 0, 0))
    chunked = pl.BlockSpec((1, ROW_TILE * NCHUNK, LANES), lambda b, i: (b, i + skip, 0))
    return pl.pallas_call(
        functools.partial(_ffn_out_kernel, alpha=alpha),
        grid=(nb, nt - skip),
        in_specs=[row, chunked, mod, _resident(ws_up.shape), _resident(ws_down.shape),
                  _resident(ln_g.shape), _resident(ln_b.shape)],
        out_specs=pl.BlockSpec((1, ROW_TILE, D_MODEL), lambda b, i: (b, i, 0)),
        out_shape=jax.ShapeDtypeStruct((nb, s - skip * ROW_TILE, D_MODEL), F32),
        compiler_params=_cparams(("arbitrary", "arbitrary")),
    )(x1, fr, mods, ws_up, ws_down, ln_g, ln_b)


def _rope_tables(n_lat):
    t = jnp.arange(n_lat)
    rowp = (t // GRID_W).astype(F32)
    colp = (t % GRID_W).astype(F32)
    n_freq = ATT_DH // 4
    inv = ROPE_BASE ** (-jnp.arange(n_freq, dtype=F32) / n_freq)
    ang = jnp.concatenate([rowp[:, None] * inv, colp[:, None] * inv], axis=-1)
    lane = jnp.arange(LANES)
    cos = jnp.cos(ang)[:, lane % (ATT_DH // 2)]
    sign = jnp.where((lane % ATT_DH) < ATT_DH // 2, -1.0, 1.0).astype(F32)
    sin = jnp.sin(ang)[:, lane % (ATT_DH // 2)] * sign
    cos = jnp.concatenate([jnp.ones((CTX_LEN, LANES), F32), cos], axis=0)
    sin = jnp.concatenate([jnp.zeros((CTX_LEN, LANES), F32), sin], axis=0)
    return cos, sin


def _block_diag(w):
    n, k, _ = w.shape
    eye = jnp.eye(n, dtype=w.dtype)
    return (eye[:, None, :, None] * w[:, :, None, :]).reshape(n * k, n * k)


def _inproj_weights(w):
    o_v = 2 * BRANCH_W
    o_lx = o_v + ATT_HEADS * ATT_DV
    o_z = o_lx + 2 * BRANCH_W
    o_xbc = o_z + BRANCH_W
    o_dt = o_xbc + 2 * BRANCH_W
    o_g = o_dt + 2 * SSD_HEADS
    wv = w[:, o_v:o_lx].T.reshape(ATT_HEADS, ATT_DV, D_MODEL)
    wv = jnp.pad(wv, ((0, 0), (0, VT_ROWS - ATT_DV), (0, 0))).reshape(ATT_HEADS * VT_ROWS, D_MODEL)
    ones = jnp.zeros((ATT_HEADS, VT_ROWS, 1), F32).at[:, ATT_DV, 0].set(1.0).reshape(ATT_HEADS * VT_ROWS, 1)
    wdt = w[:, o_dt:o_g]
    return {
        "qk": w[:, :o_v].astype(BF16),
        "vt": wv.astype(BF16),
        "ones": ones,
        "lxg": w[:, o_lx:o_z].astype(BF16),
        "z": w[:, o_z:o_xbc].astype(BF16),
        "xbc": w[:, o_xbc:o_dt].astype(BF16),
        "dt": jnp.pad(wdt, ((0, 0), (0, LANES - 2 * SSD_HEADS))).astype(BF16),
        "dtt": wdt.T.astype(BF16),
        "g": w[:, o_g:].astype(BF16),
    }


def _pad_row(v):
    return jnp.pad(v.reshape(1, -1), ((0, 0), (0, LANES - v.size)))


def kernel(x, c, ctx, c_ctx, w_mod, b_mod, w_in, lam_q, lam_k, attn_norm_g, lru_conv_w, lru_conv_b, lru_wa, lru_ba, lru_wi, lru_bi, lru_lambda, ssd_conv_w, ssd_conv_b, ssd_dt_bias, ssd_a_log, ssd_d, ssd_norm_g, w_branch, w_out, ln1_g, ln1_b, w_router, router_bias, w_up, w_down, ws_up, ws_down, ln2_g, ln2_b):
    nb, n_lat, _ = x.shape
    depth = w_mod.shape[0]
    assert ctx.shape[1] == CTX_LEN and n_lat % ROW_TILE == 0 and nb + 1 <= 16
    s = CTX_LEN + n_lat
    alpha = (2 * depth) ** 0.25

    xc = jnp.concatenate([ctx, x], axis=1)
    cc = jnp.zeros((16, D_MODEL), F32).at[:nb].set(c).at[nb].set(c_ctx)
    mods_all = _modulation(cc, w_mod, b_mod).reshape(depth, 16, 6, D_MODEL)
    cos_t, sin_t = _rope_tables(n_lat)
    head_of_channel = jnp.arange(BRANCH_W) // SSD_HEADDIM

    for l in range(depth):
        last = l == depth - 1
        lam_init = 0.8 - 0.6 * math.exp(-0.3 * l)
        mods = mods_all[l]
        qk, vt, lxg, z, xbc, dt, dtt, gates = _inproj(xc, mods, cos_t, sin_t, _inproj_weights(w_in[l]))

        ya = _attention(qk, vt, lam_q[l], lam_k[l], attn_norm_g[l], lam_init)

        yb = None
        for d in range(2):
            w_gate = jnp.concatenate([_block_diag(lru_wa[l, d]), _block_diag(lru_wi[l, d])], axis=1).astype(BF16)
            yb = _rglru(lxg, lru_conv_w[l], lru_conv_b[l].reshape(1, -1), w_gate,
                        lru_ba[l, d].reshape(1, -1), lru_bi[l, d].reshape(1, -1), lru_lambda[l, d].reshape(1, -1),
                        reverse=(d == 1), yf=yb)

        ssd_p = {
            "conv_w": ssd_conv_w[l], "conv_b": ssd_conv_b[l].reshape(1, -1),
            "dtb_row": _pad_row(ssd_dt_bias[l]), "dtb_col": ssd_dt_bias[l].reshape(-1, 1),
            "alog_row": _pad_row(ssd_a_log[l]), "alog_col": ssd_a_log[l].reshape(-1, 1),
            "skip": jnp.repeat(ssd_d[l], SSD_HEADDIM).reshape(1, -1), "norm_g": ssd_norm_g[l].reshape(1, -1),
        }
        yc = None
        for d in range(2):
            ssd_p["expand"] = (jnp.arange(LANES)[:, None] == d * SSD_HEADS + head_of_channel[None, :]).astype(BF16)
            yc = _ssd(xbc, dt, dtt, ssd_p, d, yf=yc, z=z)

        x1, h2 = _merge(ya, yb, yc, gates, xc, mods, w_branch[l].astype(BF16), w_out[l].astype(BF16),
                        ln1_g[l].reshape(1, -1), ln1_b[l].reshape(1, -1), alpha)

        counts, idx, wl = _router(h2, s, w_router[l].T, router_bias[l].reshape(-1, 1), latent_only=last)
        fr = _experts(h2, counts.reshape(nb * N_EXPERTS), idx.reshape(nb * N_EXPERTS, 1, s + MOE_SPARE),
                      wl.reshape(nb * N_EXPERTS, s // MOE_BLOCK, MOE_BLOCK),
                      w_up[l].astype(BF16), w_down[l].astype(BF16))
        xc = _ffn_out(x1, fr, mods, ws_up[l].astype(BF16), ws_down[l].astype(BF16),
                      ln2_g[l].reshape(1, -1), ln2_b[l].reshape(1, -1), alpha, latent_only=last)
    return xc
```

```python
import functools
import math

import jax
import jax.numpy as jnp
from jax import lax
from jax.experimental import pallas as pl
from jax.experimental.pallas import tpu as pltpu

F32 = jnp.float32
BF16 = jnp.bfloat16
HIGHEST = lax.Precision.HIGHEST

D_MODEL = 1024
GRID_W = 64
CTX_LEN = 256
BRANCH_W = 512
ATT_HEADS = 4
ATT_DH = 64
ATT_DV = 128
ROPE_BASE = 10000.0
LRU_C = 8.0
SSD_HEADS = 8
SSD_HEADDIM = 64
SSD_HPG = 4
SSD_GROUPS = 2
SSD_STATE = 128
N_EXPERTS = 64
N_GROUPS = 8
GROUP_SIZE = N_EXPERTS // N_GROUPS
TOP_K = 8
TOPK_GROUPS = 4
EXPERT_F = 256
ROUTED_SCALE = 2.5
LN_EPS = 1e-5
RMS_EPS = 1e-6

ROW_TILE = 256
HALO = 16
VT_ROWS = 144
KEY_CHUNK = 256
SCORE_CHUNKS = 2
MOE_BLOCK = 128
MOE_PAD = ROW_TILE
MOE_SPARE = 2 * MOE_BLOCK
SUBLANES = 8
LANES = 128
VMEM_LIMIT = 56 * 1024 * 1024
LOG2E = 1.4426950408889634


def _cparams(sem):
    return pltpu.CompilerParams(dimension_semantics=sem, vmem_limit_bytes=VMEM_LIMIT)


def _resident(shape):
    nd = len(shape)
    return pl.BlockSpec(shape, lambda *_: (0,) * nd, pipeline_mode=pl.Buffered(1))


def _silu(x):
    return x * jax.nn.sigmoid(x)


def _softplus(x):
    return jnp.maximum(x, 0.0) + jnp.log1p(jnp.exp(-jnp.abs(x)))


def _gelu_tanh(x):
    return 0.5 * x * (1.0 + jnp.tanh(math.sqrt(2.0 / math.pi) * (x + 0.044715 * (x * x * x))))


NCHUNK = D_MODEL // LANES


def _load_token_rows(ref, n, first=0):
    return jnp.concatenate([ref[pl.ds(first * NCHUNK + s, n, stride=NCHUNK), :] for s in range(NCHUNK)], axis=1)


def _store_token_rows(ref, x):
    n = x.shape[0]
    for s in range(NCHUNK):
        ref[pl.ds(s, n, stride=NCHUNK), :] = x[:, s * LANES:(s + 1) * LANES]


def _layer_norm(x, g, b):
    mu = jnp.mean(x, axis=-1, keepdims=True)
    xc = x - mu
    var = jnp.mean(xc * xc, axis=-1, keepdims=True)
    return xc * lax.rsqrt(var + LN_EPS) * g + b


def _mod_kernel(c_ref, w_ref, b_ref, o_ref):
    s = _silu(c_ref[...])
    o_ref[0] = jnp.dot(s, w_ref[0], preferred_element_type=F32, precision=HIGHEST) + b_ref[0]


def _modulation(cc, w_mod, b_mod):
    depth = w_mod.shape[0]
    nblk = 6
    return pl.pallas_call(
        _mod_kernel,
        grid=(depth, nblk),
        in_specs=[
            pl.BlockSpec((16, D_MODEL), lambda l, j: (0, 0)),
            pl.BlockSpec((1, D_MODEL, D_MODEL), lambda l, j: (l, 0, j)),
            pl.BlockSpec((1, 1, D_MODEL), lambda l, j: (l, 0, j)),
        ],
        out_specs=pl.BlockSpec((1, 16, D_MODEL), lambda l, j: (l, 0, j)),
        out_shape=jax.ShapeDtypeStruct((depth, 16, nblk * D_MODEL), F32),
        compiler_params=_cparams(("arbitrary", "arbitrary")),
    )(cc, w_mod, b_mod.reshape(depth, 1, nblk * D_MODEL))


def _mod_spec(nb):
    return pl.BlockSpec((1, 6, D_MODEL), lambda b, i: (jnp.where(i == 0, nb, b), 0, 0))


def _inproj_kernel(x_ref, m_ref, cos_ref, sin_ref, wqk_ref, wvt_ref, ones_ref, wlxg_ref, wz_ref,
                   wxbc_ref, wdt_ref, wdtt_ref, wg_ref,
                   qk_ref, vt_ref, lxg_ref, z_ref, xbc_ref, dt_ref, dtt_ref, g_ref):
    x = x_ref[0]
    h = (x * (1.0 + m_ref[0, 1:2, :]) + m_ref[0, 0:1, :]).astype(BF16)
    nt = (((1,), (1,)), ((), ()))

    qk = jnp.dot(h, wqk_ref[...], preferred_element_type=F32)
    cos = cos_ref[...]
    sin = sin_ref[...]
    lane = lax.broadcasted_iota(jnp.int32, cos.shape, 1)
    first_half = (lane % ATT_DH) < (ATT_DH // 2)
    for j in range(2 * ATT_HEADS):
        blk = qk[:, j * LANES:(j + 1) * LANES]
        partner = jnp.where(first_half, pltpu.roll(blk, LANES - ATT_DH // 2, 1),
                            pltpu.roll(blk, ATT_DH // 2, 1))
        r = blk * cos + partner * sin
        if j < ATT_HEADS:
            r = r * (ATT_DH ** -0.5 * LOG2E)
        qk_ref[0, :, j * LANES:(j + 1) * LANES] = r.astype(BF16)

    vt = lax.dot_general(wvt_ref[...], h, nt, preferred_element_type=F32) + ones_ref[...]
    vt_ref[0] = vt.astype(BF16)
    lxg_ref[0] = jnp.dot(h, wlxg_ref[...], preferred_element_type=F32).astype(BF16)
    z_ref[0] = jnp.dot(h, wz_ref[...], preferred_element_type=F32).astype(BF16)
    xbc_ref[0] = jnp.dot(h, wxbc_ref[...], preferred_element_type=F32).astype(BF16)
    dt_ref[0] = jnp.dot(h, wdt_ref[...], preferred_element_type=F32)
    dtt_ref[0] = lax.dot_general(wdtt_ref[...], h, nt, preferred_element_type=F32)
    g_ref[0] = jnp.dot(h, wg_ref[...], preferred_element_type=F32).astype(BF16)


def _inproj(xc, mods, cos_t, sin_t, w):
    nb, s, _ = xc.shape
    nt = s // ROW_TILE
    row = lambda width: pl.BlockSpec((1, ROW_TILE, width), lambda b, i: (b, i, 0))
    col = lambda rows: pl.BlockSpec((1, rows, ROW_TILE), lambda b, i: (b, 0, i))
    vt_rows = ATT_HEADS * VT_ROWS
    outs = [
        (jax.ShapeDtypeStruct((nb, s, 2 * BRANCH_W), BF16), row(2 * BRANCH_W)),
        (jax.ShapeDtypeStruct((nb, vt_rows, s), BF16), col(vt_rows)),
        (jax.ShapeDtypeStruct((nb, s, 2 * BRANCH_W), BF16), row(2 * BRANCH_W)),
        (jax.ShapeDtypeStruct((nb, s, BRANCH_W), BF16), row(BRANCH_W)),
        (jax.ShapeDtypeStruct((nb, s, 2 * BRANCH_W), BF16), row(2 * BRANCH_W)),
        (jax.ShapeDtypeStruct((nb, s, LANES), F32), row(LANES)),
        (jax.ShapeDtypeStruct((nb, 2 * SSD_HEADS, s), F32), col(2 * SSD_HEADS)),
        (jax.ShapeDtypeStruct((nb, s, 3 * D_MODEL), BF16), row(3 * D_MODEL)),
    ]
    weights = [w["qk"], w["vt"], w["ones"], w["lxg"], w["z"], w["xbc"], w["dt"], w["dtt"], w["g"]]
    return pl.pallas_call(
        _inproj_kernel,
        grid=(nb, nt),
        in_specs=[row(D_MODEL), _mod_spec(nb),
                  pl.BlockSpec((ROW_TILE, LANES), lambda b, i: (i, 0)),
                  pl.BlockSpec((ROW_TILE, LANES), lambda b, i: (i, 0))]
                 + [_resident(a.shape) for a in weights],
        out_specs=[o[1] for o in outs],
        out_shape=[o[0] for o in outs],
        compiler_params=_cparams(("arbitrary", "arbitrary")),
    )(xc, mods, cos_t, sin_t, *weights)


def _attn_kernel(lq_ref, lk_ref, g_ref, q_ref, qn_ref, k_ref, vt_ref, o_ref, sa_ref, sb_ref, m_ref, *, lam_init):
    i = pl.program_id(2)
    tq = q_ref.shape[1]
    nck = k_ref.shape[1] // KEY_CHUNK
    prod = lq_ref[...] * lk_ref[...]
    d0 = jnp.sum(prod[0:1, :], axis=1, keepdims=True)
    d1 = jnp.sum(prod[1:2, :], axis=1, keepdims=True)
    lam = jnp.exp(d0) - jnp.exp(d1) + lam_init
    gcol = g_ref[0] * (1.0 - lam_init)

    def stack_maps(q):
        lane = lax.broadcasted_iota(jnp.int32, q.shape, 1)
        zero = jnp.zeros_like(q)
        return jnp.concatenate([jnp.where(lane < ATT_DH, q, zero), jnp.where(lane >= ATT_DH, q, zero)], axis=0)

    def scores(c, q2, s_ref, m, n=1):
        rows = slice(c * KEY_CHUNK, (c + n) * KEY_CHUNK)
        st = lax.dot_general(k_ref[0, rows, :], q2, (((1,), (1,)), ((), ())),
                             preferred_element_type=F32)
        s_ref[rows, :] = st
        mc = jnp.max(st, axis=0, keepdims=True)
        return mc if m is None else jnp.maximum(m, mc)

    def weigh(c, s_ref, m, acc):
        rows = slice(c * KEY_CHUNK, (c + 1) * KEY_CHUNK)
        e = jnp.exp2((s_ref[rows, :] - m).astype(BF16))
        pv = jnp.dot(vt_ref[0, :, rows], e, preferred_element_type=F32)
        return pv if acc is None else acc + pv

    def finish(acc):
        r = 1.0 / acc[ATT_DV:ATT_DV + 1, :]
        o = acc[:ATT_DV, :tq] * r[:, :tq] - lam * (acc[:ATT_DV, tq:] * r[:, tq:])
        ms = jnp.mean(o * o, axis=0, keepdims=True)
        o_ref[0] = (o * lax.rsqrt(ms + RMS_EPS) * gcol).T.astype(BF16)

    @pl.when(i == 0)
    def _():
        q2 = stack_maps(q_ref[0])
        finish(weigh(0, sa_ref, scores(0, q2, sa_ref, None), None))
        q2n = stack_maps(qn_ref[0])
        mn = None
        for c in range(0, nck, SCORE_CHUNKS):
            mn = scores(c, q2n, sb_ref, mn, min(SCORE_CHUNKS, nck - c))
        m_ref[1:2, :] = mn

    def step(cur_ref, nxt_ref, cur_slot, nxt_slot):
        q2n = stack_maps(qn_ref[0])
        m_cur = m_ref[cur_slot:cur_slot + 1, :]
        mn = acc = None
        for c in range(nck):
            if c % SCORE_CHUNKS == 0:
                mn = scores(c, q2n, nxt_ref, mn, min(SCORE_CHUNKS, nck - c))
            acc = weigh(c, cur_ref, m_cur, acc)
        finish(acc)
        m_ref[nxt_slot:nxt_slot + 1, :] = mn

    @pl.when(i % 2 == 1)
    def _():
        step(sb_ref, sa_ref, 1, 0)

    @pl.when(jnp.logical_and(i % 2 == 0, i > 0))
    def _():
        step(sa_ref, sb_ref, 0, 1)


def _attention(qk, vt, lam_q, lam_k, attn_g, lam_init):
    nb, s, _ = qk.shape
    nq = s // ROW_TILE
    return pl.pallas_call(
        functools.partial(_attn_kernel, lam_init=lam_init),
        grid=(nb, ATT_HEADS, nq),
        in_specs=[
            pl.BlockSpec((2, ATT_DH), lambda b, h, i: (0, 0)),
            pl.BlockSpec((2, ATT_DH), lambda b, h, i: (0, 0)),
            pl.BlockSpec((1, ATT_DV, 1), lambda b, h, i: (h, 0, 0)),
            pl.BlockSpec((1, ROW_TILE, LANES), lambda b, h, i: (b, i, h)),
            pl.BlockSpec((1, ROW_TILE, LANES), lambda b, h, i: (b, jnp.minimum(i + 1, nq - 1), h)),
            pl.BlockSpec((1, s, LANES), lambda b, h, i: (b, 0, ATT_HEADS + h)),
            pl.BlockSpec((1, VT_ROWS, s), lambda b, h, i: (b, h, 0)),
        ],
        out_specs=pl.BlockSpec((1, ROW_TILE, LANES), lambda b, h, i: (b, i, h)),
        out_shape=jax.ShapeDtypeStruct((nb, s, BRANCH_W), BF16),
        scratch_shapes=[pltpu.VMEM((s, 2 * ROW_TILE), F32), pltpu.VMEM((s, 2 * ROW_TILE), F32),
                        pltpu.VMEM((SUBLANES, 2 * ROW_TILE), F32)],
        compiler_params=_cparams(("arbitrary", "arbitrary", "arbitrary")),
    )(lam_q, lam_k, attn_g.reshape(ATT_HEADS, ATT_DV, 1), qk, qk, qk, vt)


def _scan_chunk(j, nc, reverse):
    if not reverse:
        return j
    return jnp.where(j == 0, 0, nc - j)


def _conv4(x_ref, xp_ref, xn_ref, c, nc, w_ref, b_ref, bb=0):
    x = x_ref[bb].astype(F32)
    t = x.shape[0]
    row = lax.broadcasted_iota(jnp.int32, (SUBLANES, x.shape[1]), 0)
    prev_ok = (c >= 2).astype(F32)
    next_ok = jnp.logical_and(c >= 1, c < nc - 1).astype(F32)
    prev = xp_ref[bb].astype(F32)
    p1 = prev[HALO - 1:HALO, :] * prev_ok
    p2 = prev[HALO - 2:HALO - 1, :] * prev_ok
    n0 = xn_ref[bb].astype(F32)[0:1, :] * next_ok

    def patch_head(a, fix):
        return jnp.concatenate([fix(a[:SUBLANES, :]), a[SUBLANES:, :]], axis=0)

    def patch_tail(a, fix):
        return jnp.concatenate([a[:t - SUBLANES, :], fix(a[t - SUBLANES:, :])], axis=0)

    xm1 = patch_head(pltpu.roll(x, 1, 0), lambda a: jnp.where(row == 0, p1, a))
    xm2 = patch_head(pltpu.roll(x, 2, 0), lambda a: jnp.where(row == 0, p2, jnp.where(row == 1, p1, a)))
    xp1 = patch_tail(pltpu.roll(x, t - 1, 0), lambda a: jnp.where(row == SUBLANES - 1, n0, a))
    return w_ref[0:1, :] * xm2 + w_ref[1:2, :] * xm1 + w_ref[2:3, :] * x + w_ref[3:4, :] * xp1 + b_ref[...]


def _tile_specs(width, nc, reverse, blk=0, nrow=1):
    per = ROW_TILE // HALO
    last = nc * per - 1
    ch = lambda j: _scan_chunk(j, nc, reverse)
    cur = pl.BlockSpec((nrow, ROW_TILE, width), lambda b, j: (b, ch(j), blk))
    prev = pl.BlockSpec((nrow, HALO, width), lambda b, j: (b, jnp.maximum(ch(j) * per - 1, 0), blk))
    nxt = pl.BlockSpec((nrow, HALO, width), lambda b, j: (b, jnp.minimum((ch(j) + 1) * per, last), blk))
    return cur, prev, nxt


def _rglru_kernel(*refs, reverse, nc):
    if reverse:
        (x_ref, xp_ref, xn_ref, cw_ref, cb_ref, wg_ref, ba_ref, bi_ref, lam_ref,
         yf_ref, lg_ref, o_ref, h_ref) = refs
    else:
        (x_ref, xp_ref, xn_ref, cw_ref, cb_ref, wg_ref, ba_ref, bi_ref, lam_ref, o_ref, h_ref) = refs
    j = pl.program_id(1)
    c = _scan_chunk(j, nc, reverse)

    @pl.when(j == 0)
    def _():
        h_ref[...] = jnp.zeros_like(h_ref)

    for bb in range(x_ref.shape[0]):
        _rglru_tile(refs, bb, c, reverse, nc)


def _rglru_tile(refs, bb, c, reverse, nc):
    if reverse:
        (x_ref, xp_ref, xn_ref, cw_ref, cb_ref, wg_ref, ba_ref, bi_ref, lam_ref,
         yf_ref, lg_ref, o_ref, h_ref) = refs
    else:
        (x_ref, xp_ref, xn_ref, cw_ref, cb_ref, wg_ref, ba_ref, bi_ref, lam_ref, o_ref, h_ref) = refs
    u = _conv4(x_ref, xp_ref, xn_ref, c, nc, cw_ref, cb_ref, bb)
    t = u.shape[0]
    pre = jnp.dot(u.astype(BF16), wg_ref[...], preferred_element_type=F32)
    r = jax.nn.sigmoid(pre[:, :BRANCH_W] + ba_ref[...])
    gi = jax.nn.sigmoid(pre[:, BRANCH_W:] + bi_ref[...])
    log_a = (-LRU_C) * r * _softplus(-lam_ref[...])
    a = jnp.exp(log_a)
    var = 1.0 - jnp.exp(2.0 * log_a)
    bv = jnp.where(var > 0.0, var * lax.rsqrt(var), 0.0) * (gi * u)

    in_group = lax.broadcasted_iota(jnp.int32, u.shape, 0) % SUBLANES
    d = 1
    while d < SUBLANES:
        if reverse:
            keep = in_group < SUBLANES - d
            sh = t - d
        else:
            keep = in_group >= d
            sh = d
        a_s = jnp.where(keep, pltpu.roll(a, sh, 0), 1.0)
        b_s = jnp.where(keep, pltpu.roll(bv, sh, 0), 0.0)
        bv = a * b_s + bv
        a = a * a_s
        d *= 2
    carry = jnp.broadcast_to(h_ref[bb], (SUBLANES, BRANCH_W))
    ngroup = t // SUBLANES
    pieces = [None] * ngroup
    for gidx in (range(ngroup - 1, -1, -1) if reverse else range(ngroup)):
        rows = slice(gidx * SUBLANES, (gidx + 1) * SUBLANES)
        pieces[gidx] = a[rows, :] * carry + bv[rows, :]
        edge = pieces[gidx][0:1, :] if reverse else pieces[gidx][SUBLANES - 1:SUBLANES, :]
        carry = jnp.broadcast_to(edge, (SUBLANES, BRANCH_W))
    h_ref[bb] = carry[0:1, :]
    hs = jnp.concatenate(pieces, axis=0)
    if reverse:
        o_ref[bb] = ((yf_ref[bb] + hs) * _gelu_tanh(lg_ref[bb].astype(F32))).astype(BF16)
    else:
        o_ref[bb] = hs


def _rglru(lxg, conv_w, conv_b, w_gate, b_a, b_i, lam, reverse, yf=None):
    nb, s, _ = lxg.shape
    nc = s // ROW_TILE
    nrow = 2 if nb % 2 == 0 else 1
    cur, prev, nxt = _tile_specs(BRANCH_W, nc, reverse, nrow=nrow)
    params = [conv_w, conv_b, w_gate, b_a, b_i, lam]
    in_specs = [cur, prev, nxt] + [_resident(p.shape) for p in params]
    args = [lxg, lxg, lxg] + params
    if reverse:
        in_specs += [cur, _tile_specs(BRANCH_W, nc, reverse, blk=1, nrow=nrow)[0]]
        args += [yf, lxg]
    return pl.pallas_call(
        functools.partial(_rglru_kernel, reverse=reverse, nc=nc),
        grid=(nb // nrow, nc),
        in_specs=in_specs,
        out_specs=cur,
        out_shape=jax.ShapeDtypeStruct((nb, s, BRANCH_W), BF16 if reverse else F32),
        scratch_shapes=[pltpu.VMEM((nrow, 1, BRANCH_W), F32)],
        compiler_params=_cparams(("arbitrary", "arbitrary")),
    )(*args)


def _ssd_kernel(*refs, reverse, nc, direction):
    if reverse:
        (x_ref, xp_ref, xn_ref, dt_ref, dtt_ref, cw_ref, cb_ref, dtb_ref, dtbc_ref, alog_ref, alogc_ref,
         exp_ref, yf_ref, z_ref, skip_ref, g_ref, o_ref, st_ref) = refs
    else:
        (x_ref, xp_ref, xn_ref, dt_ref, dtt_ref, cw_ref, cb_ref, dtb_ref, dtbc_ref, alog_ref, alogc_ref,
         exp_ref, o_ref, st_ref) = refs
    j = pl.program_id(1)
    c = _scan_chunk(j, nc, reverse)

    @pl.when(j == 0)
    def _():
        st_ref[...] = jnp.zeros_like(st_ref)

    u = _silu(_conv4(x_ref, xp_ref, xn_ref, c, nc, cw_ref, cb_ref))
    t = u.shape[0]
    xs = u[:, :BRANCH_W]
    gw = SSD_HPG * SSD_HEADDIM

    dt_c = _softplus(dt_ref[0] + dtb_ref[...])
    a_c = dt_c * (-jnp.exp(alog_ref[...]))
    a_r = _softplus(dtt_ref[0] + dtbc_ref[...]) * (-jnp.exp(alogc_ref[...]))
    ri = lax.broadcasted_iota(jnp.int32, (t, t), 0)
    ci = lax.broadcasted_iota(jnp.int32, (t, t), 1)
    lower = (ci <= ri).astype(F32)
    upper = (ci >= ri).astype(F32)
    cs_c = jnp.dot(upper if reverse else lower, a_c, preferred_element_type=F32, precision=HIGHEST)
    cs_r = jnp.dot(a_r, lower if reverse else upper, preferred_element_type=F32, precision=HIGHEST)
    tot = cs_c[0:1, :] if reverse else cs_c[t - 1:t, :]
    keep = (ci >= ri) if reverse else (ci <= ri)

    expand = exp_ref[...]
    xdt = xs * jnp.dot(dt_c.astype(BF16), expand, preferred_element_type=F32)
    e_in = jnp.dot(jnp.exp(cs_c).astype(BF16), expand, preferred_element_type=F32)
    e_out = jnp.dot(jnp.exp(tot - cs_c).astype(BF16), expand, preferred_element_type=F32)
    e_tot = jnp.dot(jnp.broadcast_to(jnp.exp(tot), (SUBLANES, LANES)), expand.astype(F32),
                    preferred_element_type=F32, precision=HIGHEST)[0:1, :]
    xdt_b = xdt.astype(BF16)
    xdec_b = (xdt * e_out).astype(BF16)
    lane_head = lax.broadcasted_iota(jnp.int32, (t, gw), 1) // SSD_HEADDIM
    zero_b = jnp.zeros((t, gw), BF16)

    ys = []
    for g in range(SSD_GROUPS):
        bm = u[:, BRANCH_W + g * SSD_STATE:BRANCH_W + (g + 1) * SSD_STATE].astype(BF16)
        cm = u[:, BRANCH_W + (SSD_GROUPS + g) * SSD_STATE:BRANCH_W + (SSD_GROUPS + g + 1) * SSD_STATE].astype(BF16)
        cb = lax.dot_general(cm, bm, (((1,), (1,)), ((), ())), preferred_element_type=F32)
        st_g = st_ref[:, g * gw:(g + 1) * gw]
        y = jnp.dot(cm, st_g.astype(BF16), preferred_element_type=F32) * e_in[:, g * gw:(g + 1) * gw]
        xg = xdt_b[:, g * gw:(g + 1) * gw]
        for rr in range(SSD_HPG):
            col = direction * SSD_HEADS + g * SSD_HPG + rr
            decay = jnp.where(keep, jnp.exp(cs_c[:, col:col + 1] - cs_r[col:col + 1, :]), 0.0)
            gm = (cb * decay).astype(BF16)
            y = y + jnp.dot(gm, jnp.where(lane_head == rr, xg, zero_b), preferred_element_type=F32)
        ys.append(y)
        upd = lax.dot_general(bm, xdec_b[:, g * gw:(g + 1) * gw], (((0,), (0,)), ((), ())),
                              preferred_element_type=F32)
        st_ref[:, g * gw:(g + 1) * gw] = st_g * e_tot[:, g * gw:(g + 1) * gw] + upd
    y = jnp.concatenate(ys, axis=1)

    if reverse:
        y = yf_ref[0] + y + skip_ref[...] * xs
        tz = y * _silu(z_ref[0].astype(F32))
        outs = []
        for g in range(SSD_GROUPS):
            tg = tz[:, g * gw:(g + 1) * gw]
            outs.append(tg * lax.rsqrt(jnp.mean(tg * tg, axis=1, keepdims=True) + RMS_EPS))
        o_ref[0] = (jnp.concatenate(outs, axis=1) * g_ref[...]).astype(BF16)
    else:
        o_ref[0] = y


def _ssd(xbc, dt, dtt, p, direction, yf=None, z=None):
    reverse = direction == 1
    nb, s, _ = xbc.shape
    nc = s // ROW_TILE
    cur, prev, nxt = _tile_specs(2 * BRANCH_W, nc, reverse)
    ch = lambda j: _scan_chunk(j, nc, reverse)
    params = [p["conv_w"], p["conv_b"], p["dtb_row"], p["dtb_col"], p["alog_row"], p["alog_col"], p["expand"]]
    in_specs = [cur, prev, nxt,
                pl.BlockSpec((1, ROW_TILE, LANES), lambda b, j: (b, ch(j), 0)),
                pl.BlockSpec((1, 2 * SSD_HEADS, ROW_TILE), lambda b, j: (b, 0, ch(j)))]
    in_specs += [_resident(a.shape) for a in params]
    args = [xbc, xbc, xbc, dt, dtt] + params
    half = pl.BlockSpec((1, ROW_TILE, BRANCH_W), lambda b, j: (b, ch(j), 0))
    if reverse:
        in_specs += [half, half, _resident(p["skip"].shape), _resident(p["norm_g"].shape)]
        args += [yf, z, p["skip"], p["norm_g"]]
    return pl.pallas_call(
        functools.partial(_ssd_kernel, reverse=reverse, nc=nc, direction=direction),
        grid=(nb, nc),
        in_specs=in_specs,
        out_specs=half,
        out_shape=jax.ShapeDtypeStruct((nb, s, BRANCH_W), BF16 if reverse else F32),
        scratch_shapes=[pltpu.VMEM((SSD_STATE, BRANCH_W), F32)],
        compiler_params=_cparams(("arbitrary", "arbitrary")),
    )(*args)


def _merge_kernel(ya_ref, yb_ref, yc_ref, gt_ref, x_ref, m_ref, wb_ref, wo_ref, g_ref, b_ref,
                  x1_ref, h2_ref, *, alpha):
    acc = None
    for n, y_ref in enumerate((ya_ref, yb_ref, yc_ref)):
        proj = jnp.dot(y_ref[0], wb_ref[n], preferred_element_type=F32)
        gate = jax.nn.sigmoid(gt_ref[0, :, n * D_MODEL:(n + 1) * D_MODEL].astype(F32))
        acc = gate * proj if acc is None else acc + gate * proj
    mix = jnp.dot(acc.astype(BF16), wo_ref[...], preferred_element_type=F32)
    x1 = _layer_norm(alpha * x_ref[0] + m_ref[0, 2:3, :] * mix, g_ref[...], b_ref[...])
    x1_ref[0] = x1
    _store_token_rows(h2_ref.at[0], x1 * (1.0 + m_ref[0, 4:5, :]) + m_ref[0, 3:4, :])


def _zero_tile_kernel(h_ref, o_ref):
    del h_ref
    o_ref[...] = jnp.zeros_like(o_ref)


def _merge(ya, yb, yc, gates, xc, mods, w_branch, w_out, ln_g, ln_b, alpha):
    nb, s, _ = xc.shape
    nt = s // ROW_TILE
    row = lambda width: pl.BlockSpec((1, ROW_TILE, width), lambda b, i: (b, i, 0))
    h2_shape = jax.ShapeDtypeStruct((nb, (s + MOE_PAD) * NCHUNK, LANES), F32)
    x1, h2 = pl.pallas_call(
        functools.partial(_merge_kernel, alpha=alpha),
        grid=(nb, nt),
        in_specs=[row(BRANCH_W), row(BRANCH_W), row(BRANCH_W), row(3 * D_MODEL), row(D_MODEL), _mod_spec(nb),
                  _resident(w_branch.shape), _resident(w_out.shape), _resident(ln_g.shape), _resident(ln_b.shape)],
        out_specs=[row(D_MODEL), pl.BlockSpec((1, ROW_TILE * NCHUNK, LANES), lambda b, i: (b, i, 0))],
        out_shape=[jax.ShapeDtypeStruct((nb, s, D_MODEL), F32), h2_shape],
        compiler_params=_cparams(("arbitrary", "arbitrary")),
    )(ya, yb, yc, gates, xc, mods, w_branch, w_out, ln_g, ln_b)
    h2 = pl.pallas_call(
        _zero_tile_kernel,
        grid=(nb,),
        in_specs=[pl.BlockSpec(memory_space=pl.ANY)],
        out_specs=pl.BlockSpec((1, NCHUNK, LANES), lambda b: (b, s, 0)),
        out_shape=h2_shape,
        input_output_aliases={0: 0},
        compiler_params=_cparams(("arbitrary",)),
    )(h2)
    return x1, h2


def _first_index(hit, ridx, n):
    return jnp.min(jnp.where(hit, ridx, n), axis=0, keepdims=True)


def _route_kernel(h_ref, wr_ref, bias_ref, cnt_ref, rank_ref, wgt_ref, base_ref, *, latent_only):
    nrow, _, tile_tokens = wgt_ref.shape
    tm = nrow * tile_tokens

    @pl.when(pl.program_id(1) == 0)
    def _():
        base_ref[...] = jnp.zeros_like(base_ref)

    logits = jnp.concatenate(
        [lax.dot_general(wr_ref[...], _load_token_rows(h_ref.at[bb], tile_tokens), (((1,), (1,)), ((), ())),
                         preferred_element_type=F32, precision=HIGHEST) for bb in range(nrow)], axis=1)
    scores = jax.nn.sigmoid(logits)
    sel = scores + bias_ref[...]
    neg = -jnp.inf

    r8 = lax.broadcasted_iota(jnp.int32, (GROUP_SIZE, tm), 0)
    grp = jnp.zeros((N_GROUPS, tm), F32)
    for g in range(N_GROUPS):
        blk = sel[g * GROUP_SIZE:(g + 1) * GROUP_SIZE, :]
        m1 = jnp.max(blk, axis=0, keepdims=True)
        first = _first_index(blk == m1, r8, GROUP_SIZE)
        m2 = jnp.max(jnp.where(r8 == first, neg, blk), axis=0, keepdims=True)
        grp = jnp.where(r8 == g, m1 + m2, grp)
    gsel = jnp.zeros((N_GROUPS, tm), jnp.int32)
    for _ in range(TOPK_GROUPS):
        m = jnp.max(grp, axis=0, keepdims=True)
        hit = r8 == _first_index(grp == m, r8, N_GROUPS)
        gsel = jnp.where(hit, 1, gsel)
        grp = jnp.where(hit, neg, grp)
    cand = jnp.concatenate(
        [jnp.where(gsel[g:g + 1, :] > 0, sel[g * GROUP_SIZE:(g + 1) * GROUP_SIZE, :], neg) for g in range(N_GROUPS)],
        axis=0)

    re = lax.broadcasted_iota(jnp.int32, (N_EXPERTS, tm), 0)
    chosen = jnp.zeros((N_EXPERTS, tm), jnp.int32)
    for _ in range(TOP_K):
        m = jnp.max(cand, axis=0, keepdims=True)
        hit = re == _first_index(cand == m, re, N_EXPERTS)
        chosen = jnp.where(hit, 1, chosen)
        cand = jnp.where(hit, neg, cand)
    picked = jnp.where(chosen > 0, scores, 0.0)
    wgt = picked / jnp.sum(picked, axis=0, keepdims=True) * ROUTED_SCALE
    if latent_only:
        chosen = jnp.where(pl.program_id(1) == 0, 0, chosen)

    li = lax.broadcasted_iota(jnp.int32, (LANES, LANES), 0)
    lj = lax.broadcasted_iota(jnp.int32, (LANES, LANES), 1)
    strict = (li < lj).astype(BF16)
    chosen_b = chosen.astype(F32).astype(BF16)
    for bb in range(nrow):
        base = base_ref[bb]
        pos = []
        for kb in range(bb * tile_tokens // LANES, (bb + 1) * tile_tokens // LANES):
            cblk = chosen_b[:, kb * LANES:(kb + 1) * LANES]
            pos.append(jnp.dot(cblk, strict, preferred_element_type=F32) + base)
            base = base + jnp.sum(cblk.astype(F32), axis=1, keepdims=True)
        base_ref[bb] = base
        cols = slice(bb * tile_tokens, (bb + 1) * tile_tokens)
        cnt_ref[bb] = base.astype(jnp.int32)
        rank_ref[bb] = jnp.where(chosen[:, cols] > 0, jnp.concatenate(pos, axis=1).astype(jnp.int32), -1)
        wgt_ref[bb] = wgt[:, cols]


def _compact_kernel(cnt_ref, rank_ref, wgt_ref, idx_ref, wl_ref):
    rank = rank_ref[0]
    ne, tm = rank.shape
    lane = lax.broadcasted_iota(jnp.int32, (ne, tm), 1)
    dist = jnp.where(rank >= 0, lane - rank, 0)
    val = lane
    wv = wgt_ref[0]
    step = 1
    while step < tm:
        move = dist & step
        sh = tm - step
        take = pltpu.roll(move, sh, 1) != 0
        val = jnp.where(take, pltpu.roll(val, sh, 1), val)
        wv = jnp.where(take, pltpu.roll(wv, sh, 1), wv)
        dist = jnp.where(take, pltpu.roll(dist, sh, 1), jnp.where(move != 0, 0, dist))
        step *= 2
    valid = lane < cnt_ref[0]
    spare = jnp.full((ne, MOE_SPARE), tm * NCHUNK, jnp.int32)
    idx_ref[0] = jnp.concatenate([jnp.where(valid, val * NCHUNK, tm * NCHUNK), spare], axis=1)
    wl_ref[0] = jnp.where(valid, wv, 0.0)


def _router(h2, s, w_router_t, bias_col, latent_only):
    nb = h2.shape[0]
    nrow = 2 if nb % 2 == 0 else 1
    dense = pl.BlockSpec((nrow, N_EXPERTS, ROW_TILE), lambda b, i: (b, 0, i))
    counts, rank, wgt = pl.pallas_call(
        functools.partial(_route_kernel, latent_only=latent_only),
        grid=(nb // nrow, s // ROW_TILE),
        in_specs=[pl.BlockSpec((nrow, ROW_TILE * NCHUNK, LANES), lambda b, i: (b, i, 0)),
                  _resident(w_router_t.shape), _resident(bias_col.shape)],
        out_specs=[pl.BlockSpec((nrow, N_EXPERTS, 1), lambda b, i: (b, 0, 0)), dense, dense],
        out_shape=[jax.ShapeDtypeStruct((nb, N_EXPERTS, 1), jnp.int32),
                   jax.ShapeDtypeStruct((nb, N_EXPERTS, s), jnp.int32),
                   jax.ShapeDtypeStruct((nb, N_EXPERTS, s), F32)],
        scratch_shapes=[pltpu.VMEM((nrow, N_EXPERTS, 1), F32)],
        compiler_params=_cparams(("arbitrary", "arbitrary")),
    )(h2, w_router_t, bias_col)
    rows = lambda width: pl.BlockSpec((1, SUBLANES, width), lambda b, g: (b, g, 0))
    idx, wl = pl.pallas_call(
        _compact_kernel,
        grid=(nb, N_EXPERTS // SUBLANES),
        in_specs=[rows(1), rows(s), rows(s)],
        out_specs=[rows(s + MOE_SPARE), rows(s)],
        out_shape=[jax.ShapeDtypeStruct((nb, N_EXPERTS, s + MOE_SPARE), jnp.int32),
                   jax.ShapeDtypeStruct((nb, N_EXPERTS, s), F32)],
        compiler_params=_cparams(("arbitrary", "arbitrary")),
    )(counts, rank, wgt)
    return counts, idx, wl


SCATTER_GROUP = 16


def _experts_kernel(cnt_ref, idx_ref, wl_ref, h_ref, wu_ref, wd_ref, o_ref,
                    xa_ref, xb_ref, xc_ref, ya_ref, yb_ref, yc_ref):
    t = pl.program_id(0)
    e = pl.program_id(1)
    h_rows = h_ref.at[0]
    acc_ref = o_ref.at[0]

    @pl.when(e == 0)
    def _():
        o_ref[...] = jnp.zeros_like(o_ref)

    count = cnt_ref[t * N_EXPERTS + e]
    nblk = (count + MOE_BLOCK - 1) // MOE_BLOCK
    ri = lax.broadcasted_iota(jnp.int32, (MOE_BLOCK, MOE_BLOCK), 0)
    ci = lax.broadcasted_iota(jnp.int32, (MOE_BLOCK, MOE_BLOCK), 1)

    def gather(blk, x_ref):
        ids = idx_ref.at[0, 0, pl.ds(blk * MOE_BLOCK, MOE_BLOCK)]
        for r in range(MOE_BLOCK):
            x_ref[pl.ds(r * NCHUNK, NCHUNK), :] = h_rows[pl.ds(pl.multiple_of(ids[r], NCHUNK), NCHUNK), :]

    def swiglu(x, blks, y_refs):
        gu = jnp.dot(x.astype(BF16), wu_ref[0], preferred_element_type=F32)
        act = _silu(gu[:, :EXPERT_F]) * gu[:, EXPERT_F:]
        y = jnp.dot(act.astype(BF16), wd_ref[0], preferred_element_type=F32)
        for k, (blk, y_ref) in enumerate(zip(blks, y_refs)):
            w_row = wl_ref[0, pl.ds(blk, 1), :]
            w_col = jnp.sum(jnp.where(ri == ci, w_row, 0.0), axis=1, keepdims=True)
            _store_token_rows(y_ref, y[k * MOE_BLOCK:(k + 1) * MOE_BLOCK, :] * w_col)

    def scatter(blk, y_ref):
        ids = idx_ref.at[0, 0, pl.ds(blk * MOE_BLOCK, MOE_BLOCK)]
        for g0 in range(0, MOE_BLOCK, SCATTER_GROUP):
            rows = range(g0, g0 + SCATTER_GROUP)
            dst = [pl.multiple_of(ids[r], NCHUNK) for r in rows]
            new = [acc_ref[pl.ds(d, NCHUNK), :] + y_ref[pl.ds(r * NCHUNK, NCHUNK), :] for d, r in zip(dst, rows)]
            for d, v in zip(dst, new):
                acc_ref[pl.ds(d, NCHUNK), :] = v

    odd = nblk % 2 == 1
    nloop = jnp.where(jnp.logical_and(odd, nblk >= 3), nblk // 2 - 1, nblk // 2)

    @pl.when(nblk > 0)
    def _():
        gather(0, xa_ref)

    @pl.when(nblk >= 2)
    def _():
        gather(1, xb_ref)
        ya_ref[...] = jnp.zeros_like(ya_ref)
        yb_ref[...] = jnp.zeros_like(yb_ref)

    def load_rows(*x_refs):
        return jnp.concatenate([_load_token_rows(r, MOE_BLOCK) for r in x_refs], axis=0)

    def pair(j, carry):
        b0 = 2 * j
        x = load_rows(xa_ref, xb_ref)
        scatter(jnp.maximum(b0 - 2, 0), ya_ref)
        scatter(jnp.maximum(b0 - 1, 0), yb_ref)
        gather(b0 + 2, xa_ref)
        gather(b0 + 3, xb_ref)
        swiglu(x, (b0, b0 + 1), (ya_ref, yb_ref))
        return carry

    lax.fori_loop(0, nloop, pair, 0)
    b0 = 2 * nloop

    @pl.when(jnp.logical_and(odd, nblk >= 3))
    def _():
        gather(b0 + 2, xc_ref)
        x = load_rows(xa_ref, xb_ref, xc_ref)
        scatter(jnp.maximum(b0 - 2, 0), ya_ref)
        scatter(jnp.maximum(b0 - 1, 0), yb_ref)
        swiglu(x, (b0, b0 + 1, b0 + 2), (ya_ref, yb_ref, yc_ref))
        scatter(b0, ya_ref)
        scatter(b0 + 1, yb_ref)
        scatter(b0 + 2, yc_ref)

    @pl.when(jnp.logical_and(jnp.logical_not(odd), nblk > 0))
    def _():
        scatter(nblk - 2, ya_ref)
        scatter(nblk - 1, yb_ref)

    @pl.when(nblk == 1)
    def _():
        swiglu(load_rows(xa_ref), (0,), (ya_ref,))
        scatter(0, ya_ref)


def _experts(h2, counts, idx, wl, w_up, w_down):
    nb, rows, _ = h2.shape
    nlist = idx.shape[-1]
    tile = pl.BlockSpec((1, rows, LANES), lambda t, e, c: (t, 0, 0), pipeline_mode=pl.Buffered(1))
    buf = pltpu.VMEM((MOE_BLOCK * NCHUNK, LANES), F32)
    grid_spec = pltpu.PrefetchScalarGridSpec(
        num_scalar_prefetch=1,
        grid=(nb, N_EXPERTS),
        in_specs=[pl.BlockSpec((1, 1, nlist), lambda t, e, c: (t * N_EXPERTS + e, 0, 0), memory_space=pltpu.SMEM),
                  pl.BlockSpec((1,) + wl.shape[1:], lambda t, e, c: (t * N_EXPERTS + e, 0, 0)),
                  tile,
                  pl.BlockSpec((1, D_MODEL, 2 * EXPERT_F), lambda t, e, c: (e, 0, 0)),
                  pl.BlockSpec((1, EXPERT_F, D_MODEL), lambda t, e, c: (e, 0, 0))],
        out_specs=tile,
        scratch_shapes=[buf] * 6,
    )
    return pl.pallas_call(
        _experts_kernel,
        grid_spec=grid_spec,
        out_shape=jax.ShapeDtypeStruct(h2.shape, F32),
        compiler_params=_cparams(("arbitrary", "arbitrary")),
    )(counts, idx, wl, h2, w_up, w_down)


def _ffn_out_kernel(x1_ref, fr_ref, m_ref, wu_ref, wd_ref, g_ref, b_ref, o_ref, *, alpha):
    x1 = x1_ref[0]
    h2 = x1 * (1.0 + m_ref[0, 4:5, :]) + m_ref[0, 3:4, :]
    gu = jnp.dot(h2.astype(BF16), wu_ref[...], preferred_element_type=F32)
    act = _silu(gu[:, :EXPERT_F]) * gu[:, EXPERT_F:]
    f = (jnp.dot(act.astype(BF16), wd_ref[...], preferred_element_type=F32)
         + _load_token_rows(fr_ref.at[0], ROW_TILE))
    o_ref[0] = _layer_norm(alpha * x1 + m_ref[0, 5:6, :] * f, g_ref[...], b_ref[...])


def _ffn_out(x1, fr, mods, ws_up, ws_down, ln_g, ln_b, alpha, latent_only):
    nb, s, _ = x1.shape
    nt = s // ROW_TILE
    skip = CTX_LEN // ROW_TILE if latent_only else 0
    row = pl.BlockSpec((1, ROW_TILE, D_MODEL), lambda b, i: (b, i + skip, 0))
    mod = pl.BlockSpec((1, 6, D_MODEL), lambda b, i: (jnp.where(i + skip == 0, nb, b), 0, 0))
    chunked = pl.BlockSpec((1, ROW_TILE * NCHUNK, LANES), lambda b, i: (b, i + skip, 0))
    return pl.pallas_call(
        functools.partial(_ffn_out_kernel, alpha=alpha),
        grid=(nb, nt - skip),
        in_specs=[row, chunked, mod, _resident(ws_up.shape), _resident(ws_down.shape),
                  _resident(ln_g.shape), _resident(ln_b.shape)],
        out_specs=pl.BlockSpec((1, ROW_TILE, D_MODEL), lambda b, i: (b, i, 0)),
        out_shape=jax.ShapeDtypeStruct((nb, s - skip * ROW_TILE, D_MODEL), F32),
        compiler_params=_cparams(("arbitrary", "arbitrary")),
    )(x1, fr, mods, ws_up, ws_down, ln_g, ln_b)


def _rope_tables(n_lat):
    t = jnp.arange(n_lat)
    rowp = (t // GRID_W).astype(F32)
    colp = (t % GRID_W).astype(F32)
    n_freq = ATT_DH // 4
    inv = ROPE_BASE ** (-jnp.arange(n_freq, dtype=F32) / n_freq)
    ang = jnp.concatenate([rowp[:, None] * inv, colp[:, None] * inv], axis=-1)
    lane = jnp.arange(LANES)
    cos = jnp.cos(ang)[:, lane % (ATT_DH // 2)]
    sign = jnp.where((lane % ATT_DH) < ATT_DH // 2, -1.0, 1.0).astype(F32)
    sin = jnp.sin(ang)[:, lane % (ATT_DH // 2)] * sign
    cos = jnp.concatenate([jnp.ones((CTX_LEN, LANES), F32), cos], axis=0)
    sin = jnp.concatenate([jnp.zeros((CTX_LEN, LANES), F32), sin], axis=0)
    return cos, sin


def _block_diag(w):
    n, k, _ = w.shape
    eye = jnp.eye(n, dtype=w.dtype)
    return (eye[:, None, :, None] * w[:, :, None, :]).reshape(n * k, n * k)


def _inproj_weights(w):
    o_v = 2 * BRANCH_W
    o_lx = o_v + ATT_HEADS * ATT_DV
    o_z = o_lx + 2 * BRANCH_W
    o_xbc = o_z + BRANCH_W
    o_dt = o_xbc + 2 * BRANCH_W
    o_g = o_dt + 2 * SSD_HEADS
    wv = w[:, o_v:o_lx].T.reshape(ATT_HEADS, ATT_DV, D_MODEL)
    wv = jnp.pad(wv, ((0, 0), (0, VT_ROWS - ATT_DV), (0, 0))).reshape(ATT_HEADS * VT_ROWS, D_MODEL)
    ones = jnp.zeros((ATT_HEADS, VT_ROWS, 1), F32).at[:, ATT_DV, 0].set(1.0).reshape(ATT_HEADS * VT_ROWS, 1)
    wdt = w[:, o_dt:o_g]
    return {
        "qk": w[:, :o_v].astype(BF16),
        "vt": wv.astype(BF16),
        "ones": ones,
        "lxg": w[:, o_lx:o_z].astype(BF16),
        "z": w[:, o_z:o_xbc].astype(BF16),
        "xbc": w[:, o_xbc:o_dt].astype(BF16),
        "dt": jnp.pad(wdt, ((0, 0), (0, LANES - 2 * SSD_HEADS))).astype(BF16),
        "dtt": wdt.T.astype(BF16),
        "g": w[:, o_g:].astype(BF16),
    }


def _pad_row(v):
    return jnp.pad(v.reshape(1, -1), ((0, 0), (0, LANES - v.size)))


def kernel(x, c, ctx, c_ctx, w_mod, b_mod, w_in, lam_q, lam_k, attn_norm_g, lru_conv_w, lru_conv_b, lru_wa, lru_ba, lru_wi, lru_bi, lru_lambda, ssd_conv_w, ssd_conv_b, ssd_dt_bias, ssd_a_log, ssd_d, ssd_norm_g, w_branch, w_out, ln1_g, ln1_b, w_router, router_bias, w_up, w_down, ws_up, ws_down, ln2_g, ln2_b):
    nb, n_lat, _ = x.shape
    depth = w_mod.shape[0]
    assert ctx.shape[1] == CTX_LEN and n_lat % ROW_TILE == 0 and nb + 1 <= 16
    s = CTX_LEN + n_lat
    alpha = (2 * depth) ** 0.25

    xc = jnp.concatenate([ctx, x], axis=1)
    cc = jnp.zeros((16, D_MODEL), F32).at[:nb].set(c).at[nb].set(c_ctx)
    mods_all = _modulation(cc, w_mod, b_mod).reshape(depth, 16, 6, D_MODEL)
    cos_t, sin_t = _rope_tables(n_lat)
    head_of_channel = jnp.arange(BRANCH_W) // SSD_HEADDIM

    for l in range(depth):
        last = l == depth - 1
        lam_init = 0.8 - 0.6 * math.exp(-0.3 * l)
        mods = mods_all[l]
        qk, vt, lxg, z, xbc, dt, dtt, gates = _inproj(xc, mods, cos_t, sin_t, _inproj_weights(w_in[l]))

        ya = _attention(qk, vt, lam_q[l], lam_k[l], attn_norm_g[l], lam_init)

        yb = None
        for d in range(2):
            w_gate = jnp.concatenate([_block_diag(lru_wa[l, d]), _block_diag(lru_wi[l, d])], axis=1).astype(BF16)
            yb = _rglru(lxg, lru_conv_w[l], lru_conv_b[l].reshape(1, -1), w_gate,
                        lru_ba[l, d].reshape(1, -1), lru_bi[l, d].reshape(1, -1), lru_lambda[l, d].reshape(1, -1),
                        reverse=(d == 1), yf=yb)

        ssd_p = {
            "conv_w": ssd_conv_w[l], "conv_b": ssd_conv_b[l].reshape(1, -1),
            "dtb_row": _pad_row(ssd_dt_bias[l]), "dtb_col": ssd_dt_bias[l].reshape(-1, 1),
            "alog_row": _pad_row(ssd_a_log[l]), "alog_col": ssd_a_log[l].reshape(-1, 1),
            "skip": jnp.repeat(ssd_d[l], SSD_HEADDIM).reshape(1, -1), "norm_g": ssd_norm_g[l].reshape(1, -1),
        }
        yc = None
        for d in range(2):
            ssd_p["expand"] = (jnp.arange(LANES)[:, None] == d * SSD_HEADS + head_of_channel[None, :]).astype(BF16)
            yc = _ssd(xbc, dt, dtt, ssd_p, d, yf=yc, z=z)

        x1, h2 = _merge(ya, yb, yc, gates, xc, mods, w_branch[l].astype(BF16), w_out[l].astype(BF16),
                        ln1_g[l].reshape(1, -1), ln1_b[l].reshape(1, -1), alpha)

        counts, idx, wl = _router(h2, s, w_router[l].T, router_bias[l].reshape(-1, 1), latent_only=last)
        fr = _experts(h2, counts.reshape(nb * N_EXPERTS), idx.reshape(nb * N_EXPERTS, 1, s + MOE_SPARE),
                      wl.reshape(nb * N_EXPERTS, s // MOE_BLOCK, MOE_BLOCK),
                      w_up[l].astype(BF16), w_down[l].astype(BF16))
        xc = _ffn_out(x1, fr, mods, ws_up[l].astype(BF16), ws_down[l].astype(BF16),
                      ln2_g[l].reshape(1, -1), ln2_b[l].reshape(1, -1), alpha, latent_only=last)
    return xc
```
